```python
import jax, jax.numpy as jnp
from jax import lax
import numpy as np

D_MODEL = 1024
BATCH = 4
SEQ = 8192
DEPTH = 1
DEC_BATCH = 16
DEC_SEQ = 16
PAST_LEN = 1024

CHUNK = 64
D_MIX = 2 * D_MODEL
D_LRU = D_MIX // 2
LRU_BLOCKS = 16
LRU_BLOCK = D_LRU // LRU_BLOCKS
LRU_C = 8.0
CONV_W = 4
D_SSM = D_MIX - D_LRU
SSD_HEADDIM = 64
SSD_HEADS = D_SSM // SSD_HEADDIM
SSD_GROUPS = 2
SSD_HPG = SSD_HEADS // SSD_GROUPS
D_STATE = 128
D_XBC = D_SSM + 2 * SSD_GROUPS * D_STATE
N_IN = 2 * D_LRU + D_SSM + D_XBC + SSD_HEADS
MOE_GROUPS = 4
EXPERTS_PER_GROUP = 8
N_EXPERTS = MOE_GROUPS * EXPERTS_PER_GROUP
TOP_K = 2
D_EXPERT = 512
EPS = 1e-6

kernel_name = "hymba_rglru_ssd_hmoe_stream_step"


def rmsnorm(x, g):
    xf = x.astype(jnp.float32)
    y = xf * lax.rsqrt(jnp.mean(xf * xf, axis=-1, keepdims=True) + EPS)
    return (y * g.astype(jnp.float32)).astype(x.dtype)


def causal_conv(x, prev, w, b):
    l = x.shape[1]
    xp = jnp.concatenate([prev.astype(x.dtype), x], axis=1)
    y = b + xp[:, 0:l] * w[0]
    for k in range(1, CONV_W):
        y = y + xp[:, k:k + l] * w[k]
    return y, xp[:, -(CONV_W - 1):]


def _lin_combine(left, right):
    a1, b1 = left
    a2, b2 = right
    return a1 * a2, a2 * b1 + b2


def rg_lru(x, h0, wa, ba, wx, bx, lam, start_pos):
    f32 = jnp.float32
    b, l, _ = x.shape
    xf = x.astype(f32)
    xb = xf.reshape(b, l, LRU_BLOCKS, LRU_BLOCK)
    r = jax.nn.sigmoid(jnp.einsum('blhi,hij->blhj', xb, wa.astype(f32)).reshape(b, l, D_LRU) + ba.astype(f32))
    i = jax.nn.sigmoid(jnp.einsum('blhi,hij->blhj', xb, wx.astype(f32)).reshape(b, l, D_LRU) + bx.astype(f32))
    log_a = -LRU_C * r * jax.nn.softplus(-lam.astype(f32))
    a = jnp.exp(log_a)
    mult = jnp.sqrt(-jnp.expm1(2.0 * log_a))
    pos = start_pos + jnp.arange(l)
    mult = jnp.where((pos == 0)[None, :, None], 1.0, mult)
    u = mult * i * xf
    u = u.at[:, 0].add(a[:, 0] * h0.astype(f32))
    _, h = lax.associative_scan(_lin_combine, (a, u), axis=1)
    return h, h[:, -1]


def ssd_scan(x, dt, A, B, C, h0):
    b, l = x.shape[:2]
    q = min(CHUNK, l)
    c = l // q
    G, M, P, N = SSD_GROUPS, SSD_HPG, SSD_HEADDIM, D_STATE
    x = x.reshape(b, c, q, G, M, P)
    dt = dt.reshape(b, c, q, G, M)
    B = B.reshape(b, c, q, G, N)
    C = C.reshape(b, c, q, G, N)
    cs = jnp.cumsum(dt * A.reshape(G, M), axis=2)
    cst = jnp.moveaxis(cs, 2, -1)
    seg = cst[..., :, None] - cst[..., None, :]
    mask = jnp.tril(jnp.ones((q, q), dtype=bool))
    L = jnp.where(mask, jnp.exp(jnp.where(mask, seg, 0.0)), 0.0)
    CB = jnp.einsum('bcqgn,bckgn->bcgqk', C, B)
    Mw = CB[:, :, :, None] * L * jnp.moveaxis(dt, 2, -1)[..., None, :]
    y_diag = jnp.einsum('bcgmqk,bckgmp->bcqgmp', Mw, x)
    decay_end = jnp.exp(cs[:, :, -1:] - cs)
    states = jnp.einsum('bckgn,bckgmp->bcgmpn', B, (decay_end * dt)[..., None] * x)
    chunk_decay = jnp.exp(cs[:, :, -1])

    def step(S, inp):
        dec, st = inp
        return dec[..., None, None] * S + st, S

    final, S_in = lax.scan(step, h0.reshape(b, G, M, P, N),
                           (jnp.moveaxis(chunk_decay, 1, 0), jnp.moveaxis(states, 1, 0)))
    S_in = jnp.moveaxis(S_in, 0, 1)
    y_off = jnp.einsum('bcqgn,bcgmpn->bcqgmp', C, S_in) * jnp.exp(cs)[..., None]
    y = (y_diag + y_off).reshape(b, l, SSD_HEADS, P)
    return y, final.reshape(b, SSD_HEADS, P, N)


def mixer(xn, lru_h0, lru_c0, ssd_h0, ssd_c0, p, start_pos):
    f32 = jnp.float32
    b, l, _ = xn.shape
    proj = xn @ p['w_in']
    i1 = D_LRU
    i2 = 2 * D_LRU
    i3 = i2 + D_SSM
    i4 = i3 + D_XBC
    x_lru, g_lru, z, xbc, dt_raw = jnp.split(proj, [i1, i2, i3, i4], axis=-1)
    xc, lru_c_new = causal_conv(x_lru, lru_c0, p['lru_conv_w'], p['lru_conv_b'])
    h, lru_h_new = rg_lru(xc, lru_h0, p['lru_wa'], p['lru_ba'], p['lru_wx'], p['lru_bx'], p['lru_lambda'], start_pos)
    y_lru = h * jax.nn.gelu(g_lru.astype(f32), approximate=True)
    xbc_c, ssd_c_new = causal_conv(xbc, ssd_c0, p['ssd_conv_w'], p['ssd_conv_b'])
    xbc_c = jax.nn.silu(xbc_c.astype(f32))
    xs, Bm, Cm = jnp.split(xbc_c, [D_SSM, D_SSM + SSD_GROUPS * D_STATE], axis=-1)
    xs = xs.reshape(b, l, SSD_HEADS, SSD_HEADDIM)
    dt = jax.nn.softplus(dt_raw.astype(f32) + p['ssd_dt_bias'].astype(f32))
    A = -jnp.exp(p['ssd_a_log'].astype(f32))
    y, ssd_h_new = ssd_scan(xs, dt, A, Bm.reshape(b, l, SSD_GROUPS, D_STATE),
                            Cm.reshape(b, l, SSD_GROUPS, D_STATE), ssd_h0.astype(f32))
    y = y + p['ssd_d'].astype(f32)[:, None] * xs
    y = y.reshape(b, l, D_SSM) * jax.nn.silu(z.astype(f32))
    yg = y.reshape(b, l, SSD_GROUPS, D_SSM // SSD_GROUPS)
    yg = yg * lax.rsqrt(jnp.mean(yg * yg, axis=-1, keepdims=True) + EPS)
    y_ssd = yg.reshape(b, l, D_SSM) * p['ssd_norm_g'].astype(f32)
    mix = jnp.concatenate([y_lru, y_ssd], axis=-1).astype(xn.dtype)
    out = mix @ p['w_out']
    return out, lru_h_new, lru_c_new, ssd_h_new, ssd_c_new


def hier_moe(xn, p):
    f32 = jnp.float32
    b, l, d = xn.shape
    t = xn.reshape(-1, d)
    tf = t.astype(f32)
    pg = jax.nn.softmax(tf @ p['router_group_w'].astype(f32) + p['router_group_b'].astype(f32), axis=-1)
    gsel = jnp.argmax(pg, axis=-1)
    pgs = jnp.take_along_axis(pg, gsel[:, None], axis=-1)
    le = (tf @ p['router_expert_w'].astype(f32) + p['router_expert_b'].astype(f32)).reshape(-1, MOE_GROUPS, EXPERTS_PER_GROUP)
    le = jnp.take_along_axis(le, gsel[:, None, None], axis=1)[:, 0]
    pe = jax.nn.softmax(le, axis=-1)
    topv, topi = lax.top_k(pe, TOP_K)
    topv = topv / jnp.sum(topv, axis=-1, keepdims=True) * pgs
    eidx = gsel[:, None] * EXPERTS_PER_GROUP + topi
    gates = jnp.einsum('tk,tke->et', topv, jax.nn.one_hot(eidx, N_EXPERTS, dtype=f32))

    def expert(acc, ws):
        w1, w3, w2, ge = ws
        hdn = jax.nn.silu(t @ w1) * (t @ w3)
        return acc + ge[:, None] * (hdn @ w2).astype(f32), None

    out, _ = lax.scan(expert, jnp.zeros(t.shape, f32), (p['moe_w1'], p['moe_w3'], p['moe_w2'], gates))
    return out.reshape(b, l, d).astype(xn.dtype)


def block(x, lru_h0, lru_c0, ssd_h0, ssd_c0, p, start_pos):
    mo, lru_h, lru_c, ssd_h, ssd_c = mixer(rmsnorm(x, p['norm_mix_g']), lru_h0, lru_c0, ssd_h0, ssd_c0, p, start_pos)
    h = x + mo.astype(x.dtype)
    h = h + hier_moe(rmsnorm(h, p['norm_ffn_g']), p)
    dt = x.dtype
    return h, lru_h.astype(dt), lru_c.astype(dt), ssd_h.astype(dt), ssd_c.astype(dt)


def setup_inputs(seed: int = 0) -> dict:
    key = jax.random.key(seed)
    ks = jax.random.split(key, 40)
    f32 = jnp.float32
    nrm = lambda k, s, sc: jax.random.normal(k, s, f32) * sc
    a_c = jax.random.uniform(ks[10], (DEPTH, D_LRU), f32, 0.9, 0.999)
    a0 = a_c ** (1.0 / LRU_C)
    lru_lambda = jnp.log(a0) - jnp.log1p(-a0)
    dt0 = jnp.exp(jax.random.uniform(ks[13], (DEPTH, SSD_HEADS), f32, np.log(1e-3), np.log(1e-1)))
    ssd_dt_bias = dt0 + jnp.log(-jnp.expm1(-dt0))
    ssd_a_log = jnp.log(jax.random.uniform(ks[14], (DEPTH, SSD_HEADS), f32, 1.0, 16.0))
    return {
        'x_prompt': nrm(ks[0], (BATCH, SEQ, D_MODEL), 1.0),
        'x_sample': nrm(ks[1], (DEC_BATCH, DEC_SEQ, D_MODEL), 1.0),
        'state_lru_h': nrm(ks[2], (DEPTH, DEC_BATCH, D_LRU), 0.5),
        'state_lru_conv': nrm(ks[3], (DEPTH, DEC_BATCH, CONV_W - 1, D_LRU), 1.0),
        'state_ssd': nrm(ks[4], (DEPTH, DEC_BATCH, SSD_HEADS, SSD_HEADDIM, D_STATE), 0.1),
        'state_ssd_conv': nrm(ks[5], (DEPTH, DEC_BATCH, CONV_W - 1, D_XBC), 1.0),
        'norm_mix_g': 1.0 + nrm(ks[6], (DEPTH, D_MODEL), 0.01),
        'w_in': nrm(ks[7], (DEPTH, D_MODEL, N_IN), D_MODEL ** -0.5),
        'lru_conv_w': nrm(ks[8], (DEPTH, CONV_W, D_LRU), CONV_W ** -0.5),
        'lru_conv_b': nrm(ks[9], (DEPTH, D_LRU), 0.01),
        'lru_wa': nrm(ks[11], (DEPTH, LRU_BLOCKS, LRU_BLOCK, LRU_BLOCK), LRU_BLOCK ** -0.5),
        'lru_ba': nrm(ks[12], (DEPTH, D_LRU), 0.01),
        'lru_wx': nrm(ks[15], (DEPTH, LRU_BLOCKS, LRU_BLOCK, LRU_BLOCK), LRU_BLOCK ** -0.5),
        'lru_bx': nrm(ks[16], (DEPTH, D_LRU), 0.01),
        'lru_lambda': lru_lambda,
        'ssd_conv_w': nrm(ks[17], (DEPTH, CONV_W, D_XBC), CONV_W ** -0.5),
        'ssd_conv_b': nrm(ks[18], (DEPTH, D_XBC), 0.01),
        'ssd_dt_bias': ssd_dt_bias,
        'ssd_a_log': ssd_a_log,
        'ssd_d': 1.0 + nrm(ks[19], (DEPTH, SSD_HEADS), 0.1),
        'ssd_norm_g': 1.0 + nrm(ks[20], (DEPTH, D_SSM), 0.01),
        'w_out': nrm(ks[21], (DEPTH, D_MIX, D_MODEL), D_MIX ** -0.5),
        'norm_ffn_g': 1.0 + nrm(ks[22], (DEPTH, D_MODEL), 0.01),
        'router_group_w': nrm(ks[23], (DEPTH, D_MODEL, MOE_GROUPS), D_MODEL ** -0.5),
        'router_group_b': nrm(ks[24], (DEPTH, MOE_GROUPS), 0.01),
        'router_expert_w': nrm(ks[25], (DEPTH, D_MODEL, N_EXPERTS), D_MODEL ** -0.5),
        'router_expert_b': nrm(ks[26], (DEPTH, N_EXPERTS), 0.01),
        'moe_w1': nrm(ks[27], (DEPTH, N_EXPERTS, D_MODEL, D_EXPERT), D_MODEL ** -0.5),
        'moe_w3': nrm(ks[28], (DEPTH, N_EXPERTS, D_MODEL, D_EXPERT), D_MODEL ** -0.5),
        'moe_w2': nrm(ks[29], (DEPTH, N_EXPERTS, D_EXPERT, D_MODEL), D_EXPERT ** -0.5),
        'final_norm_g': 1.0 + nrm(ks[30], (D_MODEL,), 0.01),
    }


def reference(x_prompt, x_sample, state_lru_h, state_lru_conv, state_ssd, state_ssd_conv,
              norm_mix_g, w_in, lru_conv_w, lru_conv_b, lru_wa, lru_ba, lru_wx, lru_bx, lru_lambda,
              ssd_conv_w, ssd_conv_b, ssd_dt_bias, ssd_a_log, ssd_d, ssd_norm_g, w_out,
              norm_ffn_g, router_group_w, router_group_b, router_expert_w, router_expert_b,
              moe_w1, moe_w3, moe_w2, final_norm_g):
    bp = x_prompt.shape[0]
    hp = x_prompt
    hs = x_sample
    p_lru_h, p_lru_c, p_ssd, p_ssd_c = [], [], [], []
    s_lru_h, s_lru_c, s_ssd, s_ssd_c = [], [], [], []
    for i in range(DEPTH):
        prm = {
            'norm_mix_g': norm_mix_g[i], 'w_in': w_in[i],
            'lru_conv_w': lru_conv_w[i], 'lru_conv_b': lru_conv_b[i],
            'lru_wa': lru_wa[i], 'lru_ba': lru_ba[i], 'lru_wx': lru_wx[i], 'lru_bx': lru_bx[i],
            'lru_lambda': lru_lambda[i],
            'ssd_conv_w': ssd_conv_w[i], 'ssd_conv_b': ssd_conv_b[i],
            'ssd_dt_bias': ssd_dt_bias[i], 'ssd_a_log': ssd_a_log[i], 'ssd_d': ssd_d[i],
            'ssd_norm_g': ssd_norm_g[i], 'w_out': w_out[i], 'norm_ffn_g': norm_ffn_g[i],
            'router_group_w': router_group_w[i], 'router_group_b': router_group_b[i],
            'router_expert_w': router_expert_w[i], 'router_expert_b': router_expert_b[i],
            'moe_w1': moe_w1[i], 'moe_w3': moe_w3[i], 'moe_w2': moe_w2[i],
        }
        hp, a1, a2, a3, a4 = block(
            hp,
            jnp.zeros((bp, D_LRU), jnp.float32),
            jnp.zeros((bp, CONV_W - 1, D_LRU), hp.dtype),
            jnp.zeros((bp, SSD_HEADS, SSD_HEADDIM, D_STATE), jnp.float32),
            jnp.zeros((bp, CONV_W - 1, D_XBC), hp.dtype),
            prm, 0)
        p_lru_h.append(a1); p_lru_c.append(a2); p_ssd.append(a3); p_ssd_c.append(a4)
        hs, b1, b2, b3, b4 = block(hs, state_lru_h[i], state_lru_conv[i], state_ssd[i], state_ssd_conv[i], prm, PAST_LEN)
        s_lru_h.append(b1); s_lru_c.append(b2); s_ssd.append(b3); s_ssd_c.append(b4)
    y_prompt = rmsnorm(hp, final_norm_g)
    y_sample = rmsnorm(hs, final_norm_g)
    return (y_prompt, y_sample,
            jnp.stack(p_lru_h), jnp.stack(p_lru_c), jnp.stack(p_ssd), jnp.stack(p_ssd_c),
            jnp.stack(s_lru_h), jnp.stack(s_lru_c), jnp.stack(s_ssd), jnp.stack(s_ssd_c))
```

```python
import functools

import jax
import jax.numpy as jnp
from jax import lax
from jax.experimental import pallas as pl
from jax.experimental.pallas import tpu as pltpu

F32 = jnp.float32
BF16 = jnp.bfloat16
I32 = jnp.int32

D_MODEL = 1024
D_LRU = 1024
LRU_BLOCKS = 16
LRU_BLOCK = 64
LRU_C = 8.0
CONV_W = 4
D_SSM = 1024
SSD_HEADDIM = 64
SSD_HEADS = 16
SSD_GROUPS = 2
SSD_HPG = 8
D_STATE = 128
D_XBC = 1536
N_MAIN = 2 * D_LRU + D_SSM + D_XBC
MOE_GROUPS = 4
EXPERTS_PER_GROUP = 8
N_EXPERTS = 32
D_EXPERT = 512
EPS = 1e-6
SSD_CHUNK = 64
PAST_LEN = 1024

LANES = 128
SUBLANES = 8
GATE_TILE = 256
ROW_TILE = 256
VMEM_LIMIT = 52 * 1024 * 1024

_HI = lax.Precision.HIGHEST
_NT = (((1,), (1,)), ((), ()))
_TN = (((0,), (0,)), ((), ()))


def _params(n_axes):
    return pltpu.CompilerParams(dimension_semantics=("arbitrary",) * n_axes,
                                vmem_limit_bytes=VMEM_LIMIT)


def _rmsnorm(x, g):
    return x * lax.rsqrt(jnp.mean(x * x, axis=-1, keepdims=True) + EPS) * g


def _full(shape):
    n = len(shape)
    return pl.BlockSpec(shape, lambda *_: (0,) * n)


def _in_proj_body(x_ref, g_ref, w_ref, wdt_ref, xl_ref, gl_ref, z_ref, xbc_ref, dt_ref):
    xb = _rmsnorm(x_ref[...], g_ref[...]).astype(BF16)

    def mm(lo, hi):
        return jnp.dot(xb, w_ref[:, lo:hi], preferred_element_type=F32)

    xl_ref[...] = mm(0, D_LRU)
    gl_ref[...] = mm(D_LRU, 2 * D_LRU)
    z_ref[...] = mm(2 * D_LRU, 2 * D_LRU + D_SSM)
    xbc_ref[...] = mm(2 * D_LRU + D_SSM, N_MAIN)
    dt_ref[...] = jnp.dot(xb, wdt_ref[...], preferred_element_type=F32)


def _in_proj(x2d, g, w_main, w_dt, tm):
    T = x2d.shape[0]
    row = lambda w: pl.BlockSpec((tm, w), lambda i: (i, 0))
    widths = (D_LRU, D_LRU, D_SSM, D_XBC, LANES)
    return pl.pallas_call(
        _in_proj_body,
        grid=(T // tm,),
        in_specs=[row(D_MODEL), _full((1, D_MODEL)), _full((D_MODEL, N_MAIN)), _full((D_MODEL, LANES))],
        out_specs=[row(w) for w in widths],
        out_shape=[jax.ShapeDtypeStruct((T, w), F32) for w in widths],
        compiler_params=_params(1),
        name="in_proj",
    )(x2d, g, w_main, w_dt)


def _causal_conv(xpad, cw_ref, cb_ref, Tt):
    cw = cw_ref[...]
    y = cb_ref[...] + xpad[5:5 + Tt, :] * cw[0:1, :]
    y = y + xpad[6:6 + Tt, :] * cw[1:2, :]
    y = y + xpad[7:7 + Tt, :] * cw[2:3, :]
    return y + xpad[8:8 + Tt, :] * cw[3:4, :]


def _load_tile(t, xpad, c0_ref, x_ref, Tt):
    @pl.when(t == 0)
    def _():
        xpad[0:SUBLANES, :] = c0_ref[0]

    @pl.when(t > 0)
    def _():
        xpad[0:SUBLANES, :] = xpad[Tt:Tt + SUBLANES, :]

    xpad[SUBLANES:SUBLANES + Tt, :] = x_ref[0]


def _lru_body(x_ref, g_ref, c0_ref, h0_ref, cw_ref, cb_ref, wbd_ref, ba_ref, bx_ref, lam_ref,
              y_ref, hN_ref, cN_ref, xpad, a_s, u_s, hcar, *, Tt, start_pos):
    t = pl.program_id(1)
    _load_tile(t, xpad, c0_ref, x_ref, Tt)

    @pl.when(t == 0)
    def _():
        hcar[...] = jnp.broadcast_to(h0_ref[0], (SUBLANES, D_LRU))

    xc = _causal_conv(xpad, cw_ref, cb_ref, Tt)
    xcb = xc.astype(BF16)
    sp = jax.nn.softplus(-lam_ref[...])
    pos0 = (lax.broadcasted_iota(I32, (Tt, 1), 0) + t * Tt + start_pos) == 0
    for j in range(D_LRU // GATE_TILE):
        sl = slice(GATE_TILE * j, GATE_TILE * (j + 1))
        ga = jnp.dot(xcb[:, sl], wbd_ref[j], preferred_element_type=F32)
        r = jax.nn.sigmoid(ga[:, :GATE_TILE] + ba_ref[:, sl])
        i = jax.nn.sigmoid(ga[:, GATE_TILE:] + bx_ref[:, sl])
        a = jnp.exp((-LRU_C * r) * sp[:, sl])
        mult = jnp.where(pos0, 1.0, jnp.sqrt(1.0 - a * a))
        a_s[:, sl] = a
        u_s[:, sl] = mult * i * xc[:, sl]

    rows = lax.broadcasted_iota(I32, (SUBLANES, D_LRU), 0)

    def scan8(gi, hprev):
        r0 = pl.multiple_of(gi * SUBLANES, SUBLANES)
        a8 = a_s[pl.ds(r0, SUBLANES), :]
        u8 = u_s[pl.ds(r0, SUBLANES), :]
        for s in (1, 2, 4):
            ok = rows >= s
            u_sh = pltpu.roll(u8, s, 0)
            a_sh = pltpu.roll(a8, s, 0)
            u8 = jnp.where(ok, u8 + a8 * u_sh, u8)
            a8 = jnp.where(ok, a8 * a_sh, a8)
        h8 = u8 + a8 * hprev
        u_s[pl.ds(r0, SUBLANES), :] = h8
        return jnp.broadcast_to(h8[SUBLANES - 1:SUBLANES, :], (SUBLANES, D_LRU))

    hlast = lax.fori_loop(0, Tt // SUBLANES, scan8, hcar[...])
    hcar[...] = hlast
    y_ref[0] = (u_s[...] * jax.nn.gelu(g_ref[0], approximate=True)).astype(BF16)
    hN_ref[0] = hlast[0:1, :]
    cN_ref[0] = xpad[Tt:Tt + SUBLANES, :]


def _lru(x_lru, g_lru, c0, h0, cw, cb, wbd, ba, bx, lam, Tt, start_pos):
    B, L, _ = x_lru.shape
    seq = pl.BlockSpec((1, Tt, D_LRU), lambda b, t: (b, t, 0))
    per_b = lambda r: pl.BlockSpec((1, r, D_LRU), lambda b, t: (b, 0, 0))
    return pl.pallas_call(
        functools.partial(_lru_body, Tt=Tt, start_pos=start_pos),
        grid=(B, L // Tt),
        in_specs=[seq, seq, per_b(SUBLANES), per_b(1), _full((CONV_W, D_LRU)), _full((1, D_LRU)),
                  _full(wbd.shape), _full((1, D_LRU)), _full((1, D_LRU)), _full((1, D_LRU))],
        out_specs=[seq, per_b(1), per_b(SUBLANES)],
        out_shape=[jax.ShapeDtypeStruct((B, L, D_LRU), BF16),
                   jax.ShapeDtypeStruct((B, 1, D_LRU), F32),
                   jax.ShapeDtypeStruct((B, SUBLANES, D_LRU), F32)],
        scratch_shapes=[pltpu.VMEM((Tt + SUBLANES, D_LRU), F32), pltpu.VMEM((Tt, D_LRU), F32),
                        pltpu.VMEM((Tt, D_LRU), F32), pltpu.VMEM((SUBLANES, D_LRU), F32)],
        compiler_params=_params(2),
        name="lru",
    )(x_lru, g_lru, c0, h0, cw, cb, wbd, ba, bx, lam)


def _time_to_lanes(v, q):
    if q < LANES:
        v = jnp.concatenate([v, jnp.zeros((LANES - q, LANES), F32)], axis=0)
    return v.T


def _ssd_body(xbc_ref, z_ref, dt_ref, c0_ref, s0_ref, cw_ref, cb_ref, dtb_ref, alog_ref, dvec_ref, ng_ref,
              y_ref, sN_ref, cN_ref, xpad, xa_s, y_s, *, Tt, q):
    t = pl.program_id(1)
    _load_tile(t, xpad, c0_ref, xbc_ref, Tt)

    @pl.when(t == 0)
    def _():
        sN_ref[...] = s0_ref[...]

    xc = _causal_conv(xpad, cw_ref, cb_ref, Tt)
    xa_s[...] = jax.nn.silu(xc)
    A = -jnp.exp(alog_ref[...])
    ri = lax.broadcasted_iota(I32, (q, q), 0)
    ci = lax.broadcasted_iota(I32, (q, q), 1)
    mask = ri >= ci
    tri = mask.astype(F32)
    off_b = D_SSM
    off_c = D_SSM + SSD_GROUPS * D_STATE

    def chunk(c, carry):
        r0 = pl.multiple_of(c * q, q)
        xs = xa_s[pl.ds(r0, q), 0:D_SSM]
        xsb = xs.astype(BF16)
        dt = jax.nn.softplus(dt_ref[0, pl.ds(r0, q), :] + dtb_ref[...])
        cs = jnp.dot(tri, dt * A, precision=_HI, preferred_element_type=F32)
        csT = _time_to_lanes(cs, q)
        dtT = _time_to_lanes(dt, q)
        ecs = jnp.exp(cs)
        cs_last = cs[q - 1:q, :]
        wdec = jnp.exp(cs_last - cs) * dt
        cdec = jnp.exp(cs_last)
        for g in range(SSD_GROUPS):
            Bg = xa_s[pl.ds(r0, q), off_b + g * D_STATE:off_b + (g + 1) * D_STATE].astype(BF16)
            Cg = xa_s[pl.ds(r0, q), off_c + g * D_STATE:off_c + (g + 1) * D_STATE].astype(BF16)
            CB = lax.dot_general(Cg, Bg, _NT, preferred_element_type=F32)
            for m in range(SSD_HPG):
                h = g * SSD_HPG + m
                hs = slice(h * SSD_HEADDIM, (h + 1) * SSD_HEADDIM)
                seg = cs[:, h:h + 1] - csT[h:h + 1, 0:q]
                Lm = jnp.where(mask, jnp.exp(jnp.where(mask, seg, 0.0)), 0.0)
                Mw = (CB * Lm * dtT[h:h + 1, 0:q]).astype(BF16)
                yd = jnp.dot(Mw, xsb[:, hs], preferred_element_type=F32)
                S = sN_ref[0, h]
                yo = lax.dot_general(Cg, S.astype(BF16), _NT, preferred_element_type=F32) * ecs[:, h:h + 1]
                xw = (wdec[:, h:h + 1] * xs[:, hs]).astype(BF16)
                st = lax.dot_general(xw, Bg, _TN, preferred_element_type=F32)
                sN_ref[0, h] = cdec[:, h:h + 1] * S + st
                y_s[pl.ds(r0, q), hs] = yd + yo
        return carry

    lax.fori_loop(0, Tt // q, chunk, 0)

    y = y_s[...] + dvec_ref[...] * xa_s[:, 0:D_SSM]
    y = y * jax.nn.silu(z_ref[0])
    gw = D_SSM // SSD_GROUPS
    for g in range(SSD_GROUPS):
        sl = slice(g * gw, (g + 1) * gw)
        yg = y[:, sl]
        yg = yg * lax.rsqrt(jnp.mean(yg * yg, axis=-1, keepdims=True) + EPS)
        y_ref[0, :, sl] = (yg * ng_ref[:, sl]).astype(BF16)
    cN_ref[0] = xpad[Tt:Tt + SUBLANES, :]


def _ssd(xbc, z, dt, c0, s0, cw, cb, dtb, alog, dvec, ng, Tt, q):
    B, L, _ = xbc.shape
    seq = lambda w: pl.BlockSpec((1, Tt, w), lambda b, t: (b, t, 0))
    per_b = pl.BlockSpec((1, SUBLANES, D_XBC), lambda b, t: (b, 0, 0))
    st = pl.BlockSpec((1, SSD_HEADS, SSD_HEADDIM, D_STATE), lambda b, t: (b, 0, 0, 0))
    return pl.pallas_call(
        functools.partial(_ssd_body, Tt=Tt, q=q),
        grid=(B, L // Tt),
        in_specs=[seq(D_XBC), seq(D_SSM), seq(LANES), per_b, st, _full((CONV_W, D_XBC)), _full((1, D_XBC)),
                  _full((1, LANES)), _full((1, LANES)), _full((1, D_SSM)), _full((1, D_SSM))],
        out_specs=[seq(D_SSM), st, per_b],
        out_shape=[jax.ShapeDtypeStruct((B, L, D_SSM), BF16),
                   jax.ShapeDtypeStruct((B, SSD_HEADS, SSD_HEADDIM, D_STATE), F32),
                   jax.ShapeDtypeStruct((B, SUBLANES, D_XBC), F32)],
        scratch_shapes=[pltpu.VMEM((Tt + SUBLANES, D_XBC), F32), pltpu.VMEM((Tt, D_XBC), F32),
                        pltpu.VMEM((Tt, D_SSM), F32)],
        compiler_params=_params(2),
        name="ssd",
    )(xbc, z, dt, c0, s0, cw, cb, dtb, alog, dvec, ng)


def _out_router_body(x_ref, yl_ref, ys_ref, wo_ref, gf_ref, wg_ref, bg_ref, we_ref, be_ref,
                     h1_ref, xn_ref, meta_ref, gate_ref, cnt_ref, carry, *, tm):
    step = pl.program_id(0)

    @pl.when(step == 0)
    def _():
        carry[...] = jnp.zeros_like(carry)

    mix = jnp.concatenate([yl_ref[...], ys_ref[...]], axis=1)
    h1 = x_ref[...] + jnp.dot(mix, wo_ref[...], preferred_element_type=F32)
    h1_ref[...] = h1
    xn = _rmsnorm(h1, gf_ref[...])
    xn_ref[...] = xn

    rows8 = lax.broadcasted_iota(I32, (SUBLANES, tm), 0)
    lg = lax.dot_general(wg_ref[...], xn, _NT, precision=_HI, preferred_element_type=F32) + bg_ref[:, 0:1]
    lg = jnp.where(rows8 < MOE_GROUPS, lg, -jnp.inf)
    eg = jnp.exp(lg - jnp.max(lg, axis=0, keepdims=True))
    pg = eg / jnp.sum(eg, axis=0, keepdims=True)
    pgs = jnp.max(pg, axis=0, keepdims=True)
    rows8f = rows8.astype(F32)
    gsel = jnp.min(jnp.where(pg == pgs, rows8f, float(SUBLANES)), axis=0, keepdims=True)

    rows32 = lax.broadcasted_iota(I32, (N_EXPERTS, tm), 0)
    rows32f = rows32.astype(F32)
    grp = (rows32 // EXPERTS_PER_GROUP).astype(F32)
    le = lax.dot_general(we_ref[...], xn, _NT, precision=_HI, preferred_element_type=F32) + be_ref[:, 0:1]
    ing = grp == gsel
    lem = jnp.where(ing, le, -jnp.inf)
    ee = jnp.exp(lem - jnp.max(lem, axis=0, keepdims=True))
    pe = ee / jnp.sum(ee, axis=0, keepdims=True)
    pe1 = jnp.where(ing, pe, -1.0)
    v1 = jnp.max(pe1, axis=0, keepdims=True)
    i1 = jnp.min(jnp.where(pe1 == v1, rows32f, float(N_EXPERTS)), axis=0, keepdims=True)
    pe2 = jnp.where(rows32f == i1, -1.0, pe1)
    v2 = jnp.max(pe2, axis=0, keepdims=True)
    i2 = jnp.min(jnp.where(pe2 == v2, rows32f, float(N_EXPERTS)), axis=0, keepdims=True)
    sv = v1 + v2
    w1 = v1 / sv * pgs
    w2 = v2 / sv * pgs

    oh1 = rows32f == i1
    oh2 = rows32f == i2
    oh = jnp.where(oh1 | oh2, 1.0, 0.0)
    before = (lax.broadcasted_iota(I32, (tm, tm), 0) < lax.broadcasted_iota(I32, (tm, tm), 1))
    pref = jnp.dot(oh.astype(BF16), jnp.where(before, 1.0, 0.0).astype(BF16), preferred_element_type=F32)
    pref = pref + carry[:, 0:1]
    r1 = jnp.sum(jnp.where(oh1, pref, 0.0), axis=0, keepdims=True)
    r2 = jnp.sum(jnp.where(oh2, pref, 0.0), axis=0, keepdims=True)
    carry[...] = carry[...] + jnp.sum(oh, axis=1, keepdims=True)
    cnt_ref[...] = carry[...]

    meta = jnp.where(rows8 == 0, i1, jnp.where(rows8 == 1, i2, jnp.where(rows8 == 2, r1, jnp.where(rows8 == 3, r2, 0.0))))
    meta_ref[...] = meta.astype(I32)
    gate_ref[...] = jnp.where(rows8 == 0, w1, jnp.where(rows8 == 1, w2, 0.0))


def _out_router(x2d, y_lru, y_ssd, w_out, gf, wg, bg, we, be, tm):
    T = x2d.shape[0]
    row = lambda w: pl.BlockSpec((tm, w), lambda i: (i, 0))
    col = pl.BlockSpec((SUBLANES, tm), lambda i: (0, i))
    return pl.pallas_call(
        functools.partial(_out_router_body, tm=tm),
        grid=(T // tm,),
        in_specs=[row(D_MODEL), row(D_LRU), row(D_SSM), _full((D_LRU + D_SSM, D_MODEL)), _full((1, D_MODEL)),
                  _full((SUBLANES, D_MODEL)), _full((SUBLANES, LANES)),
                  _full((N_EXPERTS, D_MODEL)), _full((N_EXPERTS, LANES))],
        out_specs=[row(D_MODEL), row(D_MODEL), col, col, _full((N_EXPERTS, LANES))],
        out_shape=[jax.ShapeDtypeStruct((T, D_MODEL), F32), jax.ShapeDtypeStruct((T, D_MODEL), F32),
                   jax.ShapeDtypeStruct((SUBLANES, T), I32), jax.ShapeDtypeStruct((SUBLANES, T), F32),
                   jax.ShapeDtypeStruct((N_EXPERTS, LANES), F32)],
        scratch_shapes=[pltpu.VMEM((N_EXPERTS, LANES), F32)],
        compiler_params=_params(1),
        name="out_router",
    )(x2d, y_lru, y_ssd, w_out, gf, wg, bg, we, be)


def _row_copy(src_hbm, src_row, dst, dst_row, sem):
    return pltpu.make_async_copy(src_hbm.at[pl.ds(src_row, 1)], dst.at[pl.ds(dst_row, 1)], sem)


def _dispatch_body(offs_ref, meta_ref, xn_hbm, xs_in_hbm, xs_hbm, sem, *, tm):
    del xs_in_hbm
    base = pl.program_id(0) * tm

    def issue(t, c):
        for k in range(2):
            p = offs_ref[meta_ref[k, t]] + meta_ref[2 + k, t]
            _row_copy(xn_hbm, base + t, xs_hbm, p, sem).start()
        return c

    lax.fori_loop(0, tm, issue, 0)

    def drain(t, c):
        for k in range(2):
            _row_copy(xn_hbm, 0, xs_hbm, 0, sem).wait()
        return c

    lax.fori_loop(0, tm, drain, 0)


def _dispatch(offs, meta, xn, n_rows, tm):
    T = xn.shape[0]
    xs0 = jnp.zeros((n_rows, D_MODEL), F32)
    return pl.pallas_call(
        functools.partial(_dispatch_body, tm=tm),
        grid_spec=pltpu.PrefetchScalarGridSpec(
            num_scalar_prefetch=1,
            grid=(T // tm,),
            in_specs=[pl.BlockSpec((SUBLANES, tm), lambda i, offs: (0, i), memory_space=pltpu.SMEM),
                      pl.BlockSpec(memory_space=pl.ANY), pl.BlockSpec(memory_space=pl.ANY)],
            out_specs=pl.BlockSpec(memory_space=pl.ANY),
            scratch_shapes=[pltpu.SemaphoreType.DMA(())],
        ),
        out_shape=jax.ShapeDtypeStruct((n_rows, D_MODEL), F32),
        input_output_aliases={3: 0},
        compiler_params=_params(1),
        name="dispatch",
    )(offs, meta, xn, xs0)


def _experts_body(te_ref, na_ref, x_ref, w1_ref, w3_ref, w2_ref, o_ref, w1b, w3b, w2b):
    i = pl.program_id(0)
    changed = jnp.logical_or(i == 0, te_ref[i] != te_ref[jnp.maximum(i - 1, 0)])

    @pl.when(changed)
    def _():
        w1b[...] = w1_ref[0].astype(BF16)
        w3b[...] = w3_ref[0].astype(BF16)
        w2b[...] = w2_ref[0].astype(BF16)

    @pl.when(i < na_ref[0])
    def _():
        xb = x_ref[...].astype(BF16)
        a = jnp.dot(xb, w1b[...], preferred_element_type=F32)
        b = jnp.dot(xb, w3b[...], preferred_element_type=F32)
        hd = (jax.nn.silu(a) * b).astype(BF16)
        o_ref[...] = jnp.dot(hd, w2b[...], preferred_element_type=F32)

    @pl.when(i >= na_ref[0])
    def _():
        o_ref[...] = jnp.zeros_like(o_ref)


def _experts(tile_e, n_active, xs, w1, w3, w2, tme):
    n_rows = xs.shape[0]
    row = pl.BlockSpec((tme, D_MODEL), lambda i, te, na: (i, 0))
    return pl.pallas_call(
        _experts_body,
        grid_spec=pltpu.PrefetchScalarGridSpec(
            num_scalar_prefetch=2,
            grid=(n_rows // tme,),
            in_specs=[row,
                      pl.BlockSpec((1, D_MODEL, D_EXPERT), lambda i, te, na: (te[i], 0, 0)),
                      pl.BlockSpec((1, D_MODEL, D_EXPERT), lambda i, te, na: (te[i], 0, 0)),
                      pl.BlockSpec((1, D_EXPERT, D_MODEL), lambda i, te, na: (te[i], 0, 0))],
            out_specs=row,
            scratch_shapes=[pltpu.VMEM((D_MODEL, D_EXPERT), BF16), pltpu.VMEM((D_MODEL, D_EXPERT), BF16),
                            pltpu.VMEM((D_EXPERT, D_MODEL), BF16)],
        ),
        out_shape=jax.ShapeDtypeStruct((n_rows, D_MODEL), F32),
        compiler_params=_params(1),
        name="experts",
    )(tile_e, n_active, xs, w1, w3, w2)


def _combine_body(offs_ref, meta_ref, gate_ref, h1_ref, fg_ref, ys_hbm, y_ref, ybuf, sem, *, tm):
    def issue(t, c):
        for k in range(2):
            p = offs_ref[meta_ref[k, t]] + meta_ref[2 + k, t]
            _row_copy(ys_hbm, p, ybuf.at[k], t, sem).start()
        return c

    lax.fori_loop(0, tm, issue, 0)

    def drain(t, c):
        for k in range(2):
            _row_copy(ys_hbm, 0, ybuf.at[k], 0, sem).wait()
        return c

    lax.fori_loop(0, tm, drain, 0)

    eye = lax.broadcasted_iota(I32, (tm, tm), 0) == lax.broadcasted_iota(I32, (tm, tm), 1)
    g1 = jnp.sum(jnp.where(eye, gate_ref[0:1, :], 0.0), axis=1, keepdims=True)
    g2 = jnp.sum(jnp.where(eye, gate_ref[1:2, :], 0.0), axis=1, keepdims=True)
    h2 = h1_ref[...] + (g1 * ybuf[0] + g2 * ybuf[1])
    y_ref[...] = _rmsnorm(h2, fg_ref[...])


def _combine(offs, meta, gates, h1, fg, ys, tm):
    T = h1.shape[0]
    return pl.pallas_call(
        functools.partial(_combine_body, tm=tm),
        grid_spec=pltpu.PrefetchScalarGridSpec(
            num_scalar_prefetch=1,
            grid=(T // tm,),
            in_specs=[pl.BlockSpec((SUBLANES, tm), lambda i, offs: (0, i), memory_space=pltpu.SMEM),
                      pl.BlockSpec((SUBLANES, tm), lambda i, offs: (0, i)),
                      pl.BlockSpec((tm, D_MODEL), lambda i, offs: (i, 0)),
                      pl.BlockSpec((1, D_MODEL), lambda i, offs: (0, 0)),
                      pl.BlockSpec(memory_space=pl.ANY)],
            out_specs=pl.BlockSpec((tm, D_MODEL), lambda i, offs: (i, 0)),
            scratch_shapes=[pltpu.VMEM((2, tm, D_MODEL), F32), pltpu.SemaphoreType.DMA(())],
        ),
        out_shape=jax.ShapeDtypeStruct((T, D_MODEL), F32),
        compiler_params=_params(1),
        name="combine",
    )(offs, meta, gates, h1, fg, ys)


def _blockdiag(w):
    per = GATE_TILE // LRU_BLOCK
    w4 = w.reshape(LRU_BLOCKS // per, per, LRU_BLOCK, LRU_BLOCK)
    eye = jnp.eye(per, dtype=w.dtype)
    return jnp.einsum('jbio,bc->jbico', w4, eye).reshape(LRU_BLOCKS // per, GATE_TILE, GATE_TILE)


def _pad_rows(c):
    return jnp.pad(c, ((0, 0), (SUBLANES - (CONV_W - 1), 0), (0, 0)))


def _lane_row(v, width=LANES):
    return jnp.pad(v, (0, width - v.shape[0])).reshape(1, width)


def _prep(norm_mix_g, w_in, lru_conv_w, lru_conv_b, lru_wa, lru_ba, lru_wx, lru_bx, lru_lambda,
          ssd_conv_w, ssd_conv_b, ssd_dt_bias, ssd_a_log, ssd_d, ssd_norm_g, w_out,
          norm_ffn_g, router_group_w, router_group_b, router_expert_w, router_expert_b,
          moe_w1, moe_w3, moe_w2, final_norm_g):
    w = w_in[0]
    P = dict(
        g_mix=norm_mix_g[0].reshape(1, D_MODEL),
        w_main=w[:, :N_MAIN].astype(BF16),
        w_dt=jnp.pad(w[:, N_MAIN:], ((0, 0), (0, LANES - SSD_HEADS))).astype(BF16),
        lru_cw=lru_conv_w[0], lru_cb=lru_conv_b[0].reshape(1, D_LRU),
        wbd=jnp.concatenate([_blockdiag(lru_wa[0]), _blockdiag(lru_wx[0])], axis=2).astype(BF16),
        ba=lru_ba[0].reshape(1, D_LRU), bx=lru_bx[0].reshape(1, D_LRU), lam=lru_lambda[0].reshape(1, D_LRU),
        ssd_cw=ssd_conv_w[0], ssd_cb=ssd_conv_b[0].reshape(1, D_XBC),
        dtb=_lane_row(ssd_dt_bias[0]), alog=_lane_row(ssd_a_log[0]),
        dvec=jnp.repeat(ssd_d[0], SSD_HEADDIM).reshape(1, D_SSM),
        ng=ssd_norm_g[0].reshape(1, D_SSM),
        w_out=w_out[0].astype(BF16),
        g_ffn=norm_ffn_g[0].reshape(1, D_MODEL),
        wg=jnp.pad(router_group_w[0].T, ((0, SUBLANES - MOE_GROUPS), (0, 0))),
        bg=jnp.broadcast_to(jnp.pad(router_group_b[0], (0, SUBLANES - MOE_GROUPS))[:, None], (SUBLANES, LANES)),
        we=router_expert_w[0].T,
        be=jnp.broadcast_to(router_expert_b[0][:, None], (N_EXPERTS, LANES)),
        w1=moe_w1[0], w3=moe_w3[0], w2=moe_w2[0],
        g_final=final_norm_g.reshape(1, D_MODEL),
    )
    return P


def _expert_layout(counts, n_pairs, tme):
    cnt = counts[:, 0].astype(I32)
    padded = ((cnt + tme - 1) // tme) * tme
    ends = jnp.cumsum(padded)
    offs = ends - padded
    n_tiles = n_pairs // tme + N_EXPERTS
    n_active = ends[-1] // tme
    starts = jnp.arange(n_tiles, dtype=I32) * tme
    tile_e = jnp.sum(starts[:, None] >= ends[None, :], axis=1).astype(I32)
    last_e = jnp.sum((n_active - 1) * tme >= ends).astype(I32)
    tile_e = jnp.where(jnp.arange(n_tiles) < n_active, tile_e, last_e)
    return offs.astype(I32), tile_e, n_active.reshape(1).astype(I32), n_tiles * tme


def _group(x, lru_h0, lru_c0, ssd_h0, ssd_c0, P, start_pos):
    B, L, _ = x.shape
    T = B * L
    Tt = min(ROW_TILE, L)
    q = min(SSD_CHUNK, L)
    tm = min(ROW_TILE, T)
    x2d = x.reshape(T, D_MODEL)

    x_lru, g_lru, z, xbc, dt = _in_proj(x2d, P['g_mix'], P['w_main'], P['w_dt'], tm)
    seq = lambda a: a.reshape(B, L, a.shape[-1])
    y_lru, lru_h, lru_c = _lru(seq(x_lru), seq(g_lru), _pad_rows(lru_c0), lru_h0.reshape(B, 1, D_LRU),
                               P['lru_cw'], P['lru_cb'], P['wbd'], P['ba'], P['bx'], P['lam'], Tt, start_pos)
    y_ssd, ssd_h, ssd_c = _ssd(seq(xbc), seq(z), seq(dt), _pad_rows(ssd_c0), ssd_h0,
                               P['ssd_cw'], P['ssd_cb'], P['dtb'], P['alog'], P['dvec'], P['ng'], Tt, q)
    h1, xn, meta, gates, counts = _out_router(
        x2d, y_lru.reshape(T, D_LRU), y_ssd.reshape(T, D_SSM), P['w_out'], P['g_ffn'],
        P['wg'], P['bg'], P['we'], P['be'], tm)

    tme = min(ROW_TILE, max(SUBLANES, T // 8))
    offs, tile_e, n_active, n_rows = _expert_layout(counts, 2 * T, tme)
    xs = _dispatch(offs, meta, xn, n_rows, tm)
    ys = _experts(tile_e, n_active, xs, P['w1'], P['w3'], P['w2'], tme)
    y = _combine(offs, meta, gates, h1, P['g_final'], ys, tm)

    hist = SUBLANES - (CONV_W - 1)
    return (y.reshape(B, L, D_MODEL), lru_h.reshape(1, B, D_LRU), lru_c[:, hist:][None],
            ssd_h[None], ssd_c[:, hist:][None])


def kernel(x_prompt, x_sample, state_lru_h, state_lru_conv, state_ssd, state_ssd_conv, norm_mix_g, w_in, lru_conv_w, lru_conv_b, lru_wa, lru_ba, lru_wx, lru_bx, lru_lambda, ssd_conv_w, ssd_conv_b, ssd_dt_bias, ssd_a_log, ssd_d, ssd_norm_g, w_out, norm_ffn_g, router_group_w, router_group_b, router_expert_w, router_expert_b, moe_w1, moe_w3, moe_w2, final_norm_g):
    P = _prep(norm_mix_g, w_in, lru_conv_w, lru_conv_b, lru_wa, lru_ba, lru_wx, lru_bx, lru_lambda,
              ssd_conv_w, ssd_conv_b, ssd_dt_bias, ssd_a_log, ssd_d, ssd_norm_g, w_out,
              norm_ffn_g, router_group_w, router_group_b, router_expert_w, router_expert_b,
              moe_w1, moe_w3, moe_w2, final_norm_g)
    bp = x_prompt.shape[0]
    yp, a1, a2, a3, a4 = _group(
        x_prompt,
        jnp.zeros((bp, D_LRU), F32), jnp.zeros((bp, CONV_W - 1, D_LRU), F32),
        jnp.zeros((bp, SSD_HEADS, SSD_HEADDIM, D_STATE), F32), jnp.zeros((bp, CONV_W - 1, D_XBC), F32),
        P, 0)
    ys, b1, b2, b3, b4 = _group(x_sample, state_lru_h[0], state_lru_conv[0], state_ssd[0], state_ssd_conv[0],
                                P, PAST_LEN)
    return (yp, ys, a1, a2, a3, a4, b1, b2, b3, b4)
```

```python
import functools

import jax
import jax.numpy as jnp
from jax import lax
from jax.experimental import pallas as pl
from jax.experimental.pallas import tpu as pltpu

F32 = jnp.float32
BF16 = jnp.bfloat16
I32 = jnp.int32

D_MODEL = 1024
D_LRU = 1024
LRU_BLOCKS = 16
LRU_BLOCK = 64
LRU_C = 8.0
CONV_W = 4
D_SSM = 1024
SSD_HEADDIM = 64
SSD_HEADS = 16
SSD_GROUPS = 2
SSD_HPG = 8
D_STATE = 128
D_XBC = 1536
N_MAIN = 2 * D_LRU + D_SSM + D_XBC
MOE_GROUPS = 4
EXPERTS_PER_GROUP = 8
N_EXPERTS = 32
D_EXPERT = 512
EPS = 1e-6
SSD_CHUNK = 64
PAST_LEN = 1024

LANES = 128
SUBLANES = 8
GATE_TILE = 256
ROW_TILE = 256
ROUTER_E0 = 32
VMEM_LIMIT = 52 * 1024 * 1024

_NT = (((1,), (1,)), ((), ()))
_TN = (((0,), (0,)), ((), ()))


def _params(n_axes):
    return pltpu.CompilerParams(dimension_semantics=("arbitrary",) * n_axes,
                                vmem_limit_bytes=VMEM_LIMIT)


def _rmsnorm(x, g):
    return x * lax.rsqrt(jnp.mean(x * x, axis=-1, keepdims=True) + EPS) * g


def _full(shape):
    n = len(shape)
    return pl.BlockSpec(shape, lambda *_: (0,) * n)


def _in_proj_body(x_ref, g_ref, w_ref, wdt_ref, xl_ref, gl_ref, z_ref, xbc_ref, dt_ref):
    xb = _rmsnorm(x_ref[...], g_ref[...]).astype(BF16)

    def mm(lo, hi):
        return jnp.dot(xb, w_ref[:, lo:hi], preferred_element_type=F32)

    xl_ref[...] = mm(0, D_LRU)
    gl_ref[...] = mm(D_LRU, 2 * D_LRU)
    z_ref[...] = mm(2 * D_LRU, 2 * D_LRU + D_SSM)
    xbc_ref[...] = mm(2 * D_LRU + D_SSM, N_MAIN)
    dt_ref[...] = jnp.dot(xb, wdt_ref[...], preferred_element_type=F32)


def _in_proj(x2d, g, w_main, w_dt, tm):
    T = x2d.shape[0]
    row = lambda w: pl.BlockSpec((tm, w), lambda i: (i, 0))
    widths = (D_LRU, D_LRU, D_SSM, D_XBC, LANES)
    return pl.pallas_call(
        _in_proj_body,
        grid=(T // tm,),
        in_specs=[row(D_MODEL), _full((1, D_MODEL)), _full((D_MODEL, N_MAIN)), _full((D_MODEL, LANES))],
        out_specs=[row(w) for w in widths],
        out_shape=[jax.ShapeDtypeStruct((T, w), F32) for w in widths],
        compiler_params=_params(1),
        name="in_proj",
    )(x2d, g, w_main, w_dt)


def _causal_conv(xpad, cw_ref, cb_ref, Tt):
    cw = cw_ref[...]
    full = xpad[...]
    y = cb_ref[...]
    for k in range(CONV_W):
        shift = CONV_W - 1 - k
        xk = pltpu.roll(full, shift, 0) if shift else full
        y = y + xk[SUBLANES:SUBLANES + Tt, :] * cw[k:k + 1, :]
    return y


def _load_tile(t, xpad, c0_ref, x_ref, Tt):
    @pl.when(t == 0)
    def _():
        xpad[0:SUBLANES, :] = c0_ref[0]

    @pl.when(t > 0)
    def _():
        xpad[0:SUBLANES, :] = xpad[Tt:Tt + SUBLANES, :]

    xpad[SUBLANES:SUBLANES + Tt, :] = x_ref[0]


def _lru_body(x_ref, g_ref, c0_ref, h0_ref, cw_ref, cb_ref, wbd_ref, ba_ref, bx_ref, lam_ref,
              y_ref, hN_ref, cN_ref, xpad, a_s, u_s, hcar, *, Tt, start_pos):
    t = pl.program_id(1)
    _load_tile(t, xpad, c0_ref, x_ref, Tt)

    @pl.when(t == 0)
    def _():
        hcar[...] = jnp.broadcast_to(h0_ref[0], (SUBLANES, D_LRU))

    xc = _causal_conv(xpad, cw_ref, cb_ref, Tt)
    xcb = xc.astype(BF16)
    sp = jax.nn.softplus(-lam_ref[...])
    pos0 = (lax.broadcasted_iota(I32, (Tt, 1), 0) + t * Tt + start_pos) == 0
    for j in range(D_LRU // GATE_TILE):
        sl = slice(GATE_TILE * j, GATE_TILE * (j + 1))
        ga = jnp.dot(xcb[:, sl], wbd_ref[j], preferred_element_type=F32)
        r = jax.nn.sigmoid(ga[:, :GATE_TILE] + ba_ref[:, sl])
        i = jax.nn.sigmoid(ga[:, GATE_TILE:] + bx_ref[:, sl])
        a = jnp.exp((-LRU_C * r) * sp[:, sl])
        mult = jnp.where(pos0, 1.0, jnp.sqrt(1.0 - a * a))
        a_s[:, sl] = a
        u_s[:, sl] = mult * i * xc[:, sl]

    rows = lax.broadcasted_iota(I32, (SUBLANES, D_LRU), 0)

    def scan8(gi, hprev):
        r0 = pl.multiple_of(gi * SUBLANES, SUBLANES)
        a8 = a_s[pl.ds(r0, SUBLANES), :]
        u8 = u_s[pl.ds(r0, SUBLANES), :]
        for s in (1, 2, 4):
            ok = rows >= s
            u_sh = pltpu.roll(u8, s, 0)
            a_sh = pltpu.roll(a8, s, 0)
            u8 = jnp.where(ok, u8 + a8 * u_sh, u8)
            a8 = jnp.where(ok, a8 * a_sh, a8)
        h8 = u8 + a8 * hprev
        u_s[pl.ds(r0, SUBLANES), :] = h8
        return jnp.broadcast_to(h8[SUBLANES - 1:SUBLANES, :], (SUBLANES, D_LRU))

    hlast = lax.fori_loop(0, Tt // SUBLANES, scan8, hcar[...])
    hcar[...] = hlast
    y_ref[0] = (u_s[...] * jax.nn.gelu(g_ref[0], approximate=True)).astype(BF16)
    hN_ref[0] = hlast[0:1, :]
    cN_ref[0] = xpad[Tt:Tt + SUBLANES, :]


def _lru(x_lru, g_lru, c0, h0, cw, cb, wbd, ba, bx, lam, Tt, start_pos):
    B, L, _ = x_lru.shape
    seq = pl.BlockSpec((1, Tt, D_LRU), lambda b, t: (b, t, 0))
    per_b = lambda r: pl.BlockSpec((1, r, D_LRU), lambda b, t: (b, 0, 0))
    return pl.pallas_call(
        functools.partial(_lru_body, Tt=Tt, start_pos=start_pos),
        grid=(B, L // Tt),
        in_specs=[seq, seq, per_b(SUBLANES), per_b(1), _full((CONV_W, D_LRU)), _full((1, D_LRU)),
                  _full(wbd.shape), _full((1, D_LRU)), _full((1, D_LRU)), _full((1, D_LRU))],
        out_specs=[seq, per_b(1), per_b(SUBLANES)],
        out_shape=[jax.ShapeDtypeStruct((B, L, D_LRU), BF16),
                   jax.ShapeDtypeStruct((B, 1, D_LRU), F32),
                   jax.ShapeDtypeStruct((B, SUBLANES, D_LRU), F32)],
        scratch_shapes=[pltpu.VMEM((Tt + SUBLANES, D_LRU), F32), pltpu.VMEM((Tt, D_LRU), F32),
                        pltpu.VMEM((Tt, D_LRU), F32), pltpu.VMEM((SUBLANES, D_LRU), F32)],
        compiler_params=_params(2),
        name="lru",
    )(x_lru, g_lru, c0, h0, cw, cb, wbd, ba, bx, lam)


def _split3(v):
    hi = v.astype(BF16)
    r1 = v - hi.astype(F32)
    mid = r1.astype(BF16)
    lo = (r1 - mid.astype(F32)).astype(BF16)
    return hi, mid, lo


def _pad_time(v, rows):
    if v.shape[0] == rows:
        return v
    return jnp.concatenate([v, jnp.zeros((rows - v.shape[0], v.shape[1]), v.dtype)], axis=0)


def _ssd_body(xbc_ref, z_ref, dt_ref, c0_ref, s0_ref, cw_ref, cb_ref, dtb_ref, alog_ref, dvec_ref, ng_ref,
              y_ref, sN_ref, cN_ref, xpad, xa_s, y_s, st_s, *, Tt, q):
    t = pl.program_id(1)
    _load_tile(t, xpad, c0_ref, xbc_ref, Tt)

    @pl.when(t == 0)
    def _():
        st_s[...] = s0_ref[0].reshape(D_SSM, D_STATE).T

    xc = _causal_conv(xpad, cw_ref, cb_ref, Tt)
    xa_s[...] = jax.nn.silu(xc)
    A = -jnp.exp(alog_ref[...])
    P = SSD_HEADDIM
    tri = lax.broadcasted_iota(I32, (q, q), 0) >= lax.broadcasted_iota(I32, (q, q), 1)
    tri = jnp.where(tri, 1.0, 0.0).astype(BF16)
    expand = lax.broadcasted_iota(I32, (LANES, D_SSM), 0) == lax.broadcasted_iota(I32, (LANES, D_SSM), 1) // P
    expand = jnp.where(expand, 1.0, 0.0).astype(BF16)
    row_q = lax.broadcasted_iota(I32, (q, D_SSM), 0)
    lane_k = lax.broadcasted_iota(I32, (q, D_SSM), 1) % P
    diag = row_q == lane_k
    causal = row_q >= lane_k
    blk = GATE_TILE // P
    bd = (lax.broadcasted_iota(I32, (GATE_TILE, GATE_TILE), 0) // P
          == lax.broadcasted_iota(I32, (GATE_TILE, GATE_TILE), 1) // P)
    off_b = D_SSM
    off_c = D_SSM + SSD_GROUPS * D_STATE
    gw = D_SSM // SSD_GROUPS

    def exact01(parts, w01, left):
        one = (lambda p: jnp.dot(w01, p, preferred_element_type=F32)) if left else (
            lambda p: jnp.dot(p, w01, preferred_element_type=F32))
        hi, mid, lo = parts
        return (one(lo) + one(mid)) + one(hi)

    def chunk(c, carry):
        r0 = pl.multiple_of(c * q, q)
        xs = xa_s[pl.ds(r0, q), 0:D_SSM]
        xsb = xs.astype(BF16)
        dt = jax.nn.softplus(dt_ref[0, pl.ds(r0, q), :] + dtb_ref[...])
        cs = exact01(_split3(dt * A), tri, left=True)
        E = exact01(_split3(jnp.concatenate([cs, dt], axis=0)), expand, left=False)
        E_cs = E[0:q]
        E_dt = E[q:2 * q]
        r_cs = jnp.sum(jnp.where(diag, E_cs, 0.0), axis=0, keepdims=True)
        r_dt = jnp.sum(jnp.where(diag, E_dt, 0.0), axis=0, keepdims=True)
        Lm = jnp.where(causal, jnp.exp(jnp.where(causal, E_cs - r_cs, 0.0)), 0.0)
        Bs, Cs, CBs = [], [], []
        for g in range(SSD_GROUPS):
            Bg = xa_s[pl.ds(r0, q), off_b + g * D_STATE:off_b + (g + 1) * D_STATE].astype(BF16)
            Cg = xa_s[pl.ds(r0, q), off_c + g * D_STATE:off_c + (g + 1) * D_STATE].astype(BF16)
            Bt = jnp.concatenate([_pad_time(Bg, P)] * SSD_HPG, axis=0)
            CBs.append(lax.dot_general(Cg, Bt, _NT, preferred_element_type=F32))
            Bs.append(Bg)
            Cs.append(Cg)
        Mw = (jnp.concatenate(CBs, axis=1) * Lm * r_dt).astype(BF16)
        yd = []
        for j in range(D_SSM // GATE_TILE):
            sl = slice(j * GATE_TILE, (j + 1) * GATE_TILE)
            slab = _pad_time(xsb[:, sl], P)
            rhs = jnp.where(bd, jnp.concatenate([slab] * blk, axis=0), jnp.zeros((), BF16))
            yd.append(jnp.dot(Mw[:, sl], rhs, preferred_element_type=F32))
        cs_last = E_cs[q - 1:q, :]
        xw = (jnp.exp(cs_last - E_cs) * E_dt * xs).astype(BF16)
        cdec = jnp.exp(cs_last)
        yo = []
        for g in range(SSD_GROUPS):
            sl = slice(g * gw, (g + 1) * gw)
            S = st_s[:, sl]
            yo.append(jnp.dot(Cs[g], S.astype(BF16), preferred_element_type=F32))
            st = lax.dot_general(Bs[g], xw[:, sl], _TN, preferred_element_type=F32)
            st_s[:, sl] = cdec[:, sl] * S + st
        y_s[pl.ds(r0, q), :] = jnp.concatenate(yd, axis=1) + jnp.concatenate(yo, axis=1) * jnp.exp(E_cs)
        return carry

    lax.fori_loop(0, Tt // q, chunk, 0)

    @pl.when(t == pl.num_programs(1) - 1)
    def _():
        sN_ref[0] = st_s[...].T.reshape(SSD_HEADS, SSD_HEADDIM, D_STATE)

    y = y_s[...] + dvec_ref[...] * xa_s[:, 0:D_SSM]
    y = y * jax.nn.silu(z_ref[0])
    gw = D_SSM // SSD_GROUPS
    for g in range(SSD_GROUPS):
        sl = slice(g * gw, (g + 1) * gw)
        yg = y[:, sl]
        yg = yg * lax.rsqrt(jnp.mean(yg * yg, axis=-1, keepdims=True) + EPS)
        y_ref[0, :, sl] = (yg * ng_ref[:, sl]).astype(BF16)
    cN_ref[0] = xpad[Tt:Tt + SUBLANES, :]


def _ssd(xbc, z, dt, c0, s0, cw, cb, dtb, alog, dvec, ng, Tt, q):
    B, L, _ = xbc.shape
    seq = lambda w: pl.BlockSpec((1, Tt, w), lambda b, t: (b, t, 0))
    per_b = pl.BlockSpec((1, SUBLANES, D_XBC), lambda b, t: (b, 0, 0))
    st = pl.BlockSpec((1, SSD_HEADS, SSD_HEADDIM, D_STATE), lambda b, t: (b, 0, 0, 0))
    return pl.pallas_call(
        functools.partial(_ssd_body, Tt=Tt, q=q),
        grid=(B, L // Tt),
        in_specs=[seq(D_XBC), seq(D_SSM), seq(LANES), per_b, st, _full((CONV_W, D_XBC)), _full((1, D_XBC)),
                  _full((1, LANES)), _full((1, LANES)), _full((1, D_SSM)), _full((1, D_SSM))],
        out_specs=[seq(D_SSM), st, per_b],
        out_shape=[jax.ShapeDtypeStruct((B, L, D_SSM), BF16),
                   jax.ShapeDtypeStruct((B, SSD_HEADS, SSD_HEADDIM, D_STATE), F32),
                   jax.ShapeDtypeStruct((B, SUBLANES, D_XBC), F32)],
        scratch_shapes=[pltpu.VMEM((Tt + SUBLANES, D_XBC), F32), pltpu.VMEM((Tt, D_XBC), F32),
                        pltpu.VMEM((Tt, D_SSM), F32), pltpu.VMEM((D_STATE, D_SSM), F32)],
        compiler_params=_params(2),
        name="ssd",
    )(xbc, z, dt, c0, s0, cw, cb, dtb, alog, dvec, ng)


def _out_router_body(x_ref, yl_ref, ys_ref, wo_ref, gf_ref, wr_ref, br_ref,
                     h1_ref, xn_ref, meta_ref, gate_ref, cnt_ref, carry, *, tm):
    step = pl.program_id(0)

    @pl.when(step == 0)
    def _():
        carry[...] = jnp.zeros_like(carry)

    mix = jnp.concatenate([yl_ref[...], ys_ref[...]], axis=1)
    h1 = x_ref[...] + jnp.dot(mix, wo_ref[...], preferred_element_type=F32)
    h1_ref[...] = h1
    xn = _rmsnorm(h1, gf_ref[...])
    xn_ref[...] = xn

    xh = xn.astype(BF16)
    xm = (xn - xh.astype(F32)).astype(BF16)
    small = (jnp.dot(xh, wr_ref[1], preferred_element_type=F32)
             + jnp.dot(xm, wr_ref[0], preferred_element_type=F32))
    logits = small + jnp.dot(xh, wr_ref[0], preferred_element_type=F32) + br_ref[...]
    lt = logits.T
    rows8 = lax.broadcasted_iota(I32, (SUBLANES, tm), 0)
    lg = jnp.where(rows8 < MOE_GROUPS, lt[0:SUBLANES, :], -jnp.inf)
    eg = jnp.exp(lg - jnp.max(lg, axis=0, keepdims=True))
    pg = eg / jnp.sum(eg, axis=0, keepdims=True)
    pgs = jnp.max(pg, axis=0, keepdims=True)
    rows8f = rows8.astype(F32)
    gsel = jnp.min(jnp.where(pg == pgs, rows8f, float(SUBLANES)), axis=0, keepdims=True)

    rows32 = lax.broadcasted_iota(I32, (N_EXPERTS, tm), 0)
    rows32f = rows32.astype(F32)
    grp = (rows32 // EXPERTS_PER_GROUP).astype(F32)
    le = lt[ROUTER_E0:ROUTER_E0 + N_EXPERTS, :]
    ing = grp == gsel
    lem = jnp.where(ing, le, -jnp.inf)
    ee = jnp.exp(lem - jnp.max(lem, axis=0, keepdims=True))
    pe = ee / jnp.sum(ee, axis=0, keepdims=True)
    pe1 = jnp.where(ing, pe, -1.0)
    v1 = jnp.max(pe1, axis=0, keepdims=True)
    i1 = jnp.min(jnp.where(pe1 == v1, rows32f, float(N_EXPERTS)), axis=0, keepdims=True)
    pe2 = jnp.where(rows32f == i1, -1.0, pe1)
    v2 = jnp.max(pe2, axis=0, keepdims=True)
    i2 = jnp.min(jnp.where(pe2 == v2, rows32f, float(N_EXPERTS)), axis=0, keepdims=True)
    sv = v1 + v2
    w1 = v1 / sv * pgs
    w2 = v2 / sv * pgs

    oh1 = rows32f == i1
    oh2 = rows32f == i2
    oh = jnp.where(oh1 | oh2, 1.0, 0.0)
    before = (lax.broadcasted_iota(I32, (tm, tm), 0) < lax.broadcasted_iota(I32, (tm, tm), 1))
    pref = jnp.dot(oh.astype(BF16), jnp.where(before, 1.0, 0.0).astype(BF16), preferred_element_type=F32)
    pref = pref + carry[:, 0:1]
    r1 = jnp.sum(jnp.where(oh1, pref, 0.0), axis=0, keepdims=True)
    r2 = jnp.sum(jnp.where(oh2, pref, 0.0), axis=0, keepdims=True)
    carry[...] = carry[...] + jnp.sum(oh, axis=1, keepdims=True)
    cnt_ref[...] = carry[...]

    meta = jnp.where(rows8 == 0, i1, jnp.where(rows8 == 1, i2, jnp.where(rows8 == 2, r1, jnp.where(rows8 == 3, r2, 0.0))))
    meta_ref[...] = meta.astype(I32)
    gate_ref[...] = jnp.where(rows8 == 0, w1, jnp.where(rows8 == 1, w2, 0.0))


def _out_router(x2d, y_lru, y_ssd, w_out, gf, wr, br, tm):
    T = x2d.shape[0]
    row = lambda w: pl.BlockSpec((tm, w), lambda i: (i, 0))
    col = pl.BlockSpec((SUBLANES, tm), lambda i: (0, i))
    return pl.pallas_call(
        functools.partial(_out_router_body, tm=tm),
        grid=(T // tm,),
        in_specs=[row(D_MODEL), row(D_LRU), row(D_SSM), _full((D_LRU + D_SSM, D_MODEL)), _full((1, D_MODEL)),
                  _full((2, D_MODEL, LANES)), _full((1, LANES))],
        out_specs=[row(D_MODEL), row(D_MODEL), col, col, _full((N_EXPERTS, LANES))],
        out_shape=[jax.ShapeDtypeStruct((T, D_MODEL), F32), jax.ShapeDtypeStruct((T, D_MODEL), F32),
                   jax.ShapeDtypeStruct((SUBLANES, T), I32), jax.ShapeDtypeStruct((SUBLANES, T), F32),
                   jax.ShapeDtypeStruct((N_EXPERTS, LANES), F32)],
        scratch_shapes=[pltpu.VMEM((N_EXPERTS, LANES), F32)],
        compiler_params=_params(1),
        name="out_router",
    )(x2d, y_lru, y_ssd, w_out, gf, wr, br)


def _row_copy(src_hbm, src_row, dst, dst_row, sem):
    return pltpu.make_async_copy(src_hbm.at[pl.ds(src_row, 1)], dst.at[pl.ds(dst_row, 1)], sem)


def _positions_body(offs_ref, meta_ref, pos_ref):
    m = meta_ref[...]
    base = jnp.zeros_like(m)
    for e in range(N_EXPERTS):
        base = jnp.where(m == e, offs_ref[e], base)
    pos_ref[...] = base + pltpu.roll(m, SUBLANES - 2, 0)


def _positions(offs, meta):
    T = meta.shape[1]
    tb = min(T, 4096)
    blk = pl.BlockSpec((SUBLANES, tb), lambda i, offs: (0, i))
    return pl.pallas_call(
        _positions_body,
        grid_spec=pltpu.PrefetchScalarGridSpec(num_scalar_prefetch=1, grid=(T // tb,), in_specs=[blk], out_specs=blk),
        out_shape=jax.ShapeDtypeStruct((SUBLANES, T), I32),
        compiler_params=_params(1),
        name="positions",
    )(offs, meta)


def _wait_rows(src, dst, sem):
    pltpu.make_async_copy(src, dst, sem).wait()


def _dispatch_body(pos_ref, xn_ref, xs_in_hbm, xs_hbm, sem, *, tm):
    del xs_in_hbm

    def issue(t, c):
        for k in range(2):
            _row_copy(xn_ref, t, xs_hbm, pos_ref[k, t], sem).start(priority=k)
        return c

    lax.fori_loop(0, tm, issue, 0, unroll=8)
    for k in range(2):
        _wait_rows(xn_ref, xs_hbm.at[pl.ds(0, tm)], sem)


def _dispatch(pos, xn, n_rows, tm):
    T = xn.shape[0]
    xs0 = jnp.zeros((n_rows, D_MODEL), F32)
    return pl.pallas_call(
        functools.partial(_dispatch_body, tm=tm),
        grid=(T // tm,),
        in_specs=[pl.BlockSpec((SUBLANES, tm), lambda i: (0, i), memory_space=pltpu.SMEM),
                  pl.BlockSpec((tm, D_MODEL), lambda i: (i, 0)),
                  pl.BlockSpec(memory_space=pl.ANY)],
        out_specs=pl.BlockSpec(memory_space=pl.ANY),
        scratch_shapes=[pltpu.SemaphoreType.DMA(())],
        out_shape=jax.ShapeDtypeStruct((n_rows, D_MODEL), F32),
        input_output_aliases={2: 0},
        compiler_params=_params(1),
        name="dispatch",
    )(pos, xn, xs0)


def _experts_body(te_ref, na_ref, x_ref, w1_ref, w3_ref, w2_ref, o_ref, w1b, w3b, w2b):
    i = pl.program_id(0)
    changed = jnp.logical_or(i == 0, te_ref[i] != te_ref[jnp.maximum(i - 1, 0)])

    @pl.when(changed)
    def _():
        w1b[...] = w1_ref[0].astype(BF16)
        w3b[...] = w3_ref[0].astype(BF16)
        w2b[...] = w2_ref[0].astype(BF16)

    @pl.when(i < na_ref[0])
    def _():
        xb = x_ref[...].astype(BF16)
        a = jnp.dot(xb, w1b[...], preferred_element_type=F32)
        b = jnp.dot(xb, w3b[...], preferred_element_type=F32)
        hd = (jax.nn.silu(a) * b).astype(BF16)
        o_ref[...] = jnp.dot(hd, w2b[...], preferred_element_type=F32)

    @pl.when(i >= na_ref[0])
    def _():
        o_ref[...] = jnp.zeros_like(o_ref)


def _experts(tile_e, n_active, xs, w1, w3, w2, tme):
    n_rows = xs.shape[0]
    row = pl.BlockSpec((tme, D_MODEL), lambda i, te, na: (i, 0))
    return pl.pallas_call(
        _experts_body,
        grid_spec=pltpu.PrefetchScalarGridSpec(
            num_scalar_prefetch=2,
            grid=(n_rows // tme,),
            in_specs=[row,
                      pl.BlockSpec((1, D_MODEL, D_EXPERT), lambda i, te, na: (te[i], 0, 0)),
                      pl.BlockSpec((1, D_MODEL, D_EXPERT), lambda i, te, na: (te[i], 0, 0)),
                      pl.BlockSpec((1, D_EXPERT, D_MODEL), lambda i, te, na: (te[i], 0, 0))],
            out_specs=row,
            scratch_shapes=[pltpu.VMEM((D_MODEL, D_EXPERT), BF16), pltpu.VMEM((D_MODEL, D_EXPERT), BF16),
                            pltpu.VMEM((D_EXPERT, D_MODEL), BF16)],
        ),
        out_shape=jax.ShapeDtypeStruct((n_rows, D_MODEL), F32),
        compiler_params=_params(1),
        name="experts",
    )(tile_e, n_active, xs, w1, w3, w2)


def _combine_body(pos_ref, gate_ref, h1_ref, fg_ref, ys_hbm, y_ref, ybuf, sem, *, tm):
    def issue(t, c):
        for k in range(2):
            _row_copy(ys_hbm, pos_ref[k, t], ybuf.at[k], t, sem).start(priority=k)
        return c

    lax.fori_loop(0, tm, issue, 0, unroll=8)
    for k in range(2):
        _wait_rows(ys_hbm.at[pl.ds(0, tm)], ybuf.at[k], sem)

    eye = lax.broadcasted_iota(I32, (tm, tm), 0) == lax.broadcasted_iota(I32, (tm, tm), 1)
    g1 = jnp.sum(jnp.where(eye, gate_ref[0:1, :], 0.0), axis=1, keepdims=True)
    g2 = jnp.sum(jnp.where(eye, gate_ref[1:2, :], 0.0), axis=1, keepdims=True)
    h2 = h1_ref[...] + (g1 * ybuf[0] + g2 * ybuf[1])
    y_ref[...] = _rmsnorm(h2, fg_ref[...])


def _combine(pos, gates, h1, fg, ys, tm):
    T = h1.shape[0]
    return pl.pallas_call(
        functools.partial(_combine_body, tm=tm),
        grid=(T // tm,),
        in_specs=[pl.BlockSpec((SUBLANES, tm), lambda i: (0, i), memory_space=pltpu.SMEM),
                  pl.BlockSpec((SUBLANES, tm), lambda i: (0, i)),
                  pl.BlockSpec((tm, D_MODEL), lambda i: (i, 0)),
                  pl.BlockSpec((1, D_MODEL), lambda i: (0, 0)),
                  pl.BlockSpec(memory_space=pl.ANY)],
        out_specs=pl.BlockSpec((tm, D_MODEL), lambda i: (i, 0)),
        scratch_shapes=[pltpu.VMEM((2, tm, D_MODEL), F32), pltpu.SemaphoreType.DMA(())],
        out_shape=jax.ShapeDtypeStruct((T, D_MODEL), F32),
        compiler_params=_params(1),
        name="combine",
    )(pos, gates, h1, fg, ys)


def _blockdiag(w):
    per = GATE_TILE // LRU_BLOCK
    w4 = w.reshape(LRU_BLOCKS // per, per, LRU_BLOCK, LRU_BLOCK)
    eye = jnp.eye(per, dtype=w.dtype)
    return jnp.einsum('jbio,bc->jbico', w4, eye).reshape(LRU_BLOCKS // per, GATE_TILE, GATE_TILE)


def _pad_rows(c):
    return jnp.pad(c, ((0, 0), (SUBLANES - (CONV_W - 1), 0), (0, 0)))


def _router_cols(group_part, expert_part):
    r = group_part.shape[0]
    out = jnp.zeros((r, LANES), F32)
    out = out.at[:, 0:MOE_GROUPS].set(group_part)
    return out.at[:, ROUTER_E0:ROUTER_E0 + N_EXPERTS].set(expert_part)


def _lane_row(v, width=LANES):
    return jnp.pad(v, (0, width - v.shape[0])).reshape(1, width)


def _prep(norm_mix_g, w_in, lru_conv_w, lru_conv_b, lru_wa, lru_ba, lru_wx, lru_bx, lru_lambda,
          ssd_conv_w, ssd_conv_b, ssd_dt_bias, ssd_a_log, ssd_d, ssd_norm_g, w_out,
          norm_ffn_g, router_group_w, router_group_b, router_expert_w, router_expert_b,
          moe_w1, moe_w3, moe_w2, final_norm_g):
    w = w_in[0]
    wr = _router_cols(router_group_w[0], router_expert_w[0])
    wr_hi = wr.astype(BF16)
    P = dict(
        g_mix=norm_mix_g[0].reshape(1, D_MODEL),
        w_main=w[:, :N_MAIN].astype(BF16),
        w_dt=jnp.pad(w[:, N_MAIN:], ((0, 0), (0, LANES - SSD_HEADS))).astype(BF16),
        lru_cw=lru_conv_w[0], lru_cb=lru_conv_b[0].reshape(1, D_LRU),
        wbd=jnp.concatenate([_blockdiag(lru_wa[0]), _blockdiag(lru_wx[0])], axis=2).astype(BF16),
        ba=lru_ba[0].reshape(1, D_LRU), bx=lru_bx[0].reshape(1, D_LRU), lam=lru_lambda[0].reshape(1, D_LRU),
        ssd_cw=ssd_conv_w[0], ssd_cb=ssd_conv_b[0].reshape(1, D_XBC),
        dtb=_lane_row(ssd_dt_bias[0]), alog=_lane_row(ssd_a_log[0]),
        dvec=jnp.repeat(ssd_d[0], SSD_HEADDIM).reshape(1, D_SSM),
        ng=ssd_norm_g[0].reshape(1, D_SSM),
        w_out=w_out[0].astype(BF16),
        g_ffn=norm_ffn_g[0].reshape(1, D_MODEL),
        wr=jnp.stack([wr_hi, (wr - wr_hi.astype(F32)).astype(BF16)]),
        br=_router_cols(router_group_b[0][None], router_expert_b[0][None]),
        w1=moe_w1[0], w3=moe_w3[0], w2=moe_w2[0],
        g_final=final_norm_g.reshape(1, D_MODEL),
    )
    return P


def _expert_layout(counts, n_pairs, tme):
    cnt = counts[:, 0].astype(I32)
    padded = ((cnt + tme - 1) // tme) * tme
    ends = jnp.cumsum(padded)
    offs = ends - padded
    n_tiles = n_pairs // tme + N_EXPERTS
    n_active = ends[-1] // tme
    starts = jnp.arange(n_tiles, dtype=I32) * tme
    tile_e = jnp.sum(starts[:, None] >= ends[None, :], axis=1).astype(I32)
    last_e = jnp.sum((n_active - 1) * tme >= ends).astype(I32)
    tile_e = jnp.where(jnp.arange(n_tiles) < n_active, tile_e, last_e)
    return offs.astype(I32), tile_e, n_active.reshape(1).astype(I32), n_tiles * tme


def _group(x, lru_h0, lru_c0, ssd_h0, ssd_c0, P, start_pos):
    B, L, _ = x.shape
    T = B * L
    Tt = min(ROW_TILE, L)
    q = min(SSD_CHUNK, L)
    tm = min(ROW_TILE, T)
    x2d = x.reshape(T, D_MODEL)

    x_lru, g_lru, z, xbc, dt = _in_proj(x2d, P['g_mix'], P['w_main'], P['w_dt'], tm)
    seq = lambda a: a.reshape(B, L, a.shape[-1])
    y_lru, lru_h, lru_c = _lru(seq(x_lru), seq(g_lru), _pad_rows(lru_c0), lru_h0.reshape(B, 1, D_LRU),
                               P['lru_cw'], P['lru_cb'], P['wbd'], P['ba'], P['bx'], P['lam'], Tt, start_pos)
    y_ssd, ssd_h, ssd_c = _ssd(seq(xbc), seq(z), seq(dt), _pad_rows(ssd_c0), ssd_h0,
                               P['ssd_cw'], P['ssd_cb'], P['dtb'], P['alog'], P['dvec'], P['ng'], Tt, q)
    h1, xn, meta, gates, counts = _out_router(
        x2d, y_lru.reshape(T, D_LRU), y_ssd.reshape(T, D_SSM), P['w_out'], P['g_ffn'],
        P['wr'], P['br'], tm)

    tme = min(ROW_TILE, max(SUBLANES, T // 8))
    offs, tile_e, n_active, n_rows = _expert_layout(counts, 2 * T, tme)
    pos = _positions(offs, meta)
    xs = _dispatch(pos, xn, n_rows, tm)
    ys = _experts(tile_e, n_active, xs, P['w1'], P['w3'], P['w2'], tme)
    y = _combine(pos, gates, h1, P['g_final'], ys, tm)

    hist = SUBLANES - (CONV_W - 1)
    return (y.reshape(B, L, D_MODEL), lru_h.reshape(1, B, D_LRU), lru_c[:, hist:][None],
            ssd_h[None], ssd_c[:, hist:][None])


def kernel(x_prompt, x_sample, state_lru_h, state_lru_conv, state_ssd, state_ssd_conv, norm_mix_g, w_in, lru_conv_w, lru_conv_b, lru_wa, lru_ba, lru_wx, lru_bx, lru_lambda, ssd_conv_w, ssd_conv_b, ssd_dt_bias, ssd_a_log, ssd_d, ssd_norm_g, w_out, norm_ffn_g, router_group_w, router_group_b, router_expert_w, router_expert_b, moe_w1, moe_w3, moe_w2, final_norm_g):
    P = _prep(norm_mix_g, w_in, lru_conv_w, lru_conv_b, lru_wa, lru_ba, lru_wx, lru_bx, lru_lambda,
              ssd_conv_w, ssd_conv_b, ssd_dt_bias, ssd_a_log, ssd_d, ssd_norm_g, w_out,
              norm_ffn_g, router_group_w, router_group_b, router_expert_w, router_expert_b,
              moe_w1, moe_w3, moe_w2, final_norm_g)
    bp = x_prompt.shape[0]
    yp, a1, a2, a3, a4 = _group(
        x_prompt,
        jnp.zeros((bp, D_LRU), F32), jnp.zeros((bp, CONV_W - 1, D_LRU), F32),
        jnp.zeros((bp, SSD_HEADS, SSD_HEADDIM, D_STATE), F32), jnp.zeros((bp, CONV_W - 1, D_XBC), F32),
        P, 0)
    ys, b1, b2, b3, b4 = _group(x_sample, state_lru_h[0], state_lru_conv[0], state_ssd[0], state_ssd_conv[0],
                                P, PAST_LEN)
    return (yp, ys, a1, a2, a3, a4, b1, b2, b3, b4)
```

```python
import functools

import jax
import jax.numpy as jnp
from jax import lax
from jax.experimental import pallas as pl
from jax.experimental.pallas import tpu as pltpu

F32 = jnp.float32
BF16 = jnp.bfloat16
I32 = jnp.int32

D_MODEL = 1024
D_LRU = 1024
LRU_BLOCKS = 16
LRU_BLOCK = 64
LRU_C = 8.0
CONV_W = 4
D_SSM = 1024
SSD_HEADDIM = 64
SSD_HEADS = 16
SSD_GROUPS = 2
SSD_HPG = 8
D_STATE = 128
D_XBC = 1536
N_MAIN = 2 * D_LRU + D_SSM + D_XBC
MOE_GROUPS = 4
EXPERTS_PER_GROUP = 8
N_EXPERTS = 32
D_EXPERT = 512
EPS = 1e-6
SSD_CHUNK = 64
PAST_LEN = 1024

LANES = 128
SUBLANES = 8
GATE_TILE = 256
ROW_TILE = 256
ROUTER_E0 = 32
VMEM_LIMIT = 52 * 1024 * 1024

_NT = (((1,), (1,)), ((), ()))
_TN = (((0,), (0,)), ((), ()))


def _params(n_axes):
    return pltpu.CompilerParams(dimension_semantics=("arbitrary",) * n_axes,
                                vmem_limit_bytes=VMEM_LIMIT)


def _rmsnorm(x, g):
    return x * lax.rsqrt(jnp.mean(x * x, axis=-1, keepdims=True) + EPS) * g


def _full(shape):
    n = len(shape)
    return pl.BlockSpec(shape, lambda *_: (0,) * n)


def _in_proj_body(x_ref, g_ref, w_ref, wdt_ref, xl_ref, gl_ref, z_ref, xbc_ref, dt_ref):
    xb = _rmsnorm(x_ref[...], g_ref[...]).astype(BF16)

    def mm(lo, hi):
        return jnp.dot(xb, w_ref[:, lo:hi], preferred_element_type=F32)

    xl_ref[...] = mm(0, D_LRU)
    gl_ref[...] = mm(D_LRU, 2 * D_LRU)
    z_ref[...] = mm(2 * D_LRU, 2 * D_LRU + D_SSM)
    xbc_ref[...] = mm(2 * D_LRU + D_SSM, N_MAIN)
    dt_ref[...] = jnp.dot(xb, wdt_ref[...], preferred_element_type=F32)


def _in_proj(x2d, g, w_main, w_dt, tm):
    T = x2d.shape[0]
    row = lambda w: pl.BlockSpec((tm, w), lambda i: (i, 0))
    widths = (D_LRU, D_LRU, D_SSM, D_XBC, LANES)
    return pl.pallas_call(
        _in_proj_body,
        grid=(T // tm,),
        in_specs=[row(D_MODEL), _full((1, D_MODEL)), _full((D_MODEL, N_MAIN)), _full((D_MODEL, LANES))],
        out_specs=[row(w) for w in widths],
        out_shape=[jax.ShapeDtypeStruct((T, w), F32) for w in widths],
        compiler_params=_params(1),
        name="in_proj",
    )(x2d, g, w_main, w_dt)


def _causal_conv(xpad, cw_ref, cb_ref, Tt):
    cw = cw_ref[...]
    full = xpad[...]
    y = cb_ref[...]
    for k in range(CONV_W):
        shift = CONV_W - 1 - k
        xk = pltpu.roll(full, shift, 0) if shift else full
        y = y + xk[SUBLANES:SUBLANES + Tt, :] * cw[k:k + 1, :]
    return y


def _load_tile(t, xpad, c0_ref, x_ref, Tt):
    @pl.when(t == 0)
    def _():
        xpad[0:SUBLANES, :] = c0_ref[0]

    @pl.when(t > 0)
    def _():
        xpad[0:SUBLANES, :] = xpad[Tt:Tt + SUBLANES, :]

    xpad[SUBLANES:SUBLANES + Tt, :] = x_ref[0]


def _lru_body(x_ref, g_ref, c0_ref, h0_ref, cw_ref, cb_ref, wbd_ref, ba_ref, bx_ref, lam_ref,
              y_ref, hN_ref, cN_ref, xpad, a_s, u_s, hcar, *, Tt, start_pos):
    t = pl.program_id(1)
    _load_tile(t, xpad, c0_ref, x_ref, Tt)

    @pl.when(t == 0)
    def _():
        hcar[...] = jnp.broadcast_to(h0_ref[0], (SUBLANES, D_LRU))

    xc = _causal_conv(xpad, cw_ref, cb_ref, Tt)
    xcb = xc.astype(BF16)
    sp = jax.nn.softplus(-lam_ref[...])
    pos0 = (lax.broadcasted_iota(I32, (Tt, 1), 0) + t * Tt + start_pos) == 0
    for j in range(D_LRU // GATE_TILE):
        sl = slice(GATE_TILE * j, GATE_TILE * (j + 1))
        ga = jnp.dot(xcb[:, sl], wbd_ref[j], preferred_element_type=F32)
        r = jax.nn.sigmoid(ga[:, :GATE_TILE] + ba_ref[:, sl])
        i = jax.nn.sigmoid(ga[:, GATE_TILE:] + bx_ref[:, sl])
        a = jnp.exp((-LRU_C * r) * sp[:, sl])
        mult = jnp.where(pos0, 1.0, jnp.sqrt(1.0 - a * a))
        a_s[:, sl] = a
        u_s[:, sl] = mult * i * xc[:, sl]

    rows = lax.broadcasted_iota(I32, (SUBLANES, D_LRU), 0)

    def scan8(gi, hprev):
        r0 = pl.multiple_of(gi * SUBLANES, SUBLANES)
        a8 = a_s[pl.ds(r0, SUBLANES), :]
        u8 = u_s[pl.ds(r0, SUBLANES), :]
        for s in (1, 2, 4):
            ok = rows >= s
            u_sh = pltpu.roll(u8, s, 0)
            a_sh = pltpu.roll(a8, s, 0)
            u8 = jnp.where(ok, u8 + a8 * u_sh, u8)
            a8 = jnp.where(ok, a8 * a_sh, a8)
        h8 = u8 + a8 * hprev
        u_s[pl.ds(r0, SUBLANES), :] = h8
        return jnp.broadcast_to(h8[SUBLANES - 1:SUBLANES, :], (SUBLANES, D_LRU))

    hlast = lax.fori_loop(0, Tt // SUBLANES, scan8, hcar[...])
    hcar[...] = hlast
    y_ref[0] = (u_s[...] * jax.nn.gelu(g_ref[0], approximate=True)).astype(BF16)
    hN_ref[0] = hlast[0:1, :]
    cN_ref[0] = xpad[Tt:Tt + SUBLANES, :]


def _lru(x_lru, g_lru, c0, h0, cw, cb, wbd, ba, bx, lam, Tt, start_pos):
    B, L, _ = x_lru.shape
    seq = pl.BlockSpec((1, Tt, D_LRU), lambda b, t: (b, t, 0))
    per_b = lambda r: pl.BlockSpec((1, r, D_LRU), lambda b, t: (b, 0, 0))
    return pl.pallas_call(
        functools.partial(_lru_body, Tt=Tt, start_pos=start_pos),
        grid=(B, L // Tt),
        in_specs=[seq, seq, per_b(SUBLANES), per_b(1), _full((CONV_W, D_LRU)), _full((1, D_LRU)),
                  _full(wbd.shape), _full((1, D_LRU)), _full((1, D_LRU)), _full((1, D_LRU))],
        out_specs=[seq, per_b(1), per_b(SUBLANES)],
        out_shape=[jax.ShapeDtypeStruct((B, L, D_LRU), BF16),
                   jax.ShapeDtypeStruct((B, 1, D_LRU), F32),
                   jax.ShapeDtypeStruct((B, SUBLANES, D_LRU), F32)],
        scratch_shapes=[pltpu.VMEM((Tt + SUBLANES, D_LRU), F32), pltpu.VMEM((Tt, D_LRU), F32),
                        pltpu.VMEM((Tt, D_LRU), F32), pltpu.VMEM((SUBLANES, D_LRU), F32)],
        compiler_params=_params(2),
        name="lru",
    )(x_lru, g_lru, c0, h0, cw, cb, wbd, ba, bx, lam)


def _split3(v):
    hi = v.astype(BF16)
    r1 = v - hi.astype(F32)
    mid = r1.astype(BF16)
    lo = (r1 - mid.astype(F32)).astype(BF16)
    return hi, mid, lo


def _pad_time(v, rows):
    if v.shape[0] == rows:
        return v
    return jnp.concatenate([v, jnp.zeros((rows - v.shape[0], v.shape[1]), v.dtype)], axis=0)


def _ssd_masks(q):
    P = SSD_HEADDIM
    tri = jnp.arange(q)[:, None] >= jnp.arange(q)[None, :]
    expand = jnp.arange(LANES)[:, None] == jnp.arange(D_SSM)[None, :] // P
    row_q = jnp.arange(q)[:, None]
    lane_k = jnp.arange(D_SSM)[None, :] % P
    diag = row_q == lane_k
    causal = row_q >= lane_k
    bd = jnp.arange(GATE_TILE)[:, None] // P == jnp.arange(GATE_TILE)[None, :] // P
    return (tri.astype(BF16), expand.astype(BF16), diag.astype(F32), causal.astype(F32), bd.astype(BF16))


def _ssd_body(xbc_ref, z_ref, dt_ref, c0_ref, s0_ref, cw_ref, cb_ref, dtb_ref, alog_ref, dvec_ref, ng_ref,
              tri_ref, expand_ref, diag_ref, causal_ref, bd_ref,
              y_ref, sN_ref, cN_ref, xpad, xa_s, y_s, st_s, *, Tt, q):
    t = pl.program_id(1)
    _load_tile(t, xpad, c0_ref, xbc_ref, Tt)

    @pl.when(t == 0)
    def _():
        st_s[...] = s0_ref[0].reshape(D_SSM, D_STATE).T

    xc = _causal_conv(xpad, cw_ref, cb_ref, Tt)
    xa_s[...] = jax.nn.silu(xc)
    A = -jnp.exp(alog_ref[...])
    P = SSD_HEADDIM
    blk = GATE_TILE // P
    off_b = D_SSM
    off_c = D_SSM + SSD_GROUPS * D_STATE
    gw = D_SSM // SSD_GROUPS

    def exact01(parts, w01, left):
        one = (lambda p: jnp.dot(w01, p, preferred_element_type=F32)) if left else (
            lambda p: jnp.dot(p, w01, preferred_element_type=F32))
        hi, mid, lo = parts
        return (one(lo) + one(mid)) + one(hi)

    def chunk(c, carry):
        r0 = pl.multiple_of(c * q, q)
        xs = xa_s[pl.ds(r0, q), 0:D_SSM]
        xsb = xs.astype(BF16)
        dt = jax.nn.softplus(dt_ref[0, pl.ds(r0, q), :] + dtb_ref[...])
        cs = exact01(_split3(dt * A), tri_ref[...], left=True)
        E = exact01(_split3(jnp.concatenate([cs, dt], axis=0)), expand_ref[...], left=False)
        diag = diag_ref[...] != 0.0
        causal = causal_ref[...] != 0.0
        E_cs = E[0:q]
        E_dt = E[q:2 * q]
        r_cs = jnp.sum(jnp.where(diag, E_cs, 0.0), axis=0, keepdims=True)
        r_dt = jnp.sum(jnp.where(diag, E_dt, 0.0), axis=0, keepdims=True)
        Lm = jnp.where(causal, jnp.exp(jnp.where(causal, E_cs - r_cs, 0.0)), 0.0)
        Bs, Cs, CBs = [], [], []
        for g in range(SSD_GROUPS):
            Bg = xa_s[pl.ds(r0, q), off_b + g * D_STATE:off_b + (g + 1) * D_STATE].astype(BF16)
            Cg = xa_s[pl.ds(r0, q), off_c + g * D_STATE:off_c + (g + 1) * D_STATE].astype(BF16)
            Bt = jnp.concatenate([_pad_time(Bg, P)] * SSD_HPG, axis=0)
            CBs.append(lax.dot_general(Cg, Bt, _NT, preferred_element_type=F32))
            Bs.append(Bg)
            Cs.append(Cg)
        Mw = (jnp.concatenate(CBs, axis=1) * Lm * r_dt).astype(BF16)
        yd = []
        for j in range(D_SSM // GATE_TILE):
            sl = slice(j * GATE_TILE, (j + 1) * GATE_TILE)
            slab = _pad_time(xsb[:, sl], P)
            rhs = jnp.concatenate([slab] * blk, axis=0) * bd_ref[...]
            yd.append(jnp.dot(Mw[:, sl], rhs, preferred_element_type=F32))
        cs_last = E_cs[q - 1:q, :]
        xw = (jnp.exp(cs_last - E_cs) * E_dt * xs).astype(BF16)
        cdec = jnp.exp(cs_last)
        yo = []
        for g in range(SSD_GROUPS):
            sl = slice(g * gw, (g + 1) * gw)
            S = st_s[:, sl]
            yo.append(jnp.dot(Cs[g], S.astype(BF16), preferred_element_type=F32))
            st = lax.dot_general(Bs[g], xw[:, sl], _TN, preferred_element_type=F32)
            st_s[:, sl] = cdec[:, sl] * S + st
        y_s[pl.ds(r0, q), :] = jnp.concatenate(yd, axis=1) + jnp.concatenate(yo, axis=1) * jnp.exp(E_cs)
        return carry

    lax.fori_loop(0, Tt // q, chunk, 0)

    @pl.when(t == pl.num_programs(1) - 1)
    def _():
        sN_ref[0] = st_s[...].T.reshape(SSD_HEADS, SSD_HEADDIM, D_STATE)

    y = y_s[...] + dvec_ref[...] * xa_s[:, 0:D_SSM]
    y = y * jax.nn.silu(z_ref[0])
    gw = D_SSM // SSD_GROUPS
    for g in range(SSD_GROUPS):
        sl = slice(g * gw, (g + 1) * gw)
        yg = y[:, sl]
        yg = yg * lax.rsqrt(jnp.mean(yg * yg, axis=-1, keepdims=True) + EPS)
        y_ref[0, :, sl] = (yg * ng_ref[:, sl]).astype(BF16)
    cN_ref[0] = xpad[Tt:Tt + SUBLANES, :]


def _ssd(xbc, z, dt, c0, s0, cw, cb, dtb, alog, dvec, ng, Tt, q):
    B, L, _ = xbc.shape
    seq = lambda w: pl.BlockSpec((1, Tt, w), lambda b, t: (b, t, 0))
    per_b = pl.BlockSpec((1, SUBLANES, D_XBC), lambda b, t: (b, 0, 0))
    st = pl.BlockSpec((1, SSD_HEADS, SSD_HEADDIM, D_STATE), lambda b, t: (b, 0, 0, 0))
    masks = _ssd_masks(q)
    return pl.pallas_call(
        functools.partial(_ssd_body, Tt=Tt, q=q),
        grid=(B, L // Tt),
        in_specs=[seq(D_XBC), seq(D_SSM), seq(LANES), per_b, st, _full((CONV_W, D_XBC)), _full((1, D_XBC)),
                  _full((1, LANES)), _full((1, LANES)), _full((1, D_SSM)), _full((1, D_SSM))]
                 + [_full(m.shape) for m in masks],
        out_specs=[seq(D_SSM), st, per_b],
        out_shape=[jax.ShapeDtypeStruct((B, L, D_SSM), BF16),
                   jax.ShapeDtypeStruct((B, SSD_HEADS, SSD_HEADDIM, D_STATE), F32),
                   jax.ShapeDtypeStruct((B, SUBLANES, D_XBC), F32)],
        scratch_shapes=[pltpu.VMEM((Tt + SUBLANES, D_XBC), F32), pltpu.VMEM((Tt, D_XBC), F32),
                        pltpu.VMEM((Tt, D_SSM), F32), pltpu.VMEM((D_STATE, D_SSM), F32)],
        compiler_params=_params(2),
        name="ssd",
    )(xbc, z, dt, c0, s0, cw, cb, dtb, alog, dvec, ng, *masks)


def _out_router_body(x_ref, yl_ref, ys_ref, wo_ref, gf_ref, wr_ref, br_ref, cin_ref,
                     h1_ref, xn_ref, meta_ref, gate_ref, cnt_ref, carry, *, tm):
    step = pl.program_id(0)

    @pl.when(step == 0)
    def _():
        carry[...] = cin_ref[...]

    mix = jnp.concatenate([yl_ref[...], ys_ref[...]], axis=1)
    h1 = x_ref[...] + jnp.dot(mix, wo_ref[...], preferred_element_type=F32)
    h1_ref[...] = h1
    xn = _rmsnorm(h1, gf_ref[...])
    xn_ref[...] = xn

    xh = xn.astype(BF16)
    xm = (xn - xh.astype(F32)).astype(BF16)
    small = (jnp.dot(xh, wr_ref[1], preferred_element_type=F32)
             + jnp.dot(xm, wr_ref[0], preferred_element_type=F32))
    logits = small + jnp.dot(xh, wr_ref[0], preferred_element_type=F32) + br_ref[...]
    lt = logits.T
    rows8 = lax.broadcasted_iota(I32, (SUBLANES, tm), 0)
    lg = jnp.where(rows8 < MOE_GROUPS, lt[0:SUBLANES, :], -jnp.inf)
    eg = jnp.exp(lg - jnp.max(lg, axis=0, keepdims=True))
    pg = eg / jnp.sum(eg, axis=0, keepdims=True)
    pgs = jnp.max(pg, axis=0, keepdims=True)
    rows8f = rows8.astype(F32)
    gsel = jnp.min(jnp.where(pg == pgs, rows8f, float(SUBLANES)), axis=0, keepdims=True)

    rows32 = lax.broadcasted_iota(I32, (N_EXPERTS, tm), 0)
    rows32f = rows32.astype(F32)
    grp = (rows32 // EXPERTS_PER_GROUP).astype(F32)
    le = lt[ROUTER_E0:ROUTER_E0 + N_EXPERTS, :]
    ing = grp == gsel
    lem = jnp.where(ing, le, -jnp.inf)
    ee = jnp.exp(lem - jnp.max(lem, axis=0, keepdims=True))
    pe = ee / jnp.sum(ee, axis=0, keepdims=True)
    pe1 = jnp.where(ing, pe, -1.0)
    v1 = jnp.max(pe1, axis=0, keepdims=True)
    i1 = jnp.min(jnp.where(pe1 == v1, rows32f, float(N_EXPERTS)), axis=0, keepdims=True)
    pe2 = jnp.where(rows32f == i1, -1.0, pe1)
    v2 = jnp.max(pe2, axis=0, keepdims=True)
    i2 = jnp.min(jnp.where(pe2 == v2, rows32f, float(N_EXPERTS)), axis=0, keepdims=True)
    sv = v1 + v2
    w1 = v1 / sv * pgs
    w2 = v2 / sv * pgs

    oh1 = rows32f == i1
    oh2 = rows32f == i2
    oh = jnp.where(oh1 | oh2, 1.0, 0.0)
    before = (lax.broadcasted_iota(I32, (tm, tm), 0) < lax.broadcasted_iota(I32, (tm, tm), 1))
    pref = jnp.dot(oh.astype(BF16), jnp.where(before, 1.0, 0.0).astype(BF16), preferred_element_type=F32)
    pref = pref + carry[:, 0:1]
    r1 = jnp.sum(jnp.where(oh1, pref, 0.0), axis=0, keepdims=True)
    r2 = jnp.sum(jnp.where(oh2, pref, 0.0), axis=0, keepdims=True)
    carry[...] = carry[...] + jnp.sum(oh, axis=1, keepdims=True)
    cnt_ref[...] = carry[...]

    meta = jnp.where(rows8 == 0, i1, jnp.where(rows8 == 1, i2, jnp.where(rows8 == 2, r1, jnp.where(rows8 == 3, r2, 0.0))))
    meta_ref[...] = meta.astype(I32)
    gate_ref[...] = jnp.where(rows8 == 0, w1, jnp.where(rows8 == 1, w2, 0.0))


def _out_router(x2d, y_lru, y_ssd, w_out, gf, wr, br, counts_in, tm):
    T = x2d.shape[0]
    row = lambda w: pl.BlockSpec((tm, w), lambda i: (i, 0))
    col = pl.BlockSpec((SUBLANES, tm), lambda i: (0, i))
    return pl.pallas_call(
        functools.partial(_out_router_body, tm=tm),
        grid=(T // tm,),
        in_specs=[row(D_MODEL), row(D_LRU), row(D_SSM), _full((D_LRU + D_SSM, D_MODEL)), _full((1, D_MODEL)),
                  _full((2, D_MODEL, LANES)), _full((1, LANES)), _full((N_EXPERTS, LANES))],
        out_specs=[row(D_MODEL), row(D_MODEL), col, col, _full((N_EXPERTS, LANES))],
        out_shape=[jax.ShapeDtypeStruct((T, D_MODEL), F32), jax.ShapeDtypeStruct((T, D_MODEL), F32),
                   jax.ShapeDtypeStruct((SUBLANES, T), I32), jax.ShapeDtypeStruct((SUBLANES, T), F32),
                   jax.ShapeDtypeStruct((N_EXPERTS, LANES), F32)],
        scratch_shapes=[pltpu.VMEM((N_EXPERTS, LANES), F32)],
        compiler_params=_params(1),
        name="out_router",
    )(x2d, y_lru, y_ssd, w_out, gf, wr, br, counts_in)


def _row_copy(src_hbm, src_row, dst, dst_row, sem):
    return pltpu.make_async_copy(src_hbm.at[pl.ds(src_row, 1)], dst.at[pl.ds(dst_row, 1)], sem)


def _positions_body(offs_ref, meta_ref, pos_ref):
    m = meta_ref[...]
    base = jnp.zeros_like(m)
    for e in range(N_EXPERTS):
        base = jnp.where(m == e, offs_ref[e], base)
    pos_ref[...] = base + pltpu.roll(m, SUBLANES - 2, 0)


def _positions(offs, meta):
    T = meta.shape[1]
    tb = min(T, 4096)
    blk = pl.BlockSpec((SUBLANES, tb), lambda i, offs: (0, i))
    return pl.pallas_call(
        _positions_body,
        grid_spec=pltpu.PrefetchScalarGridSpec(num_scalar_prefetch=1, grid=(T // tb,), in_specs=[blk], out_specs=blk),
        out_shape=jax.ShapeDtypeStruct((SUBLANES, T), I32),
        compiler_params=_params(1),
        name="positions",
    )(offs, meta)


def _wait_rows(src, dst, sem):
    pltpu.make_async_copy(src, dst, sem).wait()


def _dispatch_body(zf_ref, pos_ref, *rest, tm, tme, steps):
    n_groups = len(steps)
    xn_refs = rest[:n_groups]
    xs_hbm, zbuf, sem, zsem = rest[n_groups:]
    n_tiles = xs_hbm.shape[0] // tme
    step = pl.program_id(0)

    def zero_copy(i):
        return pltpu.make_async_copy(zbuf, xs_hbm.at[pl.ds(pl.multiple_of(i * tme, tme), tme)], zsem)

    @pl.when(step == 0)
    def _():
        zbuf[...] = jnp.zeros_like(zbuf)

        def start(i, c):
            @pl.when(zf_ref[i] != 0)
            def _():
                zero_copy(i).start()
            return c

        def wait(i, c):
            @pl.when(zf_ref[i] != 0)
            def _():
                zero_copy(i).wait()
            return c

        lax.fori_loop(0, n_tiles, start, 0)
        lax.fori_loop(0, n_tiles, wait, 0)

    first = 0
    for xn_ref, n in zip(xn_refs, steps):
        @pl.when(jnp.logical_and(step >= first, step < first + n))
        def _(xn_ref=xn_ref):
            for t in range(tm):
                for k in range(2):
                    _row_copy(xn_ref, t, xs_hbm, pos_ref[k, t], sem).start(priority=k)
            for k in range(2):
                _wait_rows(xn_ref, xs_hbm.at[pl.ds(0, tm)], sem)
        first += n


def _dispatch(zflag, pos, xns, n_rows, tm, tme):
    steps = [xn.shape[0] // tm for xn in xns]
    firsts = [sum(steps[:g]) for g in range(len(steps))]

    def group_spec(first, n):
        return pl.BlockSpec((tm, D_MODEL), lambda i, zf: (jnp.clip(i - first, 0, n - 1), 0))

    return pl.pallas_call(
        functools.partial(_dispatch_body, tm=tm, tme=tme, steps=tuple(steps)),
        grid_spec=pltpu.PrefetchScalarGridSpec(
            num_scalar_prefetch=1,
            grid=(sum(steps),),
            in_specs=[pl.BlockSpec((SUBLANES, tm), lambda i, zf: (0, i), memory_space=pltpu.SMEM)]
                     + [group_spec(f, n) for f, n in zip(firsts, steps)],
            out_specs=pl.BlockSpec(memory_space=pl.ANY),
            scratch_shapes=[pltpu.VMEM((tme, D_MODEL), F32), pltpu.SemaphoreType.DMA(()),
                            pltpu.SemaphoreType.DMA(())],
        ),
        out_shape=jax.ShapeDtypeStruct((n_rows, D_MODEL), F32),
        compiler_params=_params(1),
        name="dispatch",
    )(zflag, pos, *xns)


def _experts_body(te_ref, na_ref, x_ref, w1_ref, w3_ref, w2_ref, o_ref, w1b, w3b, w2b):
    i = pl.program_id(0)
    changed = jnp.logical_or(i == 0, te_ref[i] != te_ref[jnp.maximum(i - 1, 0)])

    @pl.when(changed)
    def _():
        w1b[...] = w1_ref[0].astype(BF16)
        w3b[...] = w3_ref[0].astype(BF16)
        w2b[...] = w2_ref[0].astype(BF16)

    @pl.when(i < na_ref[0])
    def _():
        xb = x_ref[...].astype(BF16)
        a = jnp.dot(xb, w1b[...], preferred_element_type=F32)
        b = jnp.dot(xb, w3b[...], preferred_element_type=F32)
        hd = (jax.nn.silu(a) * b).astype(BF16)
        o_ref[...] = jnp.dot(hd, w2b[...], preferred_element_type=F32)

    @pl.when(i >= na_ref[0])
    def _():
        o_ref[...] = jnp.zeros_like(o_ref)


def _experts(tile_e, n_active, xs, w1, w3, w2, tme):
    n_rows = xs.shape[0]
    row = pl.BlockSpec((tme, D_MODEL), lambda i, te, na: (i, 0))
    return pl.pallas_call(
        _experts_body,
        grid_spec=pltpu.PrefetchScalarGridSpec(
            num_scalar_prefetch=2,
            grid=(n_rows // tme,),
            in_specs=[row,
                      pl.BlockSpec((1, D_MODEL, D_EXPERT), lambda i, te, na: (te[i], 0, 0)),
                      pl.BlockSpec((1, D_MODEL, D_EXPERT), lambda i, te, na: (te[i], 0, 0)),
                      pl.BlockSpec((1, D_EXPERT, D_MODEL), lambda i, te, na: (te[i], 0, 0))],
            out_specs=row,
            scratch_shapes=[pltpu.VMEM((D_MODEL, D_EXPERT), BF16), pltpu.VMEM((D_MODEL, D_EXPERT), BF16),
                            pltpu.VMEM((D_EXPERT, D_MODEL), BF16)],
        ),
        out_shape=jax.ShapeDtypeStruct((n_rows, D_MODEL), F32),
        compiler_params=_params(1),
        name="experts",
    )(tile_e, n_active, xs, w1, w3, w2)


def _combine_body(pos_ref, gate_ref, h1_ref, fg_ref, ys_hbm, y_ref, ybuf, sem, *, tm):
    for t in range(tm):
        for k in range(2):
            _row_copy(ys_hbm, pos_ref[k, t], ybuf.at[k], t, sem).start(priority=k)
    for k in range(2):
        _wait_rows(ys_hbm.at[pl.ds(0, tm)], ybuf.at[k], sem)

    eye = lax.broadcasted_iota(I32, (tm, tm), 0) == lax.broadcasted_iota(I32, (tm, tm), 1)
    g1 = jnp.sum(jnp.where(eye, gate_ref[0:1, :], 0.0), axis=1, keepdims=True)
    g2 = jnp.sum(jnp.where(eye, gate_ref[1:2, :], 0.0), axis=1, keepdims=True)
    h2 = h1_ref[...] + (g1 * ybuf[0] + g2 * ybuf[1])
    y_ref[...] = _rmsnorm(h2, fg_ref[...])


def _combine(pos, gates, h1, fg, ys, tm):
    T = h1.shape[0]
    return pl.pallas_call(
        functools.partial(_combine_body, tm=tm),
        grid=(T // tm,),
        in_specs=[pl.BlockSpec((SUBLANES, tm), lambda i: (0, i), memory_space=pltpu.SMEM),
                  pl.BlockSpec((SUBLANES, tm), lambda i: (0, i)),
                  pl.BlockSpec((tm, D_MODEL), lambda i: (i, 0)),
                  pl.BlockSpec((1, D_MODEL), lambda i: (0, 0)),
                  pl.BlockSpec(memory_space=pl.ANY)],
        out_specs=pl.BlockSpec((tm, D_MODEL), lambda i: (i, 0)),
        scratch_shapes=[pltpu.VMEM((2, tm, D_MODEL), F32), pltpu.SemaphoreType.DMA(())],
        out_shape=jax.ShapeDtypeStruct((T, D_MODEL), F32),
        compiler_params=_params(1),
        name="combine",
    )(pos, gates, h1, fg, ys)


def _blockdiag(w):
    per = GATE_TILE // LRU_BLOCK
    w4 = w.reshape(LRU_BLOCKS // per, per, LRU_BLOCK, LRU_BLOCK)
    eye = jnp.eye(per, dtype=w.dtype)
    return jnp.einsum('jbio,bc->jbico', w4, eye).reshape(LRU_BLOCKS // per, GATE_TILE, GATE_TILE)


def _pad_rows(c):
    return jnp.pad(c, ((0, 0), (SUBLANES - (CONV_W - 1), 0), (0, 0)))


def _router_cols(group_part, expert_part):
    r = group_part.shape[0]
    out = jnp.zeros((r, LANES), F32)
    out = out.at[:, 0:MOE_GROUPS].set(group_part)
    return out.at[:, ROUTER_E0:ROUTER_E0 + N_EXPERTS].set(expert_part)


def _lane_row(v, width=LANES):
    return jnp.pad(v, (0, width - v.shape[0])).reshape(1, width)


def _prep(norm_mix_g, w_in, lru_conv_w, lru_conv_b, lru_wa, lru_ba, lru_wx, lru_bx, lru_lambda,
          ssd_conv_w, ssd_conv_b, ssd_dt_bias, ssd_a_log, ssd_d, ssd_norm_g, w_out,
          norm_ffn_g, router_group_w, router_group_b, router_expert_w, router_expert_b,
          moe_w1, moe_w3, moe_w2, final_norm_g):
    w = w_in[0]
    wr = _router_cols(router_group_w[0], router_expert_w[0])
    wr_hi = wr.astype(BF16)
    P = dict(
        g_mix=norm_mix_g[0].reshape(1, D_MODEL),
        w_main=w[:, :N_MAIN].astype(BF16),
        w_dt=jnp.pad(w[:, N_MAIN:], ((0, 0), (0, LANES - SSD_HEADS))).astype(BF16),
        lru_cw=lru_conv_w[0], lru_cb=lru_conv_b[0].reshape(1, D_LRU),
        wbd=jnp.concatenate([_blockdiag(lru_wa[0]), _blockdiag(lru_wx[0])], axis=2).astype(BF16),
        ba=lru_ba[0].reshape(1, D_LRU), bx=lru_bx[0].reshape(1, D_LRU), lam=lru_lambda[0].reshape(1, D_LRU),
        ssd_cw=ssd_conv_w[0], ssd_cb=ssd_conv_b[0].reshape(1, D_XBC),
        dtb=_lane_row(ssd_dt_bias[0]), alog=_lane_row(ssd_a_log[0]),
        dvec=jnp.repeat(ssd_d[0], SSD_HEADDIM).reshape(1, D_SSM),
        ng=ssd_norm_g[0].reshape(1, D_SSM),
        w_out=w_out[0].astype(BF16),
        g_ffn=norm_ffn_g[0].reshape(1, D_MODEL),
        wr=jnp.stack([wr_hi, (wr - wr_hi.astype(F32)).astype(BF16)]),
        br=_router_cols(router_group_b[0][None], router_expert_b[0][None]),
        w1=moe_w1[0], w3=moe_w3[0], w2=moe_w2[0],
        g_final=final_norm_g.reshape(1, D_MODEL),
    )
    return P


def _expert_layout(counts, n_pairs, tme):
    cnt = counts[:, 0].astype(I32)
    padded = ((cnt + tme - 1) // tme) * tme
    ends = jnp.cumsum(padded)
    offs = ends - padded
    n_tiles = n_pairs // tme + N_EXPERTS
    n_active = ends[-1] // tme
    tiles = jnp.arange(n_tiles, dtype=I32)
    tile_e = jnp.sum((tiles * tme)[:, None] >= ends[None, :], axis=1).astype(I32)
    last_e = jnp.sum((n_active - 1) * tme >= ends).astype(I32)
    tile_e = jnp.where(tiles < n_active, tile_e, last_e)
    is_last = jnp.any((tiles[:, None] + 1) * tme == ends[None, :], axis=1)
    zflag = jnp.logical_or(is_last, tiles >= n_active).astype(I32)
    return offs.astype(I32), tile_e, n_active.reshape(1).astype(I32), zflag, n_tiles * tme


def _mixer_router(x, lru_h0, lru_c0, ssd_h0, ssd_c0, P, start_pos, counts_in):
    B, L, _ = x.shape
    T = B * L
    Tt = min(ROW_TILE, L)
    q = min(SSD_CHUNK, L)
    tm = min(ROW_TILE, T)
    x2d = x.reshape(T, D_MODEL)

    x_lru, g_lru, z, xbc, dt = _in_proj(x2d, P['g_mix'], P['w_main'], P['w_dt'], tm)
    seq = lambda a: a.reshape(B, L, a.shape[-1])
    y_lru, lru_h, lru_c = _lru(seq(x_lru), seq(g_lru), _pad_rows(lru_c0), lru_h0.reshape(B, 1, D_LRU),
                               P['lru_cw'], P['lru_cb'], P['wbd'], P['ba'], P['bx'], P['lam'], Tt, start_pos)
    y_ssd, ssd_h, ssd_c = _ssd(seq(xbc), seq(z), seq(dt), _pad_rows(ssd_c0), ssd_h0,
                               P['ssd_cw'], P['ssd_cb'], P['dtb'], P['alog'], P['dvec'], P['ng'], Tt, q)
    h1, xn, meta, gates, counts = _out_router(
        x2d, y_lru.reshape(T, D_LRU), y_ssd.reshape(T, D_SSM), P['w_out'], P['g_ffn'],
        P['wr'], P['br'], counts_in, tm)

    hist = SUBLANES - (CONV_W - 1)
    states = (lru_h.reshape(1, B, D_LRU), lru_c[:, hist:][None], ssd_h[None], ssd_c[:, hist:][None])
    return dict(h1=h1, xn=xn, meta=meta, gates=gates, shape=(B, L, D_MODEL), tm=tm), counts, states


def _moe_final(groups, counts, P):
    tme = ROW_TILE
    n_pairs = 2 * sum(g['h1'].shape[0] for g in groups)
    offs, tile_e, n_active, zflag, n_rows = _expert_layout(counts, n_pairs, tme)
    tm = groups[0]['tm']
    assert all(g['tm'] == tm for g in groups)
    for g in groups:
        g['pos'] = _positions(offs, g['meta'])
    pos_all = jnp.concatenate([g['pos'] for g in groups], axis=1)
    xs = _dispatch(zflag, pos_all, [g['xn'] for g in groups], n_rows, tm, tme)
    ys = _experts(tile_e, n_active, xs, P['w1'], P['w3'], P['w2'], tme)
    return [_combine(g['pos'], g['gates'], g['h1'], P['g_final'], ys, g['tm']).reshape(g['shape'])
            for g in groups]


def kernel(x_prompt, x_sample, state_lru_h, state_lru_conv, state_ssd, state_ssd_conv, norm_mix_g, w_in, lru_conv_w, lru_conv_b, lru_wa, lru_ba, lru_wx, lru_bx, lru_lambda, ssd_conv_w, ssd_conv_b, ssd_dt_bias, ssd_a_log, ssd_d, ssd_norm_g, w_out, norm_ffn_g, router_group_w, router_group_b, router_expert_w, router_expert_b, moe_w1, moe_w3, moe_w2, final_norm_g):
    P = _prep(norm_mix_g, w_in, lru_conv_w, lru_conv_b, lru_wa, lru_ba, lru_wx, lru_bx, lru_lambda,
              ssd_conv_w, ssd_conv_b, ssd_dt_bias, ssd_a_log, ssd_d, ssd_norm_g, w_out,
              norm_ffn_g, router_group_w, router_group_b, router_expert_w, router_expert_b,
              moe_w1, moe_w3, moe_w2, final_norm_g)
    bp = x_prompt.shape[0]
    gp, counts, (a1, a2, a3, a4) = _mixer_router(
        x_prompt,
        jnp.zeros((bp, D_LRU), F32), jnp.zeros((bp, CONV_W - 1, D_LRU), F32),
        jnp.zeros((bp, SSD_HEADS, SSD_HEADDIM, D_STATE), F32), jnp.zeros((bp, CONV_W - 1, D_XBC), F32),
        P, 0, jnp.zeros((N_EXPERTS, LANES), F32))
    gs, counts, (b1, b2, b3, b4) = _mixer_router(
        x_sample, state_lru_h[0], state_lru_conv[0], state_ssd[0], state_ssd_conv[0], P, PAST_LEN, counts)
    yp, ys = _moe_final([gp, gs], counts, P)
    return (yp, ys, a1, a2, a3, a4, b1, b2, b3, b4)
```

```python
import functools

import jax
import jax.numpy as jnp
from jax import lax
from jax.experimental import pallas as pl
from jax.experimental.pallas import tpu as pltpu

F32 = jnp.float32
BF16 = jnp.bfloat16
I32 = jnp.int32

D_MODEL = 1024
D_LRU = 1024
LRU_BLOCKS = 16
LRU_BLOCK = 64
LRU_C = 8.0
CONV_W = 4
D_SSM = 1024
SSD_HEADDIM = 64
SSD_HEADS = 16
SSD_GROUPS = 2
SSD_HPG = 8
D_STATE = 128
D_XBC = 1536
MOE_GROUPS = 4
EXPERTS_PER_GROUP = 8
N_EXPERTS = 32
D_EXPERT = 512
EPS = 1e-6
SSD_CHUNK = 64
PAST_LEN = 1024

LANES = 128
SUBLANES = 8
GATE_TILE = 256
ROW_TILE = 256
ROUTER_E0 = 32
VMEM_LIMIT = 52 * 1024 * 1024

_NT = (((1,), (1,)), ((), ()))
_TN = (((0,), (0,)), ((), ()))


def _params(n_axes):
    return pltpu.CompilerParams(dimension_semantics=("arbitrary",) * n_axes,
                                vmem_limit_bytes=VMEM_LIMIT)


def _rmsnorm(x, g):
    return x * lax.rsqrt(jnp.mean(x * x, axis=-1, keepdims=True) + EPS) * g


def _full(shape):
    n = len(shape)
    return pl.BlockSpec(shape, lambda *_: (0,) * n)


def _project(xb, w_ref, lo, hi):
    return jnp.dot(xb, w_ref[:, lo:hi], preferred_element_type=F32)


def _conv_block(xpad, cw_ref, cb_ref, Tt, sl):
    cw = cw_ref[:, sl]
    full = xpad[:, sl]
    y = cb_ref[:, sl]
    for k in range(CONV_W):
        shift = CONV_W - 1 - k
        xk = pltpu.roll(full, shift, 0) if shift else full
        y = y + xk[SUBLANES:SUBLANES + Tt, :] * cw[k:k + 1, :]
    return y


def _carry_history(t, xpad, c0_ref, Tt):
    @pl.when(t == 0)
    def _():
        xpad[0:SUBLANES, :] = c0_ref[0]

    @pl.when(t > 0)
    def _():
        xpad[0:SUBLANES, :] = xpad[Tt:Tt + SUBLANES, :]


def _lru_body(x_ref, gm_ref, w_ref, c0_ref, h0_ref, cw_ref, cb_ref, wbd_ref, ba_ref, bx_ref, lam_ref,
              y_ref, hN_ref, cN_ref, xpad, a_s, u_s, gel_s, hcar, *, Tt, start_pos):
    t = pl.program_id(1)
    _carry_history(t, xpad, c0_ref, Tt)

    @pl.when(t == 0)
    def _():
        hcar[...] = jnp.broadcast_to(h0_ref[0], (SUBLANES, D_LRU))

    xb = _rmsnorm(x_ref[0], gm_ref[...]).astype(BF16)
    sp = jax.nn.softplus(-lam_ref[...])
    pos0 = (lax.broadcasted_iota(I32, (Tt, 1), 0) + t * Tt + start_pos) == 0
    for j in range(D_LRU // GATE_TILE):
        sl = slice(GATE_TILE * j, GATE_TILE * (j + 1))
        xpad[SUBLANES:SUBLANES + Tt, sl] = _project(xb, w_ref, sl.start, sl.stop)
        xc = _conv_block(xpad, cw_ref, cb_ref, Tt, sl)
        ga = jnp.dot(xc.astype(BF16), wbd_ref[j], preferred_element_type=F32)
        r = jax.nn.sigmoid(ga[:, :GATE_TILE] + ba_ref[:, sl])
        i = jax.nn.sigmoid(ga[:, GATE_TILE:] + bx_ref[:, sl])
        a = jnp.exp((-LRU_C * r) * sp[:, sl])
        mult = jnp.where(pos0, 1.0, jnp.sqrt(1.0 - a * a))
        a_s[:, sl] = a
        u_s[:, sl] = mult * i * xc
        g = _project(xb, w_ref, D_LRU + sl.start, D_LRU + sl.stop)
        gel_s[:, sl] = jax.nn.gelu(g, approximate=True)

    rows = lax.broadcasted_iota(I32, (SUBLANES, D_LRU), 0)

    def scan8(gi, hprev):
        r0 = pl.multiple_of(gi * SUBLANES, SUBLANES)
        a8 = a_s[pl.ds(r0, SUBLANES), :]
        u8 = u_s[pl.ds(r0, SUBLANES), :]
        for s in (1, 2, 4):
            ok = rows >= s
            u_sh = pltpu.roll(u8, s, 0)
            a_sh = pltpu.roll(a8, s, 0)
            u8 = jnp.where(ok, u8 + a8 * u_sh, u8)
            a8 = jnp.where(ok, a8 * a_sh, a8)
        h8 = u8 + a8 * hprev
        u_s[pl.ds(r0, SUBLANES), :] = h8
        return jnp.broadcast_to(h8[SUBLANES - 1:SUBLANES, :], (SUBLANES, D_LRU))

    hlast = lax.fori_loop(0, Tt // SUBLANES, scan8, hcar[...])
    hcar[...] = hlast
    y_ref[0] = (u_s[...] * gel_s[...]).astype(BF16)
    hN_ref[0] = hlast[0:1, :]
    cN_ref[0] = xpad[Tt:Tt + SUBLANES, :]


def _lru(x, g_mix, w_lru, c0, h0, cw, cb, wbd, ba, bx, lam, Tt, start_pos):
    B, L, _ = x.shape
    seq = pl.BlockSpec((1, Tt, D_LRU), lambda b, t: (b, t, 0))
    per_b = lambda r: pl.BlockSpec((1, r, D_LRU), lambda b, t: (b, 0, 0))
    return pl.pallas_call(
        functools.partial(_lru_body, Tt=Tt, start_pos=start_pos),
        grid=(B, L // Tt),
        in_specs=[pl.BlockSpec((1, Tt, D_MODEL), lambda b, t: (b, t, 0)), _full((1, D_MODEL)), _full(w_lru.shape),
                  per_b(SUBLANES), per_b(1), _full((CONV_W, D_LRU)), _full((1, D_LRU)),
                  _full(wbd.shape), _full((1, D_LRU)), _full((1, D_LRU)), _full((1, D_LRU))],
        out_specs=[seq, per_b(1), per_b(SUBLANES)],
        out_shape=[jax.ShapeDtypeStruct((B, L, D_LRU), BF16),
                   jax.ShapeDtypeStruct((B, 1, D_LRU), F32),
                   jax.ShapeDtypeStruct((B, SUBLANES, D_LRU), F32)],
        scratch_shapes=[pltpu.VMEM((Tt + SUBLANES, D_LRU), F32), pltpu.VMEM((Tt, D_LRU), F32),
                        pltpu.VMEM((Tt, D_LRU), F32), pltpu.VMEM((Tt, D_LRU), F32),
                        pltpu.VMEM((SUBLANES, D_LRU), F32)],
        compiler_params=_params(2),
        name="lru",
    )(x, g_mix, w_lru, c0, h0, cw, cb, wbd, ba, bx, lam)


def _split3(v):
    hi = v.astype(BF16)
    r1 = v - hi.astype(F32)
    mid = r1.astype(BF16)
    lo = (r1 - mid.astype(F32)).astype(BF16)
    return hi, mid, lo


def _pad_time(v, rows):
    if v.shape[0] == rows:
        return v
    return jnp.concatenate([v, jnp.zeros((rows - v.shape[0], v.shape[1]), v.dtype)], axis=0)


def _ssd_masks(q):
    P = SSD_HEADDIM
    tri = jnp.arange(q)[:, None] >= jnp.arange(q)[None, :]
    expand = jnp.arange(LANES)[:, None] == jnp.arange(D_SSM)[None, :] // P
    row_q = jnp.arange(q)[:, None]
    lane_k = jnp.arange(D_SSM)[None, :] % P
    diag = row_q == lane_k
    causal = row_q >= lane_k
    bd = jnp.arange(GATE_TILE)[:, None] // P == jnp.arange(GATE_TILE)[None, :] // P
    return (tri.astype(BF16), expand.astype(BF16), diag.astype(F32), causal.astype(F32), bd.astype(BF16))


def _ssd_body(x_ref, gm_ref, w_ref, c0_ref, s0_ref, cw_ref, cb_ref, dtb_ref, alog_ref, dvec_ref, ng_ref,
              tri_ref, expand_ref, diag_ref, causal_ref, bd_ref,
              y_ref, sN_ref, cN_ref, xpad, xa_s, y_s, st_s, dt_s, zs_s, *, Tt, q):
    t = pl.program_id(1)
    _carry_history(t, xpad, c0_ref, Tt)

    @pl.when(t == 0)
    def _():
        st_s[...] = s0_ref[0].reshape(D_SSM, D_STATE).T

    xb = _rmsnorm(x_ref[0], gm_ref[...]).astype(BF16)
    for j in range(D_XBC // GATE_TILE):
        sl = slice(GATE_TILE * j, GATE_TILE * (j + 1))
        xpad[SUBLANES:SUBLANES + Tt, sl] = _project(xb, w_ref, D_SSM + sl.start, D_SSM + sl.stop)
        xa_s[:, sl] = jax.nn.silu(_conv_block(xpad, cw_ref, cb_ref, Tt, sl))
    for j in range(D_SSM // GATE_TILE):
        sl = slice(GATE_TILE * j, GATE_TILE * (j + 1))
        zs_s[:, sl] = jax.nn.silu(_project(xb, w_ref, sl.start, sl.stop))
    dt_s[...] = _project(xb, w_ref, D_SSM + D_XBC, D_SSM + D_XBC + LANES)
    A = -jnp.exp(alog_ref[...])
    P = SSD_HEADDIM
    blk = GATE_TILE // P
    off_b = D_SSM
    off_c = D_SSM + SSD_GROUPS * D_STATE
    gw = D_SSM // SSD_GROUPS

    def exact01(parts, w01, left):
        one = (lambda p: jnp.dot(w01, p, preferred_element_type=F32)) if left else (
            lambda p: jnp.dot(p, w01, preferred_element_type=F32))
        hi, mid, lo = parts
        return (one(lo) + one(mid)) + one(hi)

    def chunk(c, carry):
        r0 = pl.multiple_of(c * q, q)
        xs = xa_s[pl.ds(r0, q), 0:D_SSM]
        xsb = xs.astype(BF16)
        dt = jax.nn.softplus(dt_s[pl.ds(r0, q), :] + dtb_ref[...])
        cs = exact01(_split3(dt * A), tri_ref[...], left=True)
        E = exact01(_split3(jnp.concatenate([cs, dt], axis=0)), expand_ref[...], left=False)
        diag = diag_ref[...] != 0.0
        causal = causal_ref[...] != 0.0
        E_cs = E[0:q]
        E_dt = E[q:2 * q]
        r_cs = jnp.sum(jnp.where(diag, E_cs, 0.0), axis=0, keepdims=True)
        r_dt = jnp.sum(jnp.where(diag, E_dt, 0.0), axis=0, keepdims=True)
        Lm = jnp.where(causal, jnp.exp(jnp.where(causal, E_cs - r_cs, 0.0)), 0.0)
        Bs, Cs, CBs = [], [], []
        for g in range(SSD_GROUPS):
            Bg = xa_s[pl.ds(r0, q), off_b + g * D_STATE:off_b + (g + 1) * D_STATE].astype(BF16)
            Cg = xa_s[pl.ds(r0, q), off_c + g * D_STATE:off_c + (g + 1) * D_STATE].astype(BF16)
            Bt = jnp.concatenate([_pad_time(Bg, P)] * SSD_HPG, axis=0)
            CBs.append(lax.dot_general(Cg, Bt, _NT, preferred_element_type=F32))
            Bs.append(Bg)
            Cs.append(Cg)
        Mw = (jnp.concatenate(CBs, axis=1) * Lm * r_dt).astype(BF16)
        yd = []
        for j in range(D_SSM // GATE_TILE):
            sl = slice(j * GATE_TILE, (j + 1) * GATE_TILE)
            slab = _pad_time(xsb[:, sl], P)
            rhs = jnp.concatenate([slab] * blk, axis=0) * bd_ref[...]
            yd.append(jnp.dot(Mw[:, sl], rhs, preferred_element_type=F32))
        cs_last = E_cs[q - 1:q, :]
        xw = (jnp.exp(cs_last - E_cs) * E_dt * xs).astype(BF16)
        cdec = jnp.exp(cs_last)
        yo = []
        for g in range(SSD_GROUPS):
            sl = slice(g * gw, (g + 1) * gw)
            S = st_s[:, sl]
            yo.append(jnp.dot(Cs[g], S.astype(BF16), preferred_element_type=F32))
            st = lax.dot_general(Bs[g], xw[:, sl], _TN, preferred_element_type=F32)
            st_s[:, sl] = cdec[:, sl] * S + st
        y_s[pl.ds(r0, q), :] = jnp.concatenate(yd, axis=1) + jnp.concatenate(yo, axis=1) * jnp.exp(E_cs)
        return carry

    lax.fori_loop(0, Tt // q, chunk, 0)

    @pl.when(t == pl.num_programs(1) - 1)
    def _():
        sN_ref[0] = st_s[...].T.reshape(SSD_HEADS, SSD_HEADDIM, D_STATE)

    y = y_s[...] + dvec_ref[...] * xa_s[:, 0:D_SSM]
    y = y * zs_s[...]
    gw = D_SSM // SSD_GROUPS
    for g in range(SSD_GROUPS):
        sl = slice(g * gw, (g + 1) * gw)
        yg = y[:, sl]
        yg = yg * lax.rsqrt(jnp.mean(yg * yg, axis=-1, keepdims=True) + EPS)
        y_ref[0, :, sl] = (yg * ng_ref[:, sl]).astype(BF16)
    cN_ref[0] = xpad[Tt:Tt + SUBLANES, :]


def _ssd(x, g_mix, w_ssd, c0, s0, cw, cb, dtb, alog, dvec, ng, Tt, q):
    B, L, _ = x.shape
    seq = lambda w: pl.BlockSpec((1, Tt, w), lambda b, t: (b, t, 0))
    per_b = pl.BlockSpec((1, SUBLANES, D_XBC), lambda b, t: (b, 0, 0))
    st = pl.BlockSpec((1, SSD_HEADS, SSD_HEADDIM, D_STATE), lambda b, t: (b, 0, 0, 0))
    masks = _ssd_masks(q)
    return pl.pallas_call(
        functools.partial(_ssd_body, Tt=Tt, q=q),
        grid=(B, L // Tt),
        in_specs=[seq(D_MODEL), _full((1, D_MODEL)), _full(w_ssd.shape),
                  per_b, st, _full((CONV_W, D_XBC)), _full((1, D_XBC)),
                  _full((1, LANES)), _full((1, LANES)), _full((1, D_SSM)), _full((1, D_SSM))]
                 + [_full(m.shape) for m in masks],
        out_specs=[seq(D_SSM), st, per_b],
        out_shape=[jax.ShapeDtypeStruct((B, L, D_SSM), BF16),
                   jax.ShapeDtypeStruct((B, SSD_HEADS, SSD_HEADDIM, D_STATE), F32),
                   jax.ShapeDtypeStruct((B, SUBLANES, D_XBC), F32)],
        scratch_shapes=[pltpu.VMEM((Tt + SUBLANES, D_XBC), F32), pltpu.VMEM((Tt, D_XBC), F32),
                        pltpu.VMEM((Tt, D_SSM), F32), pltpu.VMEM((D_STATE, D_SSM), F32),
                        pltpu.VMEM((Tt, LANES), F32), pltpu.VMEM((Tt, D_SSM), F32)],
        compiler_params=_params(2),
        name="ssd",
    )(x, g_mix, w_ssd, c0, s0, cw, cb, dtb, alog, dvec, ng, *masks)


def _out_router_body(x_ref, yl_ref, ys_ref, wo_ref, gf_ref, wr_ref, br_ref, cin_ref,
                     h1_ref, xn_ref, meta_ref, gate_ref, cnt_ref, carry, *, tm):
    step = pl.program_id(0)

    @pl.when(step == 0)
    def _():
        carry[...] = cin_ref[...]

    mix = jnp.concatenate([yl_ref[...], ys_ref[...]], axis=1)
    h1 = x_ref[...] + jnp.dot(mix, wo_ref[...], preferred_element_type=F32)
    h1_ref[...] = h1
    xn = _rmsnorm(h1, gf_ref[...])
    xn_ref[...] = xn

    xh = xn.astype(BF16)
    xm = (xn - xh.astype(F32)).astype(BF16)
    small = (jnp.dot(xh, wr_ref[1], preferred_element_type=F32)
             + jnp.dot(xm, wr_ref[0], preferred_element_type=F32))
    logits = small + jnp.dot(xh, wr_ref[0], preferred_element_type=F32) + br_ref[...]
    lt = logits.T
    rows8 = lax.broadcasted_iota(I32, (SUBLANES, tm), 0)
    lg = jnp.where(rows8 < MOE_GROUPS, lt[0:SUBLANES, :], -jnp.inf)
    eg = jnp.exp(lg - jnp.max(lg, axis=0, keepdims=True))
    pg = eg / jnp.sum(eg, axis=0, keepdims=True)
    pgs = jnp.max(pg, axis=0, keepdims=True)
    rows8f = rows8.astype(F32)
    gsel = jnp.min(jnp.where(pg == pgs, rows8f, float(SUBLANES)), axis=0, keepdims=True)

    rows32 = lax.broadcasted_iota(I32, (N_EXPERTS, tm), 0)
    rows32f = rows32.astype(F32)
    grp = (rows32 // EXPERTS_PER_GROUP).astype(F32)
    le = lt[ROUTER_E0:ROUTER_E0 + N_EXPERTS, :]
    ing = grp == gsel
    lem = jnp.where(ing, le, -jnp.inf)
    ee = jnp.exp(lem - jnp.max(lem, axis=0, keepdims=True))
    pe = ee / jnp.sum(ee, axis=0, keepdims=True)
    pe1 = jnp.where(ing, pe, -1.0)
    v1 = jnp.max(pe1, axis=0, keepdims=True)
    i1 = jnp.min(jnp.where(pe1 == v1, rows32f, float(N_EXPERTS)), axis=0, keepdims=True)
    pe2 = jnp.where(rows32f == i1, -1.0, pe1)
    v2 = jnp.max(pe2, axis=0, keepdims=True)
    i2 = jnp.min(jnp.where(pe2 == v2, rows32f, float(N_EXPERTS)), axis=0, keepdims=True)
    sv = v1 + v2
    w1 = v1 / sv * pgs
    w2 = v2 / sv * pgs

    oh1 = rows32f == i1
    oh2 = rows32f == i2
    oh = jnp.where(oh1 | oh2, 1.0, 0.0)
    before = (lax.broadcasted_iota(I32, (tm, tm), 0) < lax.broadcasted_iota(I32, (tm, tm), 1))
    pref = jnp.dot(oh.astype(BF16), jnp.where(before, 1.0, 0.0).astype(BF16), preferred_element_type=F32)
    pref = pref + carry[:, 0:1]
    r1 = jnp.sum(jnp.where(oh1, pref, 0.0), axis=0, keepdims=True)
    r2 = jnp.sum(jnp.where(oh2, pref, 0.0), axis=0, keepdims=True)
    carry[...] = carry[...] + jnp.sum(oh, axis=1, keepdims=True)
    cnt_ref[...] = carry[...]

    meta = jnp.where(rows8 == 0, i1, jnp.where(rows8 == 1, i2, jnp.where(rows8 == 2, r1, jnp.where(rows8 == 3, r2, 0.0))))
    meta_ref[...] = meta.astype(I32)
    gate_ref[...] = jnp.where(rows8 == 0, w1, jnp.where(rows8 == 1, w2, 0.0))


def _out_router(x2d, y_lru, y_ssd, w_out, gf, wr, br, counts_in, tm):
    T = x2d.shape[0]
    row = lambda w: pl.BlockSpec((tm, w), lambda i: (i, 0))
    col = pl.BlockSpec((SUBLANES, tm), lambda i: (0, i))
    return pl.pallas_call(
        functools.partial(_out_router_body, tm=tm),
        grid=(T // tm,),
        in_specs=[row(D_MODEL), row(D_LRU), row(D_SSM), _full((D_LRU + D_SSM, D_MODEL)), _full((1, D_MODEL)),
                  _full((2, D_MODEL, LANES)), _full((1, LANES)), _full((N_EXPERTS, LANES))],
        out_specs=[row(D_MODEL), row(D_MODEL), col, col, _full((N_EXPERTS, LANES))],
        out_shape=[jax.ShapeDtypeStruct((T, D_MODEL), F32), jax.ShapeDtypeStruct((T, D_MODEL), F32),
                   jax.ShapeDtypeStruct((SUBLANES, T), I32), jax.ShapeDtypeStruct((SUBLANES, T), F32),
                   jax.ShapeDtypeStruct((N_EXPERTS, LANES), F32)],
        scratch_shapes=[pltpu.VMEM((N_EXPERTS, LANES), F32)],
        compiler_params=_params(1),
        name="out_router",
    )(x2d, y_lru, y_ssd, w_out, gf, wr, br, counts_in)


def _row_copy(src_hbm, src_row, dst, dst_row, sem):
    return pltpu.make_async_copy(src_hbm.at[pl.ds(src_row, 1)], dst.at[pl.ds(dst_row, 1)], sem)


def _positions_body(offs_ref, meta_ref, pos_ref):
    m = meta_ref[...]
    base = jnp.zeros_like(m)
    for e in range(N_EXPERTS):
        base = jnp.where(m == e, offs_ref[e], base)
    pos_ref[...] = base + pltpu.roll(m, SUBLANES - 2, 0)


def _positions(offs, meta):
    T = meta.shape[1]
    tb = min(T, 4096)
    blk = pl.BlockSpec((SUBLANES, tb), lambda i, offs: (0, i))
    return pl.pallas_call(
        _positions_body,
        grid_spec=pltpu.PrefetchScalarGridSpec(num_scalar_prefetch=1, grid=(T // tb,), in_specs=[blk], out_specs=blk),
        out_shape=jax.ShapeDtypeStruct((SUBLANES, T), I32),
        compiler_params=_params(1),
        name="positions",
    )(offs, meta)


def _wait_rows(src, dst, sem):
    pltpu.make_async_copy(src, dst, sem).wait()


def _dispatch_body(zf_ref, pos_ref, *rest, tm, tme, steps):
    n_groups = len(steps)
    xn_refs = rest[:n_groups]
    xs_hbm, zbuf, sem, zsem = rest[n_groups:]
    n_tiles = xs_hbm.shape[0] // tme
    step = pl.program_id(0)

    def zero_copy(i):
        return pltpu.make_async_copy(zbuf, xs_hbm.at[pl.ds(pl.multiple_of(i * tme, tme), tme)], zsem)

    @pl.when(step == 0)
    def _():
        zbuf[...] = jnp.zeros_like(zbuf)

        def start(i, c):
            @pl.when(zf_ref[i] != 0)
            def _():
                zero_copy(i).start()
            return c

        def wait(i, c):
            @pl.when(zf_ref[i] != 0)
            def _():
                zero_copy(i).wait()
            return c

        lax.fori_loop(0, n_tiles, start, 0)
        lax.fori_loop(0, n_tiles, wait, 0)

    first = 0
    for xn_ref, n in zip(xn_refs, steps):
        @pl.when(jnp.logical_and(step >= first, step < first + n))
        def _(xn_ref=xn_ref):
            for t in range(tm):
                for k in range(2):
                    _row_copy(xn_ref, t, xs_hbm, pos_ref[k, t], sem).start(priority=k)
            for k in range(2):
                _wait_rows(xn_ref, xs_hbm.at[pl.ds(0, tm)], sem)
        first += n


def _dispatch(zflag, pos, xns, n_rows, tm, tme):
    steps = [xn.shape[0] // tm for xn in xns]
    firsts = [sum(steps[:g]) for g in range(len(steps))]

    def group_spec(first, n):
        return pl.BlockSpec((tm, D_MODEL), lambda i, zf: (jnp.clip(i - first, 0, n - 1), 0))

    return pl.pallas_call(
        functools.partial(_dispatch_body, tm=tm, tme=tme, steps=tuple(steps)),
        grid_spec=pltpu.PrefetchScalarGridSpec(
            num_scalar_prefetch=1,
            grid=(sum(steps),),
            in_specs=[pl.BlockSpec((SUBLANES, tm), lambda i, zf: (0, i), memory_space=pltpu.SMEM)]
                     + [group_spec(f, n) for f, n in zip(firsts, steps)],
            out_specs=pl.BlockSpec(memory_space=pl.ANY),
            scratch_shapes=[pltpu.VMEM((tme, D_MODEL), F32), pltpu.SemaphoreType.DMA(()),
                            pltpu.SemaphoreType.DMA(())],
        ),
        out_shape=jax.ShapeDtypeStruct((n_rows, D_MODEL), F32),
        compiler_params=_params(1),
        name="dispatch",
    )(zflag, pos, *xns)


def _experts_body(te_ref, na_ref, x_ref, w1_ref, w3_ref, w2_ref, o_ref, w1b, w3b, w2b):
    i = pl.program_id(0)
    changed = jnp.logical_or(i == 0, te_ref[i] != te_ref[jnp.maximum(i - 1, 0)])

    @pl.when(changed)
    def _():
        w1b[...] = w1_ref[0].astype(BF16)
        w3b[...] = w3_ref[0].astype(BF16)
        w2b[...] = w2_ref[0].astype(BF16)

    @pl.when(i < na_ref[0])
    def _():
        xb = x_ref[...].astype(BF16)
        a = jnp.dot(xb, w1b[...], preferred_element_type=F32)
        b = jnp.dot(xb, w3b[...], preferred_element_type=F32)
        hd = (jax.nn.silu(a) * b).astype(BF16)
        o_ref[...] = jnp.dot(hd, w2b[...], preferred_element_type=F32)

    @pl.when(i >= na_ref[0])
    def _():
        o_ref[...] = jnp.zeros_like(o_ref)


def _experts(tile_e, n_active, xs, w1, w3, w2, tme):
    n_rows = xs.shape[0]
    row = pl.BlockSpec((tme, D_MODEL), lambda i, te, na: (i, 0))
    return pl.pallas_call(
        _experts_body,
        grid_spec=pltpu.PrefetchScalarGridSpec(
            num_scalar_prefetch=2,
            grid=(n_rows // tme,),
            in_specs=[row,
                      pl.BlockSpec((1, D_MODEL, D_EXPERT), lambda i, te, na: (te[i], 0, 0)),
                      pl.BlockSpec((1, D_MODEL, D_EXPERT), lambda i, te, na: (te[i], 0, 0)),
                      pl.BlockSpec((1, D_EXPERT, D_MODEL), lambda i, te, na: (te[i], 0, 0))],
            out_specs=row,
            scratch_shapes=[pltpu.VMEM((D_MODEL, D_EXPERT), BF16), pltpu.VMEM((D_MODEL, D_EXPERT), BF16),
                            pltpu.VMEM((D_EXPERT, D_MODEL), BF16)],
        ),
        out_shape=jax.ShapeDtypeStruct((n_rows, D_MODEL), F32),
        compiler_params=_params(1),
        name="experts",
    )(tile_e, n_active, xs, w1, w3, w2)


def _combine_body(pos_ref, gate_ref, h1_ref, fg_ref, ys_hbm, y_ref, ybuf, sem, *, tm):
    for t in range(tm):
        for k in range(2):
            _row_copy(ys_hbm, pos_ref[k, t], ybuf.at[k], t, sem).start(priority=k)
    for k in range(2):
        _wait_rows(ys_hbm.at[pl.ds(0, tm)], ybuf.at[k], sem)

    eye = lax.broadcasted_iota(I32, (tm, tm), 0) == lax.broadcasted_iota(I32, (tm, tm), 1)
    g1 = jnp.sum(jnp.where(eye, gate_ref[0:1, :], 0.0), axis=1, keepdims=True)
    g2 = jnp.sum(jnp.where(eye, gate_ref[1:2, :], 0.0), axis=1, keepdims=True)
    h2 = h1_ref[...] + (g1 * ybuf[0] + g2 * ybuf[1])
    y_ref[...] = _rmsnorm(h2, fg_ref[...])


def _combine(pos, gates, h1, fg, ys, tm):
    T = h1.shape[0]
    return pl.pallas_call(
        functools.partial(_combine_body, tm=tm),
        grid=(T // tm,),
        in_specs=[pl.BlockSpec((SUBLANES, tm), lambda i: (0, i), memory_space=pltpu.SMEM),
                  pl.BlockSpec((SUBLANES, tm), lambda i: (0, i)),
                  pl.BlockSpec((tm, D_MODEL), lambda i: (i, 0)),
                  pl.BlockSpec((1, D_MODEL), lambda i: (0, 0)),
                  pl.BlockSpec(memory_space=pl.ANY)],
        out_specs=pl.BlockSpec((tm, D_MODEL), lambda i: (i, 0)),
        scratch_shapes=[pltpu.VMEM((2, tm, D_MODEL), F32), pltpu.SemaphoreType.DMA(())],
        out_shape=jax.ShapeDtypeStruct((T, D_MODEL), F32),
        compiler_params=_params(1),
        name="combine",
    )(pos, gates, h1, fg, ys)


def _blockdiag(w):
    per = GATE_TILE // LRU_BLOCK
    w4 = w.reshape(LRU_BLOCKS // per, per, LRU_BLOCK, LRU_BLOCK)
    eye = jnp.eye(per, dtype=w.dtype)
    return jnp.einsum('jbio,bc->jbico', w4, eye).reshape(LRU_BLOCKS // per, GATE_TILE, GATE_TILE)


def _pad_rows(c):
    return jnp.pad(c, ((0, 0), (SUBLANES - (CONV_W - 1), 0), (0, 0)))


def _router_cols(group_part, expert_part):
    r = group_part.shape[0]
    out = jnp.zeros((r, LANES), F32)
    out = out.at[:, 0:MOE_GROUPS].set(group_part)
    return out.at[:, ROUTER_E0:ROUTER_E0 + N_EXPERTS].set(expert_part)


def _lane_row(v, width=LANES):
    return jnp.pad(v, (0, width - v.shape[0])).reshape(1, width)


def _prep(norm_mix_g, w_in, lru_conv_w, lru_conv_b, lru_wa, lru_ba, lru_wx, lru_bx, lru_lambda,
          ssd_conv_w, ssd_conv_b, ssd_dt_bias, ssd_a_log, ssd_d, ssd_norm_g, w_out,
          norm_ffn_g, router_group_w, router_group_b, router_expert_w, router_expert_b,
          moe_w1, moe_w3, moe_w2, final_norm_g):
    w = w_in[0]
    wr = _router_cols(router_group_w[0], router_expert_w[0])
    wr_hi = wr.astype(BF16)
    P = dict(
        g_mix=norm_mix_g[0].reshape(1, D_MODEL),
        w_lru=w[:, :2 * D_LRU].astype(BF16),
        w_ssd=jnp.pad(w[:, 2 * D_LRU:], ((0, 0), (0, LANES - SSD_HEADS))).astype(BF16),
        lru_cw=lru_conv_w[0], lru_cb=lru_conv_b[0].reshape(1, D_LRU),
        wbd=jnp.concatenate([_blockdiag(lru_wa[0]), _blockdiag(lru_wx[0])], axis=2).astype(BF16),
        ba=lru_ba[0].reshape(1, D_LRU), bx=lru_bx[0].reshape(1, D_LRU), lam=lru_lambda[0].reshape(1, D_LRU),
        ssd_cw=ssd_conv_w[0], ssd_cb=ssd_conv_b[0].reshape(1, D_XBC),
        dtb=_lane_row(ssd_dt_bias[0]), alog=_lane_row(ssd_a_log[0]),
        dvec=jnp.repeat(ssd_d[0], SSD_HEADDIM).reshape(1, D_SSM),
        ng=ssd_norm_g[0].reshape(1, D_SSM),
        w_out=w_out[0].astype(BF16),
        g_ffn=norm_ffn_g[0].reshape(1, D_MODEL),
        wr=jnp.stack([wr_hi, (wr - wr_hi.astype(F32)).astype(BF16)]),
        br=_router_cols(router_group_b[0][None], router_expert_b[0][None]),
        w1=moe_w1[0], w3=moe_w3[0], w2=moe_w2[0],
        g_final=final_norm_g.reshape(1, D_MODEL),
    )
    return P


def _expert_layout(counts, n_pairs, tme):
    cnt = counts[:, 0].astype(I32)
    padded = ((cnt + tme - 1) // tme) * tme
    ends = jnp.cumsum(padded)
    offs = ends - padded
    n_tiles = n_pairs // tme + N_EXPERTS
    n_active = ends[-1] // tme
    tiles = jnp.arange(n_tiles, dtype=I32)
    tile_e = jnp.sum((tiles * tme)[:, None] >= ends[None, :], axis=1).astype(I32)
    last_e = jnp.sum((n_active - 1) * tme >= ends).astype(I32)
    tile_e = jnp.where(tiles < n_active, tile_e, last_e)
    is_last = jnp.any((tiles[:, None] + 1) * tme == ends[None, :], axis=1)
    zflag = jnp.logical_or(is_last, tiles >= n_active).astype(I32)
    return offs.astype(I32), tile_e, n_active.reshape(1).astype(I32), zflag, n_tiles * tme


def _mixer_router(x, lru_h0, lru_c0, ssd_h0, ssd_c0, P, start_pos, counts_in):
    B, L, _ = x.shape
    T = B * L
    Tt = min(ROW_TILE, L)
    q = min(SSD_CHUNK, L)
    tm = min(ROW_TILE, T)
    x2d = x.reshape(T, D_MODEL)

    y_lru, lru_h, lru_c = _lru(x, P['g_mix'], P['w_lru'], _pad_rows(lru_c0), lru_h0.reshape(B, 1, D_LRU),
                               P['lru_cw'], P['lru_cb'], P['wbd'], P['ba'], P['bx'], P['lam'], Tt, start_pos)
    y_ssd, ssd_h, ssd_c = _ssd(x, P['g_mix'], P['w_ssd'], _pad_rows(ssd_c0), ssd_h0,
                               P['ssd_cw'], P['ssd_cb'], P['dtb'], P['alog'], P['dvec'], P['ng'], Tt, q)
    h1, xn, meta, gates, counts = _out_router(
        x2d, y_lru.reshape(T, D_LRU), y_ssd.reshape(T, D_SSM), P['w_out'], P['g_ffn'],
        P['wr'], P['br'], counts_in, tm)

    hist = SUBLANES - (CONV_W - 1)
    states = (lru_h.reshape(1, B, D_LRU), lru_c[:, hist:][None], ssd_h[None], ssd_c[:, hist:][None])
    return dict(h1=h1, xn=xn, meta=meta, gates=gates, shape=(B, L, D_MODEL), tm=tm), counts, states


def _moe_final(groups, counts, P):
    tme = ROW_TILE
    n_pairs = 2 * sum(g['h1'].shape[0] for g in groups)
    offs, tile_e, n_active, zflag, n_rows = _expert_layout(counts, n_pairs, tme)
    tm = groups[0]['tm']
    assert all(g['tm'] == tm for g in groups)
    for g in groups:
        g['pos'] = _positions(offs, g['meta'])
    pos_all = jnp.concatenate([g['pos'] for g in groups], axis=1)
    xs = _dispatch(zflag, pos_all, [g['xn'] for g in groups], n_rows, tm, tme)
    ys = _experts(tile_e, n_active, xs, P['w1'], P['w3'], P['w2'], tme)
    return [_combine(g['pos'], g['gates'], g['h1'], P['g_final'], ys, g['tm']).reshape(g['shape'])
            for g in groups]


def kernel(x_prompt, x_sample, state_lru_h, state_lru_conv, state_ssd, state_ssd_conv, norm_mix_g, w_in, lru_conv_w, lru_conv_b, lru_wa, lru_ba, lru_wx, lru_bx, lru_lambda, ssd_conv_w, ssd_conv_b, ssd_dt_bias, ssd_a_log, ssd_d, ssd_norm_g, w_out, norm_ffn_g, router_group_w, router_group_b, router_expert_w, router_expert_b, moe_w1, moe_w3, moe_w2, final_norm_g):
    P = _prep(norm_mix_g, w_in, lru_conv_w, lru_conv_b, lru_wa, lru_ba, lru_wx, lru_bx, lru_lambda,
              ssd_conv_w, ssd_conv_b, ssd_dt_bias, ssd_a_log, ssd_d, ssd_norm_g, w_out,
              norm_ffn_g, router_group_w, router_group_b, router_expert_w, router_expert_b,
              moe_w1, moe_w3, moe_w2, final_norm_g)
    bp = x_prompt.shape[0]
    gp, counts, (a1, a2, a3, a4) = _mixer_router(
        x_prompt,
        jnp.zeros((bp, D_LRU), F32), jnp.zeros((bp, CONV_W - 1, D_LRU), F32),
        jnp.zeros((bp, SSD_HEADS, SSD_HEADDIM, D_STATE), F32), jnp.zeros((bp, CONV_W - 1, D_XBC), F32),
        P, 0, jnp.zeros((N_EXPERTS, LANES), F32))
    gs, counts, (b1, b2, b3, b4) = _mixer_router(
        x_sample, state_lru_h[0], state_lru_conv[0], state_ssd[0], state_ssd_conv[0], P, PAST_LEN, counts)
    yp, ys = _moe_final([gp, gs], counts, P)
    return (yp, ys, a1, a2, a3, a4, b1, b2, b3, b4)
```

```python
import functools

import jax
import jax.numpy as jnp
from jax import lax
from jax.experimental import pallas as pl
from jax.experimental.pallas import tpu as pltpu

F32 = jnp.float32
BF16 = jnp.bfloat16
I32 = jnp.int32

D_MODEL = 1024
D_LRU = 1024
LRU_BLOCKS = 16
LRU_BLOCK = 64
LRU_C = 8.0
CONV_W = 4
D_SSM = 1024
SSD_HEADDIM = 64
SSD_HEADS = 16
SSD_GROUPS = 2
SSD_HPG = 8
D_STATE = 128
D_XBC = 1536
MOE_GROUPS = 4
EXPERTS_PER_GROUP = 8
N_EXPERTS = 32
D_EXPERT = 512
EPS = 1e-6
SSD_CHUNK = 64
PAST_LEN = 1024

LANES = 128
SUBLANES = 8
GATE_TILE = 256
ROW_TILE = 256
MIX_TILE = 512
EXPERT_TILE = 512
ROUTER_E0 = 32
VMEM_LIMIT = 52 * 1024 * 1024

_NT = (((1,), (1,)), ((), ()))
_TN = (((0,), (0,)), ((), ()))


def _params(n_axes):
    return pltpu.CompilerParams(dimension_semantics=("arbitrary",) * n_axes,
                                vmem_limit_bytes=VMEM_LIMIT)


def _rmsnorm(x, g):
    return x * lax.rsqrt(jnp.mean(x * x, axis=-1, keepdims=True) + EPS) * g


def _full(shape):
    n = len(shape)
    return pl.BlockSpec(shape, lambda *_: (0,) * n)


def _project(xb, w_ref, lo, hi):
    return jnp.dot(xb, w_ref[:, lo:hi], preferred_element_type=F32)


def _conv_block(xpad, cw_ref, cb_ref, Tt, sl):
    cw = cw_ref[:, sl]
    full = xpad[:, sl]
    y = cb_ref[:, sl]
    for k in range(CONV_W):
        shift = CONV_W - 1 - k
        xk = pltpu.roll(full, shift, 0) if shift else full
        y = y + xk[SUBLANES:SUBLANES + Tt, :] * cw[k:k + 1, :]
    return y


def _carry_history(t, xpad, c0_ref, Tt):
    @pl.when(t == 0)
    def _():
        xpad[0:SUBLANES, :] = c0_ref[0]

    @pl.when(t > 0)
    def _():
        xpad[0:SUBLANES, :] = xpad[Tt:Tt + SUBLANES, :]


def _lru_body(x_ref, gm_ref, w_ref, c0_ref, h0_ref, cw_ref, cb_ref, wbd_ref, ba_ref, bx_ref, lam_ref,
              y_ref, hN_ref, cN_ref, xpad, a_s, u_s, gel_s, hcar, *, Tt, start_pos):
    t = pl.program_id(1)
    _carry_history(t, xpad, c0_ref, Tt)

    @pl.when(t == 0)
    def _():
        hcar[...] = jnp.broadcast_to(h0_ref[0], (SUBLANES, D_LRU))

    xb = _rmsnorm(x_ref[0], gm_ref[...]).astype(BF16)
    sp = jax.nn.softplus(-lam_ref[...])
    pos0 = (lax.broadcasted_iota(I32, (Tt, 1), 0) + t * Tt + start_pos) == 0
    for j in range(D_LRU // GATE_TILE):
        sl = slice(GATE_TILE * j, GATE_TILE * (j + 1))
        xpad[SUBLANES:SUBLANES + Tt, sl] = _project(xb, w_ref, sl.start, sl.stop)
        xc = _conv_block(xpad, cw_ref, cb_ref, Tt, sl)
        ga = jnp.dot(xc.astype(BF16), wbd_ref[j], preferred_element_type=F32)
        r = jax.nn.sigmoid(ga[:, :GATE_TILE] + ba_ref[:, sl])
        i = jax.nn.sigmoid(ga[:, GATE_TILE:] + bx_ref[:, sl])
        a = jnp.exp((-LRU_C * r) * sp[:, sl])
        mult = jnp.where(pos0, 1.0, jnp.sqrt(1.0 - a * a))
        a_s[:, sl] = a
        u_s[:, sl] = mult * i * xc
        g = _project(xb, w_ref, D_LRU + sl.start, D_LRU + sl.stop)
        gel_s[:, sl] = jax.nn.gelu(g, approximate=True)

    rows = lax.broadcasted_iota(I32, (SUBLANES, D_LRU), 0)

    def scan8(gi, hprev):
        r0 = pl.multiple_of(gi * SUBLANES, SUBLANES)
        a8 = a_s[pl.ds(r0, SUBLANES), :]
        u8 = u_s[pl.ds(r0, SUBLANES), :]
        for s in (1, 2, 4):
            ok = rows >= s
            u_sh = pltpu.roll(u8, s, 0)
            a_sh = pltpu.roll(a8, s, 0)
            u8 = jnp.where(ok, u8 + a8 * u_sh, u8)
            a8 = jnp.where(ok, a8 * a_sh, a8)
        h8 = u8 + a8 * hprev
        u_s[pl.ds(r0, SUBLANES), :] = h8
        return jnp.broadcast_to(h8[SUBLANES - 1:SUBLANES, :], (SUBLANES, D_LRU))

    hlast = lax.fori_loop(0, Tt // SUBLANES, scan8, hcar[...])
    hcar[...] = hlast
    y_ref[0] = (u_s[...] * gel_s[...]).astype(BF16)
    hN_ref[0] = hlast[0:1, :]
    cN_ref[0] = xpad[Tt:Tt + SUBLANES, :]


def _lru(x, g_mix, w_lru, c0, h0, cw, cb, wbd, ba, bx, lam, Tt, start_pos):
    B, L, _ = x.shape
    seq = pl.BlockSpec((1, Tt, D_LRU), lambda b, t: (b, t, 0))
    per_b = lambda r: pl.BlockSpec((1, r, D_LRU), lambda b, t: (b, 0, 0))
    return pl.pallas_call(
        functools.partial(_lru_body, Tt=Tt, start_pos=start_pos),
        grid=(B, L // Tt),
        in_specs=[pl.BlockSpec((1, Tt, D_MODEL), lambda b, t: (b, t, 0)), _full((1, D_MODEL)), _full(w_lru.shape),
                  per_b(SUBLANES), per_b(1), _full((CONV_W, D_LRU)), _full((1, D_LRU)),
                  _full(wbd.shape), _full((1, D_LRU)), _full((1, D_LRU)), _full((1, D_LRU))],
        out_specs=[seq, per_b(1), per_b(SUBLANES)],
        out_shape=[jax.ShapeDtypeStruct((B, L, D_LRU), BF16),
                   jax.ShapeDtypeStruct((B, 1, D_LRU), F32),
                   jax.ShapeDtypeStruct((B, SUBLANES, D_LRU), F32)],
        scratch_shapes=[pltpu.VMEM((Tt + SUBLANES, D_LRU), F32), pltpu.VMEM((Tt, D_LRU), F32),
                        pltpu.VMEM((Tt, D_LRU), F32), pltpu.VMEM((Tt, D_LRU), F32),
                        pltpu.VMEM((SUBLANES, D_LRU), F32)],
        compiler_params=_params(2),
        name="lru",
    )(x, g_mix, w_lru, c0, h0, cw, cb, wbd, ba, bx, lam)


def _split3(v):
    hi = v.astype(BF16)
    r1 = v - hi.astype(F32)
    mid = r1.astype(BF16)
    lo = (r1 - mid.astype(F32)).astype(BF16)
    return hi, mid, lo


def _pad_time(v, rows):
    if v.shape[0] == rows:
        return v
    return jnp.concatenate([v, jnp.zeros((rows - v.shape[0], v.shape[1]), v.dtype)], axis=0)


def _ssd_masks(q):
    P = SSD_HEADDIM
    tri = jnp.arange(q)[:, None] >= jnp.arange(q)[None, :]
    expand = jnp.arange(LANES)[:, None] == jnp.arange(D_SSM)[None, :] // P
    row_q = jnp.arange(q)[:, None]
    lane_k = jnp.arange(D_SSM)[None, :] % P
    diag = row_q == lane_k
    causal = row_q >= lane_k
    bd = jnp.arange(GATE_TILE)[:, None] // P == jnp.arange(GATE_TILE)[None, :] // P
    return (tri.astype(BF16), expand.astype(BF16), diag.astype(F32), causal.astype(F32), bd.astype(BF16))


def _ssd_body(x_ref, gm_ref, w_ref, c0_ref, s0_ref, cw_ref, cb_ref, dtb_ref, alog_ref, dvec_ref, ng_ref,
              tri_ref, expand_ref, diag_ref, causal_ref, bd_ref,
              y_ref, sN_ref, cN_ref, xpad, xa_s, y_s, st_s, dt_s, zs_s, stn_s, ecs_s, cdec_s, *, Tt, q):
    t = pl.program_id(1)
    _carry_history(t, xpad, c0_ref, Tt)

    @pl.when(t == 0)
    def _():
        st_s[...] = s0_ref[0].reshape(D_SSM, D_STATE).T

    xb = _rmsnorm(x_ref[0], gm_ref[...]).astype(BF16)
    for j in range(D_XBC // GATE_TILE):
        sl = slice(GATE_TILE * j, GATE_TILE * (j + 1))
        xpad[SUBLANES:SUBLANES + Tt, sl] = _project(xb, w_ref, D_SSM + sl.start, D_SSM + sl.stop)
        xa_s[:, sl] = jax.nn.silu(_conv_block(xpad, cw_ref, cb_ref, Tt, sl))
    for j in range(D_SSM // GATE_TILE):
        sl = slice(GATE_TILE * j, GATE_TILE * (j + 1))
        zs_s[:, sl] = jax.nn.silu(_project(xb, w_ref, sl.start, sl.stop))
    dt_s[...] = _project(xb, w_ref, D_SSM + D_XBC, D_SSM + D_XBC + LANES)
    A = -jnp.exp(alog_ref[...])
    P = SSD_HEADDIM
    blk = GATE_TILE // P
    off_b = D_SSM
    off_c = D_SSM + SSD_GROUPS * D_STATE
    gw = D_SSM // SSD_GROUPS

    def exact01(parts, w01, left):
        one = (lambda p: jnp.dot(w01, p, preferred_element_type=F32)) if left else (
            lambda p: jnp.dot(p, w01, preferred_element_type=F32))
        hi, mid, lo = parts
        return (one(lo) + one(mid)) + one(hi)

    n_chunks = Tt // q
    nb = min(4, n_chunks)

    def within_chunks(cb, carry):
        cidx = [cb * nb + i for i in range(nb)]
        r0s = [pl.multiple_of(c * q, q) for c in cidx]
        xs = [xa_s[pl.ds(r0, q), 0:D_SSM] for r0 in r0s]
        dts = [jax.nn.softplus(dt_s[pl.ds(r0, q), :] + dtb_ref[...]) for r0 in r0s]
        css = [exact01(_split3(dt * A), tri_ref[...], left=True) for dt in dts]
        Es = [exact01(_split3(jnp.concatenate([cs, dt], axis=0)), expand_ref[...], left=False)
              for cs, dt in zip(css, dts)]
        diag = diag_ref[...] != 0.0
        causal = causal_ref[...] != 0.0
        Bgs = [[xa_s[pl.ds(r0, q), off_b + g * D_STATE:off_b + (g + 1) * D_STATE].astype(BF16)
                for g in range(SSD_GROUPS)] for r0 in r0s]
        Cgs = [[xa_s[pl.ds(r0, q), off_c + g * D_STATE:off_c + (g + 1) * D_STATE].astype(BF16)
                for g in range(SSD_GROUPS)] for r0 in r0s]
        CBs = [jnp.concatenate(
            [lax.dot_general(Cg, jnp.concatenate([_pad_time(Bg, P)] * SSD_HPG, axis=0), _NT,
                             preferred_element_type=F32) for Bg, Cg in zip(Bgc, Cgc)], axis=1)
            for Bgc, Cgc in zip(Bgs, Cgs)]
        xws = []
        for i in range(nb):
            E_cs, E_dt = Es[i][0:q], Es[i][q:2 * q]
            cs_last = E_cs[q - 1:q, :]
            xws.append((jnp.exp(cs_last - E_cs) * E_dt * xs[i]).astype(BF16))
            ecs_s[pl.ds(r0s[i], q), :] = jnp.exp(E_cs)
            cdec_s[pl.ds(pl.multiple_of(cidx[i] * SUBLANES, SUBLANES), SUBLANES), :] = jnp.broadcast_to(
                jnp.exp(cs_last), (SUBLANES, D_SSM))
        for i in range(nb):
            for g in range(SSD_GROUPS):
                sl = slice(g * gw, (g + 1) * gw)
                stn_s[cidx[i], :, sl] = lax.dot_general(Bgs[i][g], xws[i][:, sl], _TN, preferred_element_type=F32)
        Mws = []
        for i in range(nb):
            E_cs, E_dt = Es[i][0:q], Es[i][q:2 * q]
            r_cs = jnp.sum(jnp.where(diag, E_cs, 0.0), axis=0, keepdims=True)
            r_dt = jnp.sum(jnp.where(diag, E_dt, 0.0), axis=0, keepdims=True)
            Lm = jnp.where(causal, jnp.exp(jnp.where(causal, E_cs - r_cs, 0.0)), 0.0)
            Mws.append((CBs[i] * Lm * r_dt).astype(BF16))
        for i in range(nb):
            xsb = xs[i].astype(BF16)
            for j in range(D_SSM // GATE_TILE):
                sl = slice(j * GATE_TILE, (j + 1) * GATE_TILE)
                slab = _pad_time(xsb[:, sl], P)
                rhs = jnp.concatenate([slab] * blk, axis=0) * bd_ref[...]
                y_s[pl.ds(r0s[i], q), sl] = jnp.dot(Mws[i][:, sl], rhs, preferred_element_type=F32)
        return carry

    def across_chunks(c, carry):
        r0 = pl.multiple_of(c * q, q)
        cdec = cdec_s[pl.ds(pl.multiple_of(c * SUBLANES, SUBLANES), 1), :]
        for g in range(SSD_GROUPS):
            sl = slice(g * gw, (g + 1) * gw)
            Cg = xa_s[pl.ds(r0, q), off_c + g * D_STATE:off_c + (g + 1) * D_STATE].astype(BF16)
            S = st_s[:, sl]
            yo = jnp.dot(Cg, S.astype(BF16), preferred_element_type=F32)
            y_s[pl.ds(r0, q), sl] = y_s[pl.ds(r0, q), sl] + yo * ecs_s[pl.ds(r0, q), sl]
            st_s[:, sl] = cdec[:, sl] * S + stn_s[c, :, sl]
        return carry

    lax.fori_loop(0, n_chunks // nb, within_chunks, 0)
    lax.fori_loop(0, n_chunks, across_chunks, 0, unroll=min(2, n_chunks))

    @pl.when(t == pl.num_programs(1) - 1)
    def _():
        sN_ref[0] = st_s[...].T.reshape(SSD_HEADS, SSD_HEADDIM, D_STATE)

    y = y_s[...] + dvec_ref[...] * xa_s[:, 0:D_SSM]
    y = y * zs_s[...]
    gw = D_SSM // SSD_GROUPS
    for g in range(SSD_GROUPS):
        sl = slice(g * gw, (g + 1) * gw)
        yg = y[:, sl]
        yg = yg * lax.rsqrt(jnp.mean(yg * yg, axis=-1, keepdims=True) + EPS)
        y_ref[0, :, sl] = (yg * ng_ref[:, sl]).astype(BF16)
    cN_ref[0] = xpad[Tt:Tt + SUBLANES, :]


def _ssd(x, g_mix, w_ssd, c0, s0, cw, cb, dtb, alog, dvec, ng, Tt, q):
    B, L, _ = x.shape
    seq = lambda w: pl.BlockSpec((1, Tt, w), lambda b, t: (b, t, 0))
    per_b = pl.BlockSpec((1, SUBLANES, D_XBC), lambda b, t: (b, 0, 0))
    st = pl.BlockSpec((1, SSD_HEADS, SSD_HEADDIM, D_STATE), lambda b, t: (b, 0, 0, 0))
    masks = _ssd_masks(q)
    return pl.pallas_call(
        functools.partial(_ssd_body, Tt=Tt, q=q),
        grid=(B, L // Tt),
        in_specs=[seq(D_MODEL), _full((1, D_MODEL)), _full(w_ssd.shape),
                  per_b, st, _full((CONV_W, D_XBC)), _full((1, D_XBC)),
                  _full((1, LANES)), _full((1, LANES)), _full((1, D_SSM)), _full((1, D_SSM))]
                 + [_full(m.shape) for m in masks],
        out_specs=[seq(D_SSM), st, per_b],
        out_shape=[jax.ShapeDtypeStruct((B, L, D_SSM), BF16),
                   jax.ShapeDtypeStruct((B, SSD_HEADS, SSD_HEADDIM, D_STATE), F32),
                   jax.ShapeDtypeStruct((B, SUBLANES, D_XBC), F32)],
        scratch_shapes=[pltpu.VMEM((Tt + SUBLANES, D_XBC), F32), pltpu.VMEM((Tt, D_XBC), F32),
                        pltpu.VMEM((Tt, D_SSM), F32), pltpu.VMEM((D_STATE, D_SSM), F32),
                        pltpu.VMEM((Tt, LANES), F32), pltpu.VMEM((Tt, D_SSM), F32),
                        pltpu.VMEM((Tt // q, D_STATE, D_SSM), F32), pltpu.VMEM((Tt, D_SSM), F32),
                        pltpu.VMEM((Tt // q * SUBLANES, D_SSM), F32)],
        compiler_params=_params(2),
        name="ssd",
    )(x, g_mix, w_ssd, c0, s0, cw, cb, dtb, alog, dvec, ng, *masks)


def _out_router_body(x_ref, yl_ref, ys_ref, wo_ref, gf_ref, wr_ref, br_ref, cin_ref,
                     h1_ref, xn_ref, meta_ref, gate_ref, cnt_ref, carry, *, tm):
    step = pl.program_id(0)

    @pl.when(step == 0)
    def _():
        carry[...] = cin_ref[...]

    mix = jnp.concatenate([yl_ref[...], ys_ref[...]], axis=1)
    h1 = x_ref[...] + jnp.dot(mix, wo_ref[...], preferred_element_type=F32)
    h1_ref[...] = h1
    xn = _rmsnorm(h1, gf_ref[...])
    xn_ref[...] = xn

    xh = xn.astype(BF16)
    xm = (xn - xh.astype(F32)).astype(BF16)
    small = (jnp.dot(xh, wr_ref[1], preferred_element_type=F32)
             + jnp.dot(xm, wr_ref[0], preferred_element_type=F32))
    logits = small + jnp.dot(xh, wr_ref[0], preferred_element_type=F32) + br_ref[...]
    lt = logits.T
    rows8 = lax.broadcasted_iota(I32, (SUBLANES, tm), 0)
    lg = jnp.where(rows8 < MOE_GROUPS, lt[0:SUBLANES, :], -jnp.inf)
    eg = jnp.exp(lg - jnp.max(lg, axis=0, keepdims=True))
    pg = eg / jnp.sum(eg, axis=0, keepdims=True)
    pgs = jnp.max(pg, axis=0, keepdims=True)
    rows8f = rows8.astype(F32)
    gsel = jnp.min(jnp.where(pg == pgs, rows8f, float(SUBLANES)), axis=0, keepdims=True)

    rows32 = lax.broadcasted_iota(I32, (N_EXPERTS, tm), 0)
    rows32f = rows32.astype(F32)
    grp = (rows32 // EXPERTS_PER_GROUP).astype(F32)
    le = lt[ROUTER_E0:ROUTER_E0 + N_EXPERTS, :]
    ing = grp == gsel
    lem = jnp.where(ing, le, -jnp.inf)
    ee = jnp.exp(lem - jnp.max(lem, axis=0, keepdims=True))
    pe = ee / jnp.sum(ee, axis=0, keepdims=True)
    pe1 = jnp.where(ing, pe, -1.0)
    v1 = jnp.max(pe1, axis=0, keepdims=True)
    i1 = jnp.min(jnp.where(pe1 == v1, rows32f, float(N_EXPERTS)), axis=0, keepdims=True)
    pe2 = jnp.where(rows32f == i1, -1.0, pe1)
    v2 = jnp.max(pe2, axis=0, keepdims=True)
    i2 = jnp.min(jnp.where(pe2 == v2, rows32f, float(N_EXPERTS)), axis=0, keepdims=True)
    sv = v1 + v2
    w1 = v1 / sv * pgs
    w2 = v2 / sv * pgs

    oh1 = rows32f == i1
    oh2 = rows32f == i2
    oh = jnp.where(oh1 | oh2, 1.0, 0.0)
    before = (lax.broadcasted_iota(I32, (tm, tm), 0) < lax.broadcasted_iota(I32, (tm, tm), 1))
    pref = jnp.dot(oh.astype(BF16), jnp.where(before, 1.0, 0.0).astype(BF16), preferred_element_type=F32)
    pref = pref + carry[:, 0:1]
    r1 = jnp.sum(jnp.where(oh1, pref, 0.0), axis=0, keepdims=True)
    r2 = jnp.sum(jnp.where(oh2, pref, 0.0), axis=0, keepdims=True)
    carry[...] = carry[...] + jnp.sum(oh, axis=1, keepdims=True)
    cnt_ref[...] = carry[...]

    meta = jnp.where(rows8 == 0, i1, jnp.where(rows8 == 1, i2, jnp.where(rows8 == 2, r1, jnp.where(rows8 == 3, r2, 0.0))))
    meta_ref[...] = meta.astype(I32)
    gate_ref[...] = jnp.where(rows8 == 0, w1, jnp.where(rows8 == 1, w2, 0.0))


def _out_router(x2d, y_lru, y_ssd, w_out, gf, wr, br, counts_in, tm):
    T = x2d.shape[0]
    row = lambda w: pl.BlockSpec((tm, w), lambda i: (i, 0))
    col = pl.BlockSpec((SUBLANES, tm), lambda i: (0, i))
    return pl.pallas_call(
        functools.partial(_out_router_body, tm=tm),
        grid=(T // tm,),
        in_specs=[row(D_MODEL), row(D_LRU), row(D_SSM), _full((D_LRU + D_SSM, D_MODEL)), _full((1, D_MODEL)),
                  _full((2, D_MODEL, LANES)), _full((1, LANES)), _full((N_EXPERTS, LANES))],
        out_specs=[row(D_MODEL), row(D_MODEL), col, col, _full((N_EXPERTS, LANES))],
        out_shape=[jax.ShapeDtypeStruct((T, D_MODEL), F32), jax.ShapeDtypeStruct((T, D_MODEL), F32),
                   jax.ShapeDtypeStruct((SUBLANES, T), I32), jax.ShapeDtypeStruct((SUBLANES, T), F32),
                   jax.ShapeDtypeStruct((N_EXPERTS, LANES), F32)],
        scratch_shapes=[pltpu.VMEM((N_EXPERTS, LANES), F32)],
        compiler_params=_params(1),
        name="out_router",
    )(x2d, y_lru, y_ssd, w_out, gf, wr, br, counts_in)


def _row_copy(src_hbm, src_row, dst, dst_row, sem):
    return pltpu.make_async_copy(src_hbm.at[pl.ds(src_row, 1)], dst.at[pl.ds(dst_row, 1)], sem)


def _positions_body(offs_ref, meta_ref, pos_ref):
    m = meta_ref[...]
    base = jnp.zeros_like(m)
    for e in range(N_EXPERTS):
        base = jnp.where(m == e, offs_ref[e], base)
    pos_ref[...] = base + pltpu.roll(m, SUBLANES - 2, 0)


def _positions(offs, meta):
    T = meta.shape[1]
    tb = min(T, 4096)
    blk = pl.BlockSpec((SUBLANES, tb), lambda i, offs: (0, i))
    return pl.pallas_call(
        _positions_body,
        grid_spec=pltpu.PrefetchScalarGridSpec(num_scalar_prefetch=1, grid=(T // tb,), in_specs=[blk], out_specs=blk),
        out_shape=jax.ShapeDtypeStruct((SUBLANES, T), I32),
        compiler_params=_params(1),
        name="positions",
    )(offs, meta)


def _wait_rows(src, dst, sem):
    pltpu.make_async_copy(src, dst, sem).wait()


def _dispatch_body(zf_ref, pos_ref, *rest, tm, tme, steps):
    n_groups = len(steps)
    xn_refs = rest[:n_groups]
    xs_hbm, zbuf, sem, zsem = rest[n_groups:]
    n_tiles = xs_hbm.shape[0] // tme
    step = pl.program_id(0)

    def zero_copy(i):
        return pltpu.make_async_copy(zbuf, xs_hbm.at[pl.ds(pl.multiple_of(i * tme, tme), tme)], zsem)

    @pl.when(step == 0)
    def _():
        zbuf[...] = jnp.zeros_like(zbuf)

        def start(i, c):
            @pl.when(zf_ref[i] != 0)
            def _():
                zero_copy(i).start()
            return c

        def wait(i, c):
            @pl.when(zf_ref[i] != 0)
            def _():
                zero_copy(i).wait()
            return c

        lax.fori_loop(0, n_tiles, start, 0)
        lax.fori_loop(0, n_tiles, wait, 0)

    first = 0
    for xn_ref, n in zip(xn_refs, steps):
        @pl.when(jnp.logical_and(step >= first, step < first + n))
        def _(xn_ref=xn_ref):
            for t in range(tm):
                for k in range(2):
                    _row_copy(xn_ref, t, xs_hbm, pos_ref[k, t], sem).start(priority=k)
            for k in range(2):
                _wait_rows(xn_ref, xs_hbm.at[pl.ds(0, tm)], sem)
        first += n


def _dispatch(zflag, pos, xns, n_rows, tm, tme):
    steps = [xn.shape[0] // tm for xn in xns]
    firsts = [sum(steps[:g]) for g in range(len(steps))]

    def group_spec(first, n):
        return pl.BlockSpec((tm, D_MODEL), lambda i, zf: (jnp.clip(i - first, 0, n - 1), 0))

    return pl.pallas_call(
        functools.partial(_dispatch_body, tm=tm, tme=tme, steps=tuple(steps)),
        grid_spec=pltpu.PrefetchScalarGridSpec(
            num_scalar_prefetch=1,
            grid=(sum(steps),),
            in_specs=[pl.BlockSpec((SUBLANES, tm), lambda i, zf: (0, i), memory_space=pltpu.SMEM)]
                     + [group_spec(f, n) for f, n in zip(firsts, steps)],
            out_specs=pl.BlockSpec(memory_space=pl.ANY),
            scratch_shapes=[pltpu.VMEM((tme, D_MODEL), F32), pltpu.SemaphoreType.DMA(()),
                            pltpu.SemaphoreType.DMA(())],
        ),
        out_shape=jax.ShapeDtypeStruct((n_rows, D_MODEL), F32),
        compiler_params=_params(1),
        name="dispatch",
    )(zflag, pos, *xns)


def _experts_body(te_ref, na_ref, x_ref, w1_ref, w3_ref, w2_ref, o_ref, w1b, w3b, w2b):
    i = pl.program_id(0)
    changed = jnp.logical_or(i == 0, te_ref[i] != te_ref[jnp.maximum(i - 1, 0)])

    @pl.when(changed)
    def _():
        w1b[...] = w1_ref[0].astype(BF16)
        w3b[...] = w3_ref[0].astype(BF16)
        w2b[...] = w2_ref[0].astype(BF16)

    @pl.when(i < na_ref[0])
    def _():
        xb = x_ref[...].astype(BF16)
        a = jnp.dot(xb, w1b[...], preferred_element_type=F32)
        b = jnp.dot(xb, w3b[...], preferred_element_type=F32)
        hd = (jax.nn.silu(a) * b).astype(BF16)
        o_ref[...] = jnp.dot(hd, w2b[...], preferred_element_type=F32)

    @pl.when(i >= na_ref[0])
    def _():
        o_ref[...] = jnp.zeros_like(o_ref)


def _experts(tile_e, n_active, xs, w1, w3, w2, tme):
    n_rows = xs.shape[0]
    row = pl.BlockSpec((tme, D_MODEL), lambda i, te, na: (i, 0))
    return pl.pallas_call(
        _experts_body,
        grid_spec=pltpu.PrefetchScalarGridSpec(
            num_scalar_prefetch=2,
            grid=(n_rows // tme,),
            in_specs=[row,
                      pl.BlockSpec((1, D_MODEL, D_EXPERT), lambda i, te, na: (te[i], 0, 0)),
                      pl.BlockSpec((1, D_MODEL, D_EXPERT), lambda i, te, na: (te[i], 0, 0)),
                      pl.BlockSpec((1, D_EXPERT, D_MODEL), lambda i, te, na: (te[i], 0, 0))],
            out_specs=row,
            scratch_shapes=[pltpu.VMEM((D_MODEL, D_EXPERT), BF16), pltpu.VMEM((D_MODEL, D_EXPERT), BF16),
                            pltpu.VMEM((D_EXPERT, D_MODEL), BF16)],
        ),
        out_shape=jax.ShapeDtypeStruct((n_rows, D_MODEL), F32),
        compiler_params=_params(1),
        name="experts",
    )(tile_e, n_active, xs, w1, w3, w2)


def _combine_body(pos_ref, nxt_ref, gate_ref, h1_ref, fg_ref, ys_hbm, y_ref, ybuf, sem, *, tm):
    step = pl.program_id(0)
    slot = step % 2

    def gather(p_ref, s):
        for t in range(tm):
            for k in range(2):
                _row_copy(ys_hbm, p_ref[k, t], ybuf.at[s, k], t, sem.at[s]).start(priority=k)

    @pl.when(step == 0)
    def _():
        gather(pos_ref, 0)

    @pl.when(step + 1 < pl.num_programs(0))
    def _():
        gather(nxt_ref, 1 - slot)

    for k in range(2):
        _wait_rows(ys_hbm.at[pl.ds(0, tm)], ybuf.at[slot, k], sem.at[slot])

    eye = lax.broadcasted_iota(I32, (tm, tm), 0) == lax.broadcasted_iota(I32, (tm, tm), 1)
    g1 = jnp.sum(jnp.where(eye, gate_ref[0:1, :], 0.0), axis=1, keepdims=True)
    g2 = jnp.sum(jnp.where(eye, gate_ref[1:2, :], 0.0), axis=1, keepdims=True)
    h2 = h1_ref[...] + (g1 * ybuf[slot, 0] + g2 * ybuf[slot, 1])
    y_ref[...] = _rmsnorm(h2, fg_ref[...])


def _combine(pos, gates, h1, fg, ys, tm):
    T = h1.shape[0]
    last = T // tm - 1
    return pl.pallas_call(
        functools.partial(_combine_body, tm=tm),
        grid=(T // tm,),
        in_specs=[pl.BlockSpec((SUBLANES, tm), lambda i: (0, i), memory_space=pltpu.SMEM),
                  pl.BlockSpec((SUBLANES, tm), lambda i: (0, jnp.minimum(i + 1, last)), memory_space=pltpu.SMEM),
                  pl.BlockSpec((SUBLANES, tm), lambda i: (0, i)),
                  pl.BlockSpec((tm, D_MODEL), lambda i: (i, 0)),
                  pl.BlockSpec((1, D_MODEL), lambda i: (0, 0)),
                  pl.BlockSpec(memory_space=pl.ANY)],
        out_specs=pl.BlockSpec((tm, D_MODEL), lambda i: (i, 0)),
        scratch_shapes=[pltpu.VMEM((2, 2, tm, D_MODEL), F32), pltpu.SemaphoreType.DMA((2,))],
        out_shape=jax.ShapeDtypeStruct((T, D_MODEL), F32),
        compiler_params=_params(1),
        name="combine",
    )(pos, pos, gates, h1, fg, ys)


def _blockdiag(w):
    per = GATE_TILE // LRU_BLOCK
    w4 = w.reshape(LRU_BLOCKS // per, per, LRU_BLOCK, LRU_BLOCK)
    eye = jnp.eye(per, dtype=w.dtype)
    return jnp.einsum('jbio,bc->jbico', w4, eye).reshape(LRU_BLOCKS // per, GATE_TILE, GATE_TILE)


def _pad_rows(c):
    return jnp.pad(c, ((0, 0), (SUBLANES - (CONV_W - 1), 0), (0, 0)))


def _router_cols(group_part, expert_part):
    r = group_part.shape[0]
    out = jnp.zeros((r, LANES), F32)
    out = out.at[:, 0:MOE_GROUPS].set(group_part)
    return out.at[:, ROUTER_E0:ROUTER_E0 + N_EXPERTS].set(expert_part)


def _lane_row(v, width=LANES):
    return jnp.pad(v, (0, width - v.shape[0])).reshape(1, width)


def _prep(norm_mix_g, w_in, lru_conv_w, lru_conv_b, lru_wa, lru_ba, lru_wx, lru_bx, lru_lambda,
          ssd_conv_w, ssd_conv_b, ssd_dt_bias, ssd_a_log, ssd_d, ssd_norm_g, w_out,
          norm_ffn_g, router_group_w, router_group_b, router_expert_w, router_expert_b,
          moe_w1, moe_w3, moe_w2, final_norm_g):
    w = w_in[0]
    wr = _router_cols(router_group_w[0], router_expert_w[0])
    wr_hi = wr.astype(BF16)
    P = dict(
        g_mix=norm_mix_g[0].reshape(1, D_MODEL),
        w_lru=w[:, :2 * D_LRU].astype(BF16),
        w_ssd=jnp.pad(w[:, 2 * D_LRU:], ((0, 0), (0, LANES - SSD_HEADS))).astype(BF16),
        lru_cw=lru_conv_w[0], lru_cb=lru_conv_b[0].reshape(1, D_LRU),
        wbd=jnp.concatenate([_blockdiag(lru_wa[0]), _blockdiag(lru_wx[0])], axis=2).astype(BF16),
        ba=lru_ba[0].reshape(1, D_LRU), bx=lru_bx[0].reshape(1, D_LRU), lam=lru_lambda[0].reshape(1, D_LRU),
        ssd_cw=ssd_conv_w[0], ssd_cb=ssd_conv_b[0].reshape(1, D_XBC),
        dtb=_lane_row(ssd_dt_bias[0]), alog=_lane_row(ssd_a_log[0]),
        dvec=jnp.repeat(ssd_d[0], SSD_HEADDIM).reshape(1, D_SSM),
        ng=ssd_norm_g[0].reshape(1, D_SSM),
        w_out=w_out[0].astype(BF16),
        g_ffn=norm_ffn_g[0].reshape(1, D_MODEL),
        wr=jnp.stack([wr_hi, (wr - wr_hi.astype(F32)).astype(BF16)]),
        br=_router_cols(router_group_b[0][None], router_expert_b[0][None]),
        w1=moe_w1[0], w3=moe_w3[0], w2=moe_w2[0],
        g_final=final_norm_g.reshape(1, D_MODEL),
    )
    return P


def _expert_layout(counts, n_pairs, tme):
    cnt = counts[:, 0].astype(I32)
    padded = ((cnt + tme - 1) // tme) * tme
    ends = jnp.cumsum(padded)
    offs = ends - padded
    n_tiles = n_pairs // tme + N_EXPERTS
    n_active = ends[-1] // tme
    tiles = jnp.arange(n_tiles, dtype=I32)
    tile_e = jnp.sum((tiles * tme)[:, None] >= ends[None, :], axis=1).astype(I32)
    last_e = jnp.sum((n_active - 1) * tme >= ends).astype(I32)
    tile_e = jnp.where(tiles < n_active, tile_e, last_e)
    is_last = jnp.any((tiles[:, None] + 1) * tme == ends[None, :], axis=1)
    zflag = jnp.logical_or(is_last, tiles >= n_active).astype(I32)
    return offs.astype(I32), tile_e, n_active.reshape(1).astype(I32), zflag, n_tiles * tme


def _mixer_router(x, lru_h0, lru_c0, ssd_h0, ssd_c0, P, start_pos, counts_in):
    B, L, _ = x.shape
    T = B * L
    Tt = min(MIX_TILE, L)
    q = min(SSD_CHUNK, L)
    tm = min(ROW_TILE, T)
    x2d = x.reshape(T, D_MODEL)

    y_lru, lru_h, lru_c = _lru(x, P['g_mix'], P['w_lru'], _pad_rows(lru_c0), lru_h0.reshape(B, 1, D_LRU),
                               P['lru_cw'], P['lru_cb'], P['wbd'], P['ba'], P['bx'], P['lam'], Tt, start_pos)
    y_ssd, ssd_h, ssd_c = _ssd(x, P['g_mix'], P['w_ssd'], _pad_rows(ssd_c0), ssd_h0,
                               P['ssd_cw'], P['ssd_cb'], P['dtb'], P['alog'], P['dvec'], P['ng'], Tt, q)
    h1, xn, meta, gates, counts = _out_router(
        x2d, y_lru.reshape(T, D_LRU), y_ssd.reshape(T, D_SSM), P['w_out'], P['g_ffn'],
        P['wr'], P['br'], counts_in, min(MIX_TILE, T))

    hist = SUBLANES - (CONV_W - 1)
    states = (lru_h.reshape(1, B, D_LRU), lru_c[:, hist:][None], ssd_h[None], ssd_c[:, hist:][None])
    return dict(h1=h1, xn=xn, meta=meta, gates=gates, shape=(B, L, D_MODEL), tm=tm), counts, states


def _moe_final(groups, counts, P):
    tme = EXPERT_TILE
    n_pairs = 2 * sum(g['h1'].shape[0] for g in groups)
    offs, tile_e, n_active, zflag, n_rows = _expert_layout(counts, n_pairs, tme)
    tm = groups[0]['tm']
    assert all(g['tm'] == tm for g in groups)
    for g in groups:
        g['pos'] = _positions(offs, g['meta'])
    pos_all = jnp.concatenate([g['pos'] for g in groups], axis=1)
    xs = _dispatch(zflag, pos_all, [g['xn'] for g in groups], n_rows, tm, tme)
    ys = _experts(tile_e, n_active, xs, P['w1'], P['w3'], P['w2'], tme)
    return [_combine(g['pos'], g['gates'], g['h1'], P['g_final'], ys, g['tm']).reshape(g['shape'])
            for g in groups]


def kernel(x_prompt, x_sample, state_lru_h, state_lru_conv, state_ssd, state_ssd_conv, norm_mix_g, w_in, lru_conv_w, lru_conv_b, lru_wa, lru_ba, lru_wx, lru_bx, lru_lambda, ssd_conv_w, ssd_conv_b, ssd_dt_bias, ssd_a_log, ssd_d, ssd_norm_g, w_out, norm_ffn_g, router_group_w, router_group_b, router_expert_w, router_expert_b, moe_w1, moe_w3, moe_w2, final_norm_g):
    P = _prep(norm_mix_g, w_in, lru_conv_w, lru_conv_b, lru_wa, lru_ba, lru_wx, lru_bx, lru_lambda,
              ssd_conv_w, ssd_conv_b, ssd_dt_bias, ssd_a_log, ssd_d, ssd_norm_g, w_out,
              norm_ffn_g, router_group_w, router_group_b, router_expert_w, router_expert_b,
              moe_w1, moe_w3, moe_w2, final_norm_g)
    bp = x_prompt.shape[0]
    gp, counts, (a1, a2, a3, a4) = _mixer_router(
        x_prompt,
        jnp.zeros((bp, D_LRU), F32), jnp.zeros((bp, CONV_W - 1, D_LRU), F32),
        jnp.zeros((bp, SSD_HEADS, SSD_HEADDIM, D_STATE), F32), jnp.zeros((bp, CONV_W - 1, D_XBC), F32),
        P, 0, jnp.zeros((N_EXPERTS, LANES), F32))
    gs, counts, (b1, b2, b3, b4) = _mixer_router(
        x_sample, state_lru_h[0], state_lru_conv[0], state_ssd[0], state_ssd_conv[0], P, PAST_LEN, counts)
    yp, ys = _moe_final([gp, gs], counts, P)
    return (yp, ys, a1, a2, a3, a4, b1, b2, b3, b4)
```

```python
import functools

import jax
import jax.numpy as jnp
from jax import lax
from jax.experimental import pallas as pl
from jax.experimental.pallas import tpu as pltpu

F32 = jnp.float32
BF16 = jnp.bfloat16
I32 = jnp.int32

D_MODEL = 1024
D_LRU = 1024
LRU_BLOCKS = 16
LRU_BLOCK = 64
LRU_C = 8.0
CONV_W = 4
D_SSM = 1024
SSD_HEADDIM = 64
SSD_HEADS = 16
SSD_GROUPS = 2
SSD_HPG = 8
D_STATE = 128
D_XBC = 1536
MOE_GROUPS = 4
EXPERTS_PER_GROUP = 8
N_EXPERTS = 32
D_EXPERT = 512
EPS = 1e-6
SSD_CHUNK = 64
PAST_LEN = 1024

LANES = 128
SUBLANES = 8
GATE_TILE = 256
ROW_TILE = 256
MIX_TILE = 512
EXPERT_TILE = 512
ROUTER_E0 = 32
VMEM_LIMIT = 52 * 1024 * 1024

_NT = (((1,), (1,)), ((), ()))
_TN = (((0,), (0,)), ((), ()))


def _params(n_axes):
    return pltpu.CompilerParams(dimension_semantics=("arbitrary",) * n_axes,
                                vmem_limit_bytes=VMEM_LIMIT)


def _rmsnorm(x, g):
    return x * lax.rsqrt(jnp.mean(x * x, axis=-1, keepdims=True) + EPS) * g


def _full(shape):
    n = len(shape)
    return pl.BlockSpec(shape, lambda *_: (0,) * n)


def _project(xb, w_ref, lo, hi):
    return jnp.dot(xb, w_ref[:, lo:hi], preferred_element_type=F32)


def _conv_block(xpad, cw_ref, cb_ref, Tt, sl):
    cw = cw_ref[:, sl]
    full = xpad[:, sl]
    y = cb_ref[:, sl]
    for k in range(CONV_W):
        shift = CONV_W - 1 - k
        xk = pltpu.roll(full, shift, 0) if shift else full
        y = y + xk[SUBLANES:SUBLANES + Tt, :] * cw[k:k + 1, :]
    return y


def _carry_history(t, xpad, c0_ref, Tt):
    @pl.when(t == 0)
    def _():
        xpad[0:SUBLANES, :] = c0_ref[0]

    @pl.when(t > 0)
    def _():
        xpad[0:SUBLANES, :] = xpad[Tt:Tt + SUBLANES, :]


def _lru_body(x_ref, gm_ref, w_ref, c0_ref, h0_ref, cw_ref, cb_ref, wbd_ref, ba_ref, bx_ref, lam_ref,
              y_ref, hN_ref, cN_ref, xpad, a_s, u_s, gel_s, hcar, *, Tt, start_pos):
    t = pl.program_id(1)
    _carry_history(t, xpad, c0_ref, Tt)

    @pl.when(t == 0)
    def _():
        hcar[...] = jnp.broadcast_to(h0_ref[0], (SUBLANES, D_LRU))

    xb = _rmsnorm(x_ref[0], gm_ref[...]).astype(BF16)
    sp = jax.nn.softplus(-lam_ref[...])
    pos0 = (lax.broadcasted_iota(I32, (Tt, 1), 0) + t * Tt + start_pos) == 0
    for j in range(D_LRU // GATE_TILE):
        sl = slice(GATE_TILE * j, GATE_TILE * (j + 1))
        xpad[SUBLANES:SUBLANES + Tt, sl] = _project(xb, w_ref, sl.start, sl.stop)
        xc = _conv_block(xpad, cw_ref, cb_ref, Tt, sl)
        ga = jnp.dot(xc.astype(BF16), wbd_ref[j], preferred_element_type=F32)
        r = jax.nn.sigmoid(ga[:, :GATE_TILE] + ba_ref[:, sl])
        i = jax.nn.sigmoid(ga[:, GATE_TILE:] + bx_ref[:, sl])
        a = jnp.exp((-LRU_C * r) * sp[:, sl])
        mult = jnp.where(pos0, 1.0, jnp.sqrt(1.0 - a * a))
        a_s[:, sl] = a
        u_s[:, sl] = mult * i * xc
        g = _project(xb, w_ref, D_LRU + sl.start, D_LRU + sl.stop)
        gel_s[:, sl] = jax.nn.gelu(g, approximate=True)

    rows = lax.broadcasted_iota(I32, (SUBLANES, D_LRU), 0)

    def scan8(gi, hprev):
        r0 = pl.multiple_of(gi * SUBLANES, SUBLANES)
        a8 = a_s[pl.ds(r0, SUBLANES), :]
        u8 = u_s[pl.ds(r0, SUBLANES), :]
        for s in (1, 2, 4):
            ok = rows >= s
            u_sh = pltpu.roll(u8, s, 0)
            a_sh = pltpu.roll(a8, s, 0)
            u8 = jnp.where(ok, u8 + a8 * u_sh, u8)
            a8 = jnp.where(ok, a8 * a_sh, a8)
        h8 = u8 + a8 * hprev
        u_s[pl.ds(r0, SUBLANES), :] = h8
        return jnp.broadcast_to(h8[SUBLANES - 1:SUBLANES, :], (SUBLANES, D_LRU))

    hlast = lax.fori_loop(0, Tt // SUBLANES, scan8, hcar[...])
    hcar[...] = hlast
    y_ref[0] = (u_s[...] * gel_s[...]).astype(BF16)
    hN_ref[0] = hlast[0:1, :]
    cN_ref[0] = xpad[Tt:Tt + SUBLANES, :]


def _lru(x, g_mix, w_lru, c0, h0, cw, cb, wbd, ba, bx, lam, Tt, start_pos):
    B, L, _ = x.shape
    seq = pl.BlockSpec((1, Tt, D_LRU), lambda b, t: (b, t, 0))
    per_b = lambda r: pl.BlockSpec((1, r, D_LRU), lambda b, t: (b, 0, 0))
    return pl.pallas_call(
        functools.partial(_lru_body, Tt=Tt, start_pos=start_pos),
        grid=(B, L // Tt),
        in_specs=[pl.BlockSpec((1, Tt, D_MODEL), lambda b, t: (b, t, 0)), _full((1, D_MODEL)), _full(w_lru.shape),
                  per_b(SUBLANES), per_b(1), _full((CONV_W, D_LRU)), _full((1, D_LRU)),
                  _full(wbd.shape), _full((1, D_LRU)), _full((1, D_LRU)), _full((1, D_LRU))],
        out_specs=[seq, per_b(1), per_b(SUBLANES)],
        out_shape=[jax.ShapeDtypeStruct((B, L, D_LRU), BF16),
                   jax.ShapeDtypeStruct((B, 1, D_LRU), F32),
                   jax.ShapeDtypeStruct((B, SUBLANES, D_LRU), F32)],
        scratch_shapes=[pltpu.VMEM((Tt + SUBLANES, D_LRU), F32), pltpu.VMEM((Tt, D_LRU), F32),
                        pltpu.VMEM((Tt, D_LRU), F32), pltpu.VMEM((Tt, D_LRU), F32),
                        pltpu.VMEM((SUBLANES, D_LRU), F32)],
        compiler_params=_params(2),
        name="lru",
    )(x, g_mix, w_lru, c0, h0, cw, cb, wbd, ba, bx, lam)


def _split3(v):
    hi = v.astype(BF16)
    r1 = v - hi.astype(F32)
    mid = r1.astype(BF16)
    lo = (r1 - mid.astype(F32)).astype(BF16)
    return hi, mid, lo


def _pad_time(v, rows):
    if v.shape[0] == rows:
        return v
    return jnp.concatenate([v, jnp.zeros((rows - v.shape[0], v.shape[1]), v.dtype)], axis=0)


def _ssd_masks(q):
    P = SSD_HEADDIM
    tri = jnp.arange(q)[:, None] >= jnp.arange(q)[None, :]
    expand = jnp.arange(LANES)[:, None] == jnp.arange(D_SSM)[None, :] // P
    row_q = jnp.arange(q)[:, None]
    lane_k = jnp.arange(D_SSM)[None, :] % P
    diag = row_q == lane_k
    causal = row_q >= lane_k
    bd = jnp.arange(GATE_TILE)[:, None] // P == jnp.arange(GATE_TILE)[None, :] // P
    return (tri.astype(BF16), expand.astype(BF16), diag.astype(F32), causal.astype(F32), bd.astype(BF16))


def _ssd_body(x_ref, gm_ref, w_ref, c0_ref, s0_ref, cw_ref, cb_ref, dtb_ref, alog_ref, dvec_ref, ng_ref,
              tri_ref, expand_ref, diag_ref, causal_ref, bd_ref,
              y_ref, sN_ref, cN_ref, xpad, xa_s, y_s, st_s, dt_s, zs_s, stn_s, ecs_s, cdec_s, *, Tt, q):
    t = pl.program_id(1)
    _carry_history(t, xpad, c0_ref, Tt)

    @pl.when(t == 0)
    def _():
        st_s[...] = s0_ref[0].reshape(D_SSM, D_STATE).T

    xb = _rmsnorm(x_ref[0], gm_ref[...]).astype(BF16)
    for j in range(D_XBC // GATE_TILE):
        sl = slice(GATE_TILE * j, GATE_TILE * (j + 1))
        xpad[SUBLANES:SUBLANES + Tt, sl] = _project(xb, w_ref, D_SSM + sl.start, D_SSM + sl.stop)
        xa_s[:, sl] = jax.nn.silu(_conv_block(xpad, cw_ref, cb_ref, Tt, sl))
    for j in range(D_SSM // GATE_TILE):
        sl = slice(GATE_TILE * j, GATE_TILE * (j + 1))
        zs_s[:, sl] = jax.nn.silu(_project(xb, w_ref, sl.start, sl.stop))
    dt_s[...] = _project(xb, w_ref, D_SSM + D_XBC, D_SSM + D_XBC + LANES)
    A = -jnp.exp(alog_ref[...])
    P = SSD_HEADDIM
    blk = GATE_TILE // P
    off_b = D_SSM
    off_c = D_SSM + SSD_GROUPS * D_STATE
    gw = D_SSM // SSD_GROUPS

    def exact01(parts, w01, left):
        one = (lambda p: jnp.dot(w01, p, preferred_element_type=F32)) if left else (
            lambda p: jnp.dot(p, w01, preferred_element_type=F32))
        hi, mid, lo = parts
        return (one(lo) + one(mid)) + one(hi)

    n_chunks = Tt // q
    nb = min(4, n_chunks)

    def within_chunks(cb, carry):
        cidx = [cb * nb + i for i in range(nb)]
        r0s = [pl.multiple_of(c * q, q) for c in cidx]
        xs = [xa_s[pl.ds(r0, q), 0:D_SSM] for r0 in r0s]
        dts = [jax.nn.softplus(dt_s[pl.ds(r0, q), :] + dtb_ref[...]) for r0 in r0s]
        css = [exact01(_split3(dt * A), tri_ref[...], left=True) for dt in dts]
        Es = [exact01(_split3(jnp.concatenate([cs, dt], axis=0)), expand_ref[...], left=False)
              for cs, dt in zip(css, dts)]
        diag = diag_ref[...] != 0.0
        causal = causal_ref[...] != 0.0
        Bgs = [[xa_s[pl.ds(r0, q), off_b + g * D_STATE:off_b + (g + 1) * D_STATE].astype(BF16)
                for g in range(SSD_GROUPS)] for r0 in r0s]
        Cgs = [[xa_s[pl.ds(r0, q), off_c + g * D_STATE:off_c + (g + 1) * D_STATE].astype(BF16)
                for g in range(SSD_GROUPS)] for r0 in r0s]
        CBs = [jnp.concatenate(
            [lax.dot_general(Cg, jnp.concatenate([_pad_time(Bg, P)] * SSD_HPG, axis=0), _NT,
                             preferred_element_type=F32) for Bg, Cg in zip(Bgc, Cgc)], axis=1)
            for Bgc, Cgc in zip(Bgs, Cgs)]
        xws = []
        for i in range(nb):
            E_cs, E_dt = Es[i][0:q], Es[i][q:2 * q]
            cs_last = E_cs[q - 1:q, :]
            xws.append((jnp.exp(cs_last - E_cs) * E_dt * xs[i]).astype(BF16))
            ecs_s[pl.ds(r0s[i], q), :] = jnp.exp(E_cs)
            cdec_s[pl.ds(pl.multiple_of(cidx[i] * SUBLANES, SUBLANES), SUBLANES), :] = jnp.broadcast_to(
                jnp.exp(cs_last), (SUBLANES, D_SSM))
        for i in range(nb):
            for g in range(SSD_GROUPS):
                sl = slice(g * gw, (g + 1) * gw)
                stn_s[cidx[i], :, sl] = lax.dot_general(Bgs[i][g], xws[i][:, sl], _TN, preferred_element_type=F32)
        Mws = []
        for i in range(nb):
            E_cs, E_dt = Es[i][0:q], Es[i][q:2 * q]
            r_cs = jnp.sum(jnp.where(diag, E_cs, 0.0), axis=0, keepdims=True)
            r_dt = jnp.sum(jnp.where(diag, E_dt, 0.0), axis=0, keepdims=True)
            Lm = jnp.where(causal, jnp.exp(jnp.where(causal, E_cs - r_cs, 0.0)), 0.0)
            Mws.append((CBs[i] * Lm * r_dt).astype(BF16))
        for i in range(nb):
            xsb = xs[i].astype(BF16)
            for j in range(D_SSM // GATE_TILE):
                sl = slice(j * GATE_TILE, (j + 1) * GATE_TILE)
                slab = _pad_time(xsb[:, sl], P)
                rhs = jnp.concatenate([slab] * blk, axis=0) * bd_ref[...]
                y_s[pl.ds(r0s[i], q), sl] = jnp.dot(Mws[i][:, sl], rhs, preferred_element_type=F32)
        return carry

    def across_chunks(c, carry):
        r0 = pl.multiple_of(c * q, q)
        cdec = cdec_s[pl.ds(pl.multiple_of(c * SUBLANES, SUBLANES), 1), :]
        for g in range(SSD_GROUPS):
            sl = slice(g * gw, (g + 1) * gw)
            Cg = xa_s[pl.ds(r0, q), off_c + g * D_STATE:off_c + (g + 1) * D_STATE].astype(BF16)
            S = st_s[:, sl]
            yo = jnp.dot(Cg, S.astype(BF16), preferred_element_type=F32)
            y_s[pl.ds(r0, q), sl] = y_s[pl.ds(r0, q), sl] + yo * ecs_s[pl.ds(r0, q), sl]
            st_s[:, sl] = cdec[:, sl] * S + stn_s[c, :, sl]
        return carry

    lax.fori_loop(0, n_chunks // nb, within_chunks, 0)
    lax.fori_loop(0, n_chunks, across_chunks, 0, unroll=min(2, n_chunks))

    @pl.when(t == pl.num_programs(1) - 1)
    def _():
        sN_ref[0] = st_s[...].T.reshape(SSD_HEADS, SSD_HEADDIM, D_STATE)

    y = y_s[...] + dvec_ref[...] * xa_s[:, 0:D_SSM]
    y = y * zs_s[...]
    gw = D_SSM // SSD_GROUPS
    for g in range(SSD_GROUPS):
        sl = slice(g * gw, (g + 1) * gw)
        yg = y[:, sl]
        yg = yg * lax.rsqrt(jnp.mean(yg * yg, axis=-1, keepdims=True) + EPS)
        y_ref[0, :, sl] = (yg * ng_ref[:, sl]).astype(BF16)
    cN_ref[0] = xpad[Tt:Tt + SUBLANES, :]


def _ssd(x, g_mix, w_ssd, c0, s0, cw, cb, dtb, alog, dvec, ng, Tt, q):
    B, L, _ = x.shape
    seq = lambda w: pl.BlockSpec((1, Tt, w), lambda b, t: (b, t, 0))
    per_b = pl.BlockSpec((1, SUBLANES, D_XBC), lambda b, t: (b, 0, 0))
    st = pl.BlockSpec((1, SSD_HEADS, SSD_HEADDIM, D_STATE), lambda b, t: (b, 0, 0, 0))
    masks = _ssd_masks(q)
    return pl.pallas_call(
        functools.partial(_ssd_body, Tt=Tt, q=q),
        grid=(B, L // Tt),
        in_specs=[seq(D_MODEL), _full((1, D_MODEL)), _full(w_ssd.shape),
                  per_b, st, _full((CONV_W, D_XBC)), _full((1, D_XBC)),
                  _full((1, LANES)), _full((1, LANES)), _full((1, D_SSM)), _full((1, D_SSM))]
                 + [_full(m.shape) for m in masks],
        out_specs=[seq(D_SSM), st, per_b],
        out_shape=[jax.ShapeDtypeStruct((B, L, D_SSM), BF16),
                   jax.ShapeDtypeStruct((B, SSD_HEADS, SSD_HEADDIM, D_STATE), F32),
                   jax.ShapeDtypeStruct((B, SUBLANES, D_XBC), F32)],
        scratch_shapes=[pltpu.VMEM((Tt + SUBLANES, D_XBC), F32), pltpu.VMEM((Tt, D_XBC), F32),
                        pltpu.VMEM((Tt, D_SSM), F32), pltpu.VMEM((D_STATE, D_SSM), F32),
                        pltpu.VMEM((Tt, LANES), F32), pltpu.VMEM((Tt, D_SSM), F32),
                        pltpu.VMEM((Tt // q, D_STATE, D_SSM), F32), pltpu.VMEM((Tt, D_SSM), F32),
                        pltpu.VMEM((Tt // q * SUBLANES, D_SSM), F32)],
        compiler_params=_params(2),
        name="ssd",
    )(x, g_mix, w_ssd, c0, s0, cw, cb, dtb, alog, dvec, ng, *masks)


def _out_router_body(x_ref, yl_ref, ys_ref, wo_ref, gf_ref, wr_ref, br_ref, cin_ref,
                     h1_ref, xn_ref, meta_ref, gate_ref, cnt_ref, carry, *, tm):
    step = pl.program_id(0)

    @pl.when(step == 0)
    def _():
        carry[...] = cin_ref[...]

    mix = jnp.concatenate([yl_ref[...], ys_ref[...]], axis=1)
    h1 = x_ref[...] + jnp.dot(mix, wo_ref[...], preferred_element_type=F32)
    h1_ref[...] = h1
    xn = _rmsnorm(h1, gf_ref[...])
    xn_ref[...] = xn

    xh = xn.astype(BF16)
    xm = (xn - xh.astype(F32)).astype(BF16)
    small = (jnp.dot(xh, wr_ref[1], preferred_element_type=F32)
             + jnp.dot(xm, wr_ref[0], preferred_element_type=F32))
    logits = small + jnp.dot(xh, wr_ref[0], preferred_element_type=F32) + br_ref[...]
    lt = logits.T
    rows8 = lax.broadcasted_iota(I32, (SUBLANES, tm), 0)
    lg = jnp.where(rows8 < MOE_GROUPS, lt[0:SUBLANES, :], -jnp.inf)
    eg = jnp.exp(lg - jnp.max(lg, axis=0, keepdims=True))
    pg = eg / jnp.sum(eg, axis=0, keepdims=True)
    pgs = jnp.max(pg, axis=0, keepdims=True)
    rows8f = rows8.astype(F32)
    gsel = jnp.min(jnp.where(pg == pgs, rows8f, float(SUBLANES)), axis=0, keepdims=True)

    rows32 = lax.broadcasted_iota(I32, (N_EXPERTS, tm), 0)
    rows32f = rows32.astype(F32)
    grp = (rows32 // EXPERTS_PER_GROUP).astype(F32)
    le = lt[ROUTER_E0:ROUTER_E0 + N_EXPERTS, :]
    ing = grp == gsel
    lem = jnp.where(ing, le, -jnp.inf)
    ee = jnp.exp(lem - jnp.max(lem, axis=0, keepdims=True))
    pe = ee / jnp.sum(ee, axis=0, keepdims=True)
    pe1 = jnp.where(ing, pe, -1.0)
    v1 = jnp.max(pe1, axis=0, keepdims=True)
    i1 = jnp.min(jnp.where(pe1 == v1, rows32f, float(N_EXPERTS)), axis=0, keepdims=True)
    pe2 = jnp.where(rows32f == i1, -1.0, pe1)
    v2 = jnp.max(pe2, axis=0, keepdims=True)
    i2 = jnp.min(jnp.where(pe2 == v2, rows32f, float(N_EXPERTS)), axis=0, keepdims=True)
    sv = v1 + v2
    w1 = v1 / sv * pgs
    w2 = v2 / sv * pgs

    oh1 = rows32f == i1
    oh2 = rows32f == i2
    oh = jnp.where(oh1 | oh2, 1.0, 0.0)
    before = (lax.broadcasted_iota(I32, (tm, tm), 0) < lax.broadcasted_iota(I32, (tm, tm), 1))
    pref = jnp.dot(oh.astype(BF16), jnp.where(before, 1.0, 0.0).astype(BF16), preferred_element_type=F32)
    pref = pref + carry[:, 0:1]
    r1 = jnp.sum(jnp.where(oh1, pref, 0.0), axis=0, keepdims=True)
    r2 = jnp.sum(jnp.where(oh2, pref, 0.0), axis=0, keepdims=True)
    carry[...] = carry[...] + jnp.sum(oh, axis=1, keepdims=True)
    cnt_ref[...] = carry[...]

    meta = jnp.where(rows8 == 0, i1, jnp.where(rows8 == 1, i2, jnp.where(rows8 == 2, r1, jnp.where(rows8 == 3, r2, 0.0))))
    meta_ref[...] = meta.astype(I32)
    gate_ref[...] = jnp.where(rows8 == 0, w1, jnp.where(rows8 == 1, w2, 0.0))


def _out_router(x2d, y_lru, y_ssd, w_out, gf, wr, br, counts_in, tm):
    T = x2d.shape[0]
    row = lambda w: pl.BlockSpec((tm, w), lambda i: (i, 0))
    col = pl.BlockSpec((SUBLANES, tm), lambda i: (0, i))
    return pl.pallas_call(
        functools.partial(_out_router_body, tm=tm),
        grid=(T // tm,),
        in_specs=[row(D_MODEL), row(D_LRU), row(D_SSM), _full((D_LRU + D_SSM, D_MODEL)), _full((1, D_MODEL)),
                  _full((2, D_MODEL, LANES)), _full((1, LANES)), _full((N_EXPERTS, LANES))],
        out_specs=[row(D_MODEL), row(D_MODEL), col, col, _full((N_EXPERTS, LANES))],
        out_shape=[jax.ShapeDtypeStruct((T, D_MODEL), F32), jax.ShapeDtypeStruct((T, D_MODEL), F32),
                   jax.ShapeDtypeStruct((SUBLANES, T), I32), jax.ShapeDtypeStruct((SUBLANES, T), F32),
                   jax.ShapeDtypeStruct((N_EXPERTS, LANES), F32)],
        scratch_shapes=[pltpu.VMEM((N_EXPERTS, LANES), F32)],
        compiler_params=_params(1),
        name="out_router",
    )(x2d, y_lru, y_ssd, w_out, gf, wr, br, counts_in)


def _row_copy(src_hbm, src_row, dst, dst_row, sem):
    return pltpu.make_async_copy(src_hbm.at[pl.ds(src_row, 1)], dst.at[pl.ds(dst_row, 1)], sem)


def _positions_body(offs_ref, meta_ref, pos_ref):
    m = meta_ref[...]
    base = jnp.zeros_like(m)
    for e in range(N_EXPERTS):
        base = jnp.where(m == e, offs_ref[e], base)
    pos_ref[...] = base + pltpu.roll(m, SUBLANES - 2, 0)


def _positions(offs, meta):
    T = meta.shape[1]
    tb = min(T, 4096)
    blk = pl.BlockSpec((SUBLANES, tb), lambda i, offs: (0, i))
    return pl.pallas_call(
        _positions_body,
        grid_spec=pltpu.PrefetchScalarGridSpec(num_scalar_prefetch=1, grid=(T // tb,), in_specs=[blk], out_specs=blk),
        out_shape=jax.ShapeDtypeStruct((SUBLANES, T), I32),
        compiler_params=_params(1),
        name="positions",
    )(offs, meta)


def _wait_rows(src, dst, sem):
    pltpu.make_async_copy(src, dst, sem).wait()


def _dispatch_body(zf_ref, pos_ref, *rest, tm, tme, steps):
    n_groups = len(steps)
    xn_hbms = rest[:n_groups]
    xs_hbm, zbuf, xbuf, zsem, lsem, ssem = rest[n_groups:]
    n_steps = sum(steps)
    n_tiles = xs_hbm.shape[0] // tme
    ring = xbuf.shape[0]
    step = pl.program_id(0)

    def zero_copy(i):
        return pltpu.make_async_copy(zbuf, xs_hbm.at[pl.ds(pl.multiple_of(i * tme, tme), tme)], zsem)

    def load(i, start):
        slot = i % ring
        first = 0
        for xn_hbm, n in zip(xn_hbms, steps):
            @pl.when(jnp.logical_and(i >= first, i < first + n))
            def _(xn_hbm=xn_hbm, first=first):
                rows = pl.ds(pl.multiple_of((i - first) * tm, tm), tm)
                cp = pltpu.make_async_copy(xn_hbm.at[rows], xbuf.at[slot], lsem.at[slot])
                if start:
                    cp.start()
                else:
                    cp.wait()
            first += n

    def wait_scatter(i):
        slot = i % ring
        for k in range(2):
            _wait_rows(xbuf.at[slot], xs_hbm.at[pl.ds(0, tm)], ssem.at[slot])

    @pl.when(step == 0)
    def _():
        zbuf[...] = jnp.zeros_like(zbuf)

        def start(i, c):
            @pl.when(zf_ref[i] != 0)
            def _():
                zero_copy(i).start()
            return c

        def wait(i, c):
            @pl.when(zf_ref[i] != 0)
            def _():
                zero_copy(i).wait()
            return c

        lax.fori_loop(0, n_tiles, start, 0)
        lax.fori_loop(0, n_tiles, wait, 0)
        load(step, True)

    @pl.when(step + 1 < n_steps)
    def _():
        @pl.when(step >= ring - 1)
        def _():
            wait_scatter(step - (ring - 1))
        load(step + 1, True)

    load(step, False)
    slot = step % ring
    for t in range(tm):
        for k in range(2):
            _row_copy(xbuf.at[slot], t, xs_hbm, pos_ref[k, t], ssem.at[slot]).start(priority=k)

    @pl.when(step == n_steps - 1)
    def _():
        for j in range(max(n_steps - ring, 0), n_steps):
            wait_scatter(j)


def _dispatch(zflag, pos, xns, n_rows, tm, tme):
    steps = tuple(xn.shape[0] // tm for xn in xns)
    ring = 3
    return pl.pallas_call(
        functools.partial(_dispatch_body, tm=tm, tme=tme, steps=steps),
        grid_spec=pltpu.PrefetchScalarGridSpec(
            num_scalar_prefetch=1,
            grid=(sum(steps),),
            in_specs=[pl.BlockSpec((SUBLANES, tm), lambda i, zf: (0, i), memory_space=pltpu.SMEM)]
                     + [pl.BlockSpec(memory_space=pl.ANY) for _ in xns],
            out_specs=pl.BlockSpec(memory_space=pl.ANY),
            scratch_shapes=[pltpu.VMEM((tme, D_MODEL), F32), pltpu.VMEM((ring, tm, D_MODEL), F32),
                            pltpu.SemaphoreType.DMA(()), pltpu.SemaphoreType.DMA((ring,)),
                            pltpu.SemaphoreType.DMA((ring,))],
        ),
        out_shape=jax.ShapeDtypeStruct((n_rows, D_MODEL), F32),
        compiler_params=_params(1),
        name="dispatch",
    )(zflag, pos, *xns)


def _experts_body(te_ref, na_ref, x_ref, w1_ref, w3_ref, w2_ref, o_ref, w1b, w3b, w2b):
    i = pl.program_id(0)
    changed = jnp.logical_or(i == 0, te_ref[i] != te_ref[jnp.maximum(i - 1, 0)])

    @pl.when(changed)
    def _():
        w1b[...] = w1_ref[0].astype(BF16)
        w3b[...] = w3_ref[0].astype(BF16)
        w2b[...] = w2_ref[0].astype(BF16)

    @pl.when(i < na_ref[0])
    def _():
        xb = x_ref[...].astype(BF16)
        a = jnp.dot(xb, w1b[...], preferred_element_type=F32)
        b = jnp.dot(xb, w3b[...], preferred_element_type=F32)
        hd = (jax.nn.silu(a) * b).astype(BF16)
        o_ref[...] = jnp.dot(hd, w2b[...], preferred_element_type=F32)

    @pl.when(i >= na_ref[0])
    def _():
        o_ref[...] = jnp.zeros_like(o_ref)


def _experts(tile_e, n_active, xs, w1, w3, w2, tme):
    n_rows = xs.shape[0]
    row = pl.BlockSpec((tme, D_MODEL), lambda i, te, na: (i, 0))
    return pl.pallas_call(
        _experts_body,
        grid_spec=pltpu.PrefetchScalarGridSpec(
            num_scalar_prefetch=2,
            grid=(n_rows // tme,),
            in_specs=[row,
                      pl.BlockSpec((1, D_MODEL, D_EXPERT), lambda i, te, na: (te[i], 0, 0)),
                      pl.BlockSpec((1, D_MODEL, D_EXPERT), lambda i, te, na: (te[i], 0, 0)),
                      pl.BlockSpec((1, D_EXPERT, D_MODEL), lambda i, te, na: (te[i], 0, 0))],
            out_specs=row,
            scratch_shapes=[pltpu.VMEM((D_MODEL, D_EXPERT), BF16), pltpu.VMEM((D_MODEL, D_EXPERT), BF16),
                            pltpu.VMEM((D_EXPERT, D_MODEL), BF16)],
        ),
        out_shape=jax.ShapeDtypeStruct((n_rows, D_MODEL), F32),
        compiler_params=_params(1),
        name="experts",
    )(tile_e, n_active, xs, w1, w3, w2)


def _combine_body(pos_ref, nxt_ref, gate_ref, h1_ref, fg_ref, ys_hbm, y_ref, ybuf, sem, *, tm):
    step = pl.program_id(0)
    slot = step % 2

    def gather(p_ref, s):
        for t in range(tm):
            for k in range(2):
                _row_copy(ys_hbm, p_ref[k, t], ybuf.at[s, k], t, sem.at[s]).start(priority=k)

    @pl.when(step == 0)
    def _():
        gather(pos_ref, 0)

    @pl.when(step + 1 < pl.num_programs(0))
    def _():
        gather(nxt_ref, 1 - slot)

    for k in range(2):
        _wait_rows(ys_hbm.at[pl.ds(0, tm)], ybuf.at[slot, k], sem.at[slot])

    eye = lax.broadcasted_iota(I32, (tm, tm), 0) == lax.broadcasted_iota(I32, (tm, tm), 1)
    g1 = jnp.sum(jnp.where(eye, gate_ref[0:1, :], 0.0), axis=1, keepdims=True)
    g2 = jnp.sum(jnp.where(eye, gate_ref[1:2, :], 0.0), axis=1, keepdims=True)
    h2 = h1_ref[...] + (g1 * ybuf[slot, 0] + g2 * ybuf[slot, 1])
    y_ref[...] = _rmsnorm(h2, fg_ref[...])


def _combine(pos, gates, h1, fg, ys, tm):
    T = h1.shape[0]
    last = T // tm - 1
    return pl.pallas_call(
        functools.partial(_combine_body, tm=tm),
        grid=(T // tm,),
        in_specs=[pl.BlockSpec((SUBLANES, tm), lambda i: (0, i), memory_space=pltpu.SMEM),
                  pl.BlockSpec((SUBLANES, tm), lambda i: (0, jnp.minimum(i + 1, last)), memory_space=pltpu.SMEM),
                  pl.BlockSpec((SUBLANES, tm), lambda i: (0, i)),
                  pl.BlockSpec((tm, D_MODEL), lambda i: (i, 0)),
                  pl.BlockSpec((1, D_MODEL), lambda i: (0, 0)),
                  pl.BlockSpec(memory_space=pl.ANY)],
        out_specs=pl.BlockSpec((tm, D_MODEL), lambda i: (i, 0)),
        scratch_shapes=[pltpu.VMEM((2, 2, tm, D_MODEL), F32), pltpu.SemaphoreType.DMA((2,))],
        out_shape=jax.ShapeDtypeStruct((T, D_MODEL), F32),
        compiler_params=_params(1),
        name="combine",
    )(pos, pos, gates, h1, fg, ys)


def _blockdiag(w):
    per = GATE_TILE // LRU_BLOCK
    w4 = w.reshape(LRU_BLOCKS // per, per, LRU_BLOCK, LRU_BLOCK)
    eye = jnp.eye(per, dtype=w.dtype)
    return jnp.einsum('jbio,bc->jbico', w4, eye).reshape(LRU_BLOCKS // per, GATE_TILE, GATE_TILE)


def _pad_rows(c):
    return jnp.pad(c, ((0, 0), (SUBLANES - (CONV_W - 1), 0), (0, 0)))


def _router_cols(group_part, expert_part):
    r = group_part.shape[0]
    out = jnp.zeros((r, LANES), F32)
    out = out.at[:, 0:MOE_GROUPS].set(group_part)
    return out.at[:, ROUTER_E0:ROUTER_E0 + N_EXPERTS].set(expert_part)


def _lane_row(v, width=LANES):
    return jnp.pad(v, (0, width - v.shape[0])).reshape(1, width)


def _prep(norm_mix_g, w_in, lru_conv_w, lru_conv_b, lru_wa, lru_ba, lru_wx, lru_bx, lru_lambda,
          ssd_conv_w, ssd_conv_b, ssd_dt_bias, ssd_a_log, ssd_d, ssd_norm_g, w_out,
          norm_ffn_g, router_group_w, router_group_b, router_expert_w, router_expert_b,
          moe_w1, moe_w3, moe_w2, final_norm_g):
    w = w_in[0]
    wr = _router_cols(router_group_w[0], router_expert_w[0])
    wr_hi = wr.astype(BF16)
    P = dict(
        g_mix=norm_mix_g[0].reshape(1, D_MODEL),
        w_lru=w[:, :2 * D_LRU].astype(BF16),
        w_ssd=jnp.pad(w[:, 2 * D_LRU:], ((0, 0), (0, LANES - SSD_HEADS))).astype(BF16),
        lru_cw=lru_conv_w[0], lru_cb=lru_conv_b[0].reshape(1, D_LRU),
        wbd=jnp.concatenate([_blockdiag(lru_wa[0]), _blockdiag(lru_wx[0])], axis=2).astype(BF16),
        ba=lru_ba[0].reshape(1, D_LRU), bx=lru_bx[0].reshape(1, D_LRU), lam=lru_lambda[0].reshape(1, D_LRU),
        ssd_cw=ssd_conv_w[0], ssd_cb=ssd_conv_b[0].reshape(1, D_XBC),
        dtb=_lane_row(ssd_dt_bias[0]), alog=_lane_row(ssd_a_log[0]),
        dvec=jnp.repeat(ssd_d[0], SSD_HEADDIM).reshape(1, D_SSM),
        ng=ssd_norm_g[0].reshape(1, D_SSM),
        w_out=w_out[0].astype(BF16),
        g_ffn=norm_ffn_g[0].reshape(1, D_MODEL),
        wr=jnp.stack([wr_hi, (wr - wr_hi.astype(F32)).astype(BF16)]),
        br=_router_cols(router_group_b[0][None], router_expert_b[0][None]),
        w1=moe_w1[0], w3=moe_w3[0], w2=moe_w2[0],
        g_final=final_norm_g.reshape(1, D_MODEL),
    )
    return P


def _expert_layout(counts, n_pairs, tme):
    cnt = counts[:, 0].astype(I32)
    padded = ((cnt + tme - 1) // tme) * tme
    ends = jnp.cumsum(padded)
    offs = ends - padded
    n_tiles = n_pairs // tme + N_EXPERTS
    n_active = ends[-1] // tme
    tiles = jnp.arange(n_tiles, dtype=I32)
    tile_e = jnp.sum((tiles * tme)[:, None] >= ends[None, :], axis=1).astype(I32)
    last_e = jnp.sum((n_active - 1) * tme >= ends).astype(I32)
    tile_e = jnp.where(tiles < n_active, tile_e, last_e)
    is_last = jnp.any((tiles[:, None] + 1) * tme == ends[None, :], axis=1)
    zflag = jnp.logical_or(is_last, tiles >= n_active).astype(I32)
    return offs.astype(I32), tile_e, n_active.reshape(1).astype(I32), zflag, n_tiles * tme


def _mixer_router(x, lru_h0, lru_c0, ssd_h0, ssd_c0, P, start_pos, counts_in):
    B, L, _ = x.shape
    T = B * L
    Tt = min(MIX_TILE, L)
    q = min(SSD_CHUNK, L)
    tm = min(ROW_TILE, T)
    x2d = x.reshape(T, D_MODEL)

    y_lru, lru_h, lru_c = _lru(x, P['g_mix'], P['w_lru'], _pad_rows(lru_c0), lru_h0.reshape(B, 1, D_LRU),
                               P['lru_cw'], P['lru_cb'], P['wbd'], P['ba'], P['bx'], P['lam'], Tt, start_pos)
    y_ssd, ssd_h, ssd_c = _ssd(x, P['g_mix'], P['w_ssd'], _pad_rows(ssd_c0), ssd_h0,
                               P['ssd_cw'], P['ssd_cb'], P['dtb'], P['alog'], P['dvec'], P['ng'], Tt, q)
    h1, xn, meta, gates, counts = _out_router(
        x2d, y_lru.reshape(T, D_LRU), y_ssd.reshape(T, D_SSM), P['w_out'], P['g_ffn'],
        P['wr'], P['br'], counts_in, min(MIX_TILE, T))

    hist = SUBLANES - (CONV_W - 1)
    states = (lru_h.reshape(1, B, D_LRU), lru_c[:, hist:][None], ssd_h[None], ssd_c[:, hist:][None])
    return dict(h1=h1, xn=xn, meta=meta, gates=gates, shape=(B, L, D_MODEL), tm=tm), counts, states


def _moe_final(groups, counts, P):
    tme = EXPERT_TILE
    n_pairs = 2 * sum(g['h1'].shape[0] for g in groups)
    offs, tile_e, n_active, zflag, n_rows = _expert_layout(counts, n_pairs, tme)
    tm = groups[0]['tm']
    assert all(g['tm'] == tm for g in groups)
    for g in groups:
        g['pos'] = _positions(offs, g['meta'])
    pos_all = jnp.concatenate([g['pos'] for g in groups], axis=1)
    xs = _dispatch(zflag, pos_all, [g['xn'] for g in groups], n_rows, tm, tme)
    ys = _experts(tile_e, n_active, xs, P['w1'], P['w3'], P['w2'], tme)
    return [_combine(g['pos'], g['gates'], g['h1'], P['g_final'], ys, g['tm']).reshape(g['shape'])
            for g in groups]


def kernel(x_prompt, x_sample, state_lru_h, state_lru_conv, state_ssd, state_ssd_conv, norm_mix_g, w_in, lru_conv_w, lru_conv_b, lru_wa, lru_ba, lru_wx, lru_bx, lru_lambda, ssd_conv_w, ssd_conv_b, ssd_dt_bias, ssd_a_log, ssd_d, ssd_norm_g, w_out, norm_ffn_g, router_group_w, router_group_b, router_expert_w, router_expert_b, moe_w1, moe_w3, moe_w2, final_norm_g):
    P = _prep(norm_mix_g, w_in, lru_conv_w, lru_conv_b, lru_wa, lru_ba, lru_wx, lru_bx, lru_lambda,
              ssd_conv_w, ssd_conv_b, ssd_dt_bias, ssd_a_log, ssd_d, ssd_norm_g, w_out,
              norm_ffn_g, router_group_w, router_group_b, router_expert_w, router_expert_b,
              moe_w1, moe_w3, moe_w2, final_norm_g)
    bp = x_prompt.shape[0]
    gp, counts, (a1, a2, a3, a4) = _mixer_router(
        x_prompt,
        jnp.zeros((bp, D_LRU), F32), jnp.zeros((bp, CONV_W - 1, D_LRU), F32),
        jnp.zeros((bp, SSD_HEADS, SSD_HEADDIM, D_STATE), F32), jnp.zeros((bp, CONV_W - 1, D_XBC), F32),
        P, 0, jnp.zeros((N_EXPERTS, LANES), F32))
    gs, counts, (b1, b2, b3, b4) = _mixer_router(
        x_sample, state_lru_h[0], state_lru_conv[0], state_ssd[0], state_ssd_conv[0], P, PAST_LEN, counts)
    yp, ys = _moe_final([gp, gs], counts, P)
    return (yp, ys, a1, a2, a3, a4, b1, b2, b3, b4)
```

```python
import functools

import jax
import jax.numpy as jnp
from jax import lax
from jax.experimental import pallas as pl
from jax.experimental.pallas import tpu as pltpu

F32 = jnp.float32
BF16 = jnp.bfloat16
I32 = jnp.int32

D_MODEL = 1024
D_LRU = 1024
LRU_BLOCKS = 16
LRU_BLOCK = 64
LRU_C = 8.0
CONV_W = 4
D_SSM = 1024
SSD_HEADDIM = 64
SSD_HEADS = 16
SSD_GROUPS = 2
SSD_HPG = 8
D_STATE = 128
D_XBC = 1536
MOE_GROUPS = 4
EXPERTS_PER_GROUP = 8
N_EXPERTS = 32
D_EXPERT = 512
EPS = 1e-6
SSD_CHUNK = 64
PAST_LEN = 1024

LANES = 128
SUBLANES = 8
GATE_TILE = 256
ROW_TILE = 256
MIX_TILE = 512
EXPERT_TILE = 512
ROUTER_E0 = 32
VMEM_LIMIT = 52 * 1024 * 1024

_NT = (((1,), (1,)), ((), ()))
_TN = (((0,), (0,)), ((), ()))


def _params(n_axes):
    return pltpu.CompilerParams(dimension_semantics=("arbitrary",) * n_axes,
                                vmem_limit_bytes=VMEM_LIMIT)


def _rmsnorm(x, g):
    return x * lax.rsqrt(jnp.mean(x * x, axis=-1, keepdims=True) + EPS) * g


def _full(shape):
    n = len(shape)
    return pl.BlockSpec(shape, lambda *_: (0,) * n)


def _project(xb, w_ref, lo, hi):
    return jnp.dot(xb, w_ref[:, lo:hi], preferred_element_type=F32)


def _conv_block(xpad, cw_ref, cb_ref, Tt, sl):
    cw = cw_ref[:, sl]
    full = xpad[:, sl]
    y = cb_ref[:, sl]
    for k in range(CONV_W):
        shift = CONV_W - 1 - k
        xk = pltpu.roll(full, shift, 0) if shift else full
        y = y + xk[SUBLANES:SUBLANES + Tt, :] * cw[k:k + 1, :]
    return y


def _carry_history(t, xpad, c0_ref, Tt):
    @pl.when(t == 0)
    def _():
        xpad[0:SUBLANES, :] = c0_ref[0]

    @pl.when(t > 0)
    def _():
        xpad[0:SUBLANES, :] = xpad[Tt:Tt + SUBLANES, :]


def _lru_body(x_ref, gm_ref, w_ref, c0_ref, h0_ref, cw_ref, cb_ref, wbd_ref, ba_ref, bx_ref, lam_ref,
              y_ref, hN_ref, cN_ref, xpad, a_s, u_s, gel_s, hcar, *, Tt, start_pos):
    t = pl.program_id(1)
    _carry_history(t, xpad, c0_ref, Tt)

    @pl.when(t == 0)
    def _():
        hcar[...] = jnp.broadcast_to(h0_ref[0], (SUBLANES, D_LRU))

    xb = _rmsnorm(x_ref[0], gm_ref[...]).astype(BF16)
    sp = jax.nn.softplus(-lam_ref[...])
    pos0 = (lax.broadcasted_iota(I32, (Tt, 1), 0) + t * Tt + start_pos) == 0
    for j in range(D_LRU // GATE_TILE):
        sl = slice(GATE_TILE * j, GATE_TILE * (j + 1))
        xpad[SUBLANES:SUBLANES + Tt, sl] = _project(xb, w_ref, sl.start, sl.stop)
        xc = _conv_block(xpad, cw_ref, cb_ref, Tt, sl)
        ga = jnp.dot(xc.astype(BF16), wbd_ref[j], preferred_element_type=F32)
        r = jax.nn.sigmoid(ga[:, :GATE_TILE] + ba_ref[:, sl])
        i = jax.nn.sigmoid(ga[:, GATE_TILE:] + bx_ref[:, sl])
        a = jnp.exp((-LRU_C * r) * sp[:, sl])
        mult = jnp.where(pos0, 1.0, jnp.sqrt(1.0 - a * a))
        a_s[:, sl] = a
        u_s[:, sl] = mult * i * xc
        g = _project(xb, w_ref, D_LRU + sl.start, D_LRU + sl.stop)
        gel_s[:, sl] = jax.nn.gelu(g, approximate=True)

    rows = lax.broadcasted_iota(I32, (SUBLANES, D_LRU), 0)

    def scan8(gi, hprev):
        r0 = pl.multiple_of(gi * SUBLANES, SUBLANES)
        a8 = a_s[pl.ds(r0, SUBLANES), :]
        u8 = u_s[pl.ds(r0, SUBLANES), :]
        for s in (1, 2, 4):
            ok = rows >= s
            u_sh = pltpu.roll(u8, s, 0)
            a_sh = pltpu.roll(a8, s, 0)
            u8 = jnp.where(ok, u8 + a8 * u_sh, u8)
            a8 = jnp.where(ok, a8 * a_sh, a8)
        h8 = u8 + a8 * hprev
        u_s[pl.ds(r0, SUBLANES), :] = h8
        return jnp.broadcast_to(h8[SUBLANES - 1:SUBLANES, :], (SUBLANES, D_LRU))

    hlast = lax.fori_loop(0, Tt // SUBLANES, scan8, hcar[...])
    hcar[...] = hlast
    y_ref[0] = (u_s[...] * gel_s[...]).astype(BF16)
    hN_ref[0] = hlast[0:1, :]
    cN_ref[0] = xpad[Tt:Tt + SUBLANES, :]


def _lru(x, g_mix, w_lru, c0, h0, cw, cb, wbd, ba, bx, lam, Tt, start_pos):
    B, L, _ = x.shape
    seq = pl.BlockSpec((1, Tt, D_LRU), lambda b, t: (b, t, 0))
    per_b = lambda r: pl.BlockSpec((1, r, D_LRU), lambda b, t: (b, 0, 0))
    return pl.pallas_call(
        functools.partial(_lru_body, Tt=Tt, start_pos=start_pos),
        grid=(B, L // Tt),
        in_specs=[pl.BlockSpec((1, Tt, D_MODEL), lambda b, t: (b, t, 0)), _full((1, D_MODEL)), _full(w_lru.shape),
                  per_b(SUBLANES), per_b(1), _full((CONV_W, D_LRU)), _full((1, D_LRU)),
                  _full(wbd.shape), _full((1, D_LRU)), _full((1, D_LRU)), _full((1, D_LRU))],
        out_specs=[seq, per_b(1), per_b(SUBLANES)],
        out_shape=[jax.ShapeDtypeStruct((B, L, D_LRU), BF16),
                   jax.ShapeDtypeStruct((B, 1, D_LRU), F32),
                   jax.ShapeDtypeStruct((B, SUBLANES, D_LRU), F32)],
        scratch_shapes=[pltpu.VMEM((Tt + SUBLANES, D_LRU), F32), pltpu.VMEM((Tt, D_LRU), F32),
                        pltpu.VMEM((Tt, D_LRU), F32), pltpu.VMEM((Tt, D_LRU), F32),
                        pltpu.VMEM((SUBLANES, D_LRU), F32)],
        compiler_params=_params(2),
        name="lru",
    )(x, g_mix, w_lru, c0, h0, cw, cb, wbd, ba, bx, lam)


def _split3(v):
    hi = v.astype(BF16)
    r1 = v - hi.astype(F32)
    mid = r1.astype(BF16)
    lo = (r1 - mid.astype(F32)).astype(BF16)
    return hi, mid, lo


def _pad_time(v, rows):
    if v.shape[0] == rows:
        return v
    return jnp.concatenate([v, jnp.zeros((rows - v.shape[0], v.shape[1]), v.dtype)], axis=0)


def _ssd_masks(q):
    P = SSD_HEADDIM
    tri = jnp.arange(q)[:, None] >= jnp.arange(q)[None, :]
    expand = jnp.arange(LANES)[:, None] == jnp.arange(D_SSM)[None, :] // P
    row_q = jnp.arange(q)[:, None]
    lane_k = jnp.arange(D_SSM)[None, :] % P
    diag = row_q == lane_k
    causal = row_q >= lane_k
    bd = jnp.arange(GATE_TILE)[:, None] // P == jnp.arange(GATE_TILE)[None, :] // P
    return (tri.astype(BF16), expand.astype(BF16), diag.astype(F32), causal.astype(F32), bd.astype(BF16))


def _ssd_body(x_ref, gm_ref, w_ref, c0_ref, s0_ref, cw_ref, cb_ref, dtb_ref, alog_ref, dvec_ref, ng_ref,
              tri_ref, expand_ref, diag_ref, causal_ref, bd_ref,
              y_ref, sN_ref, cN_ref, xpad, xa_s, y_s, st_s, dt_s, zs_s, stn_s, ecs_s, cdec_s, *, Tt, q):
    t = pl.program_id(1)
    _carry_history(t, xpad, c0_ref, Tt)

    @pl.when(t == 0)
    def _():
        st_s[...] = s0_ref[0].reshape(D_SSM, D_STATE).T

    xb = _rmsnorm(x_ref[0], gm_ref[...]).astype(BF16)
    for j in range(D_XBC // GATE_TILE):
        sl = slice(GATE_TILE * j, GATE_TILE * (j + 1))
        xpad[SUBLANES:SUBLANES + Tt, sl] = _project(xb, w_ref, D_SSM + sl.start, D_SSM + sl.stop)
        xa_s[:, sl] = jax.nn.silu(_conv_block(xpad, cw_ref, cb_ref, Tt, sl))
    for j in range(D_SSM // GATE_TILE):
        sl = slice(GATE_TILE * j, GATE_TILE * (j + 1))
        zs_s[:, sl] = jax.nn.silu(_project(xb, w_ref, sl.start, sl.stop))
    dt_s[...] = _project(xb, w_ref, D_SSM + D_XBC, D_SSM + D_XBC + LANES)
    A = -jnp.exp(alog_ref[...])
    P = SSD_HEADDIM
    blk = GATE_TILE // P
    off_b = D_SSM
    off_c = D_SSM + SSD_GROUPS * D_STATE
    gw = D_SSM // SSD_GROUPS

    def exact01(parts, w01, left):
        one = (lambda p: jnp.dot(w01, p, preferred_element_type=F32)) if left else (
            lambda p: jnp.dot(p, w01, preferred_element_type=F32))
        hi, mid, lo = parts
        return (one(lo) + one(mid)) + one(hi)

    n_chunks = Tt // q
    nb = min(8, n_chunks)

    def within_chunks(cb, carry):
        cidx = [cb * nb + i for i in range(nb)]
        r0s = [pl.multiple_of(c * q, q) for c in cidx]
        xs = [xa_s[pl.ds(r0, q), 0:D_SSM] for r0 in r0s]
        dts = [jax.nn.softplus(dt_s[pl.ds(r0, q), :] + dtb_ref[...]) for r0 in r0s]
        css = [exact01(_split3(dt * A), tri_ref[...], left=True) for dt in dts]
        Es = [exact01(_split3(jnp.concatenate([cs, dt], axis=0)), expand_ref[...], left=False)
              for cs, dt in zip(css, dts)]
        diag = diag_ref[...] != 0.0
        causal = causal_ref[...] != 0.0
        Bgs = [[xa_s[pl.ds(r0, q), off_b + g * D_STATE:off_b + (g + 1) * D_STATE].astype(BF16)
                for g in range(SSD_GROUPS)] for r0 in r0s]
        Cgs = [[xa_s[pl.ds(r0, q), off_c + g * D_STATE:off_c + (g + 1) * D_STATE].astype(BF16)
                for g in range(SSD_GROUPS)] for r0 in r0s]
        CBs = [jnp.concatenate(
            [lax.dot_general(Cg, jnp.concatenate([_pad_time(Bg, P)] * SSD_HPG, axis=0), _NT,
                             preferred_element_type=F32) for Bg, Cg in zip(Bgc, Cgc)], axis=1)
            for Bgc, Cgc in zip(Bgs, Cgs)]
        xws = []
        for i in range(nb):
            E_cs, E_dt = Es[i][0:q], Es[i][q:2 * q]
            cs_last = E_cs[q - 1:q, :]
            xws.append((jnp.exp(cs_last - E_cs) * E_dt * xs[i]).astype(BF16))
            ecs_s[pl.ds(r0s[i], q), :] = jnp.exp(E_cs)
            cdec_s[pl.ds(pl.multiple_of(cidx[i] * SUBLANES, SUBLANES), SUBLANES), :] = jnp.broadcast_to(
                jnp.exp(cs_last), (SUBLANES, D_SSM))
        for i in range(nb):
            for g in range(SSD_GROUPS):
                sl = slice(g * gw, (g + 1) * gw)
                stn_s[cidx[i], :, sl] = lax.dot_general(Bgs[i][g], xws[i][:, sl], _TN, preferred_element_type=F32)
        Mws = []
        for i in range(nb):
            E_cs, E_dt = Es[i][0:q], Es[i][q:2 * q]
            r_cs = jnp.sum(jnp.where(diag, E_cs, 0.0), axis=0, keepdims=True)
            r_dt = jnp.sum(jnp.where(diag, E_dt, 0.0), axis=0, keepdims=True)
            Lm = jnp.where(causal, jnp.exp(jnp.where(causal, E_cs - r_cs, 0.0)), 0.0)
            Mws.append((CBs[i] * Lm * r_dt).astype(BF16))
        for i in range(nb):
            xsb = xs[i].astype(BF16)
            for j in range(D_SSM // GATE_TILE):
                sl = slice(j * GATE_TILE, (j + 1) * GATE_TILE)
                slab = _pad_time(xsb[:, sl], P)
                rhs = jnp.concatenate([slab] * blk, axis=0) * bd_ref[...]
                y_s[pl.ds(r0s[i], q), sl] = jnp.dot(Mws[i][:, sl], rhs, preferred_element_type=F32)
        return carry

    def across_chunks(c, carry):
        r0 = pl.multiple_of(c * q, q)
        cdec = cdec_s[pl.ds(pl.multiple_of(c * SUBLANES, SUBLANES), 1), :]
        for g in range(SSD_GROUPS):
            sl = slice(g * gw, (g + 1) * gw)
            Cg = xa_s[pl.ds(r0, q), off_c + g * D_STATE:off_c + (g + 1) * D_STATE].astype(BF16)
            S = st_s[:, sl]
            yo = jnp.dot(Cg, S.astype(BF16), preferred_element_type=F32)
            y_s[pl.ds(r0, q), sl] = y_s[pl.ds(r0, q), sl] + yo * ecs_s[pl.ds(r0, q), sl]
            st_s[:, sl] = cdec[:, sl] * S + stn_s[c, :, sl]
        return carry

    lax.fori_loop(0, n_chunks // nb, within_chunks, 0)
    lax.fori_loop(0, n_chunks, across_chunks, 0, unroll=True)

    @pl.when(t == pl.num_programs(1) - 1)
    def _():
        sN_ref[0] = st_s[...].T.reshape(SSD_HEADS, SSD_HEADDIM, D_STATE)

    y = y_s[...] + dvec_ref[...] * xa_s[:, 0:D_SSM]
    y = y * zs_s[...]
    gw = D_SSM // SSD_GROUPS
    for g in range(SSD_GROUPS):
        sl = slice(g * gw, (g + 1) * gw)
        yg = y[:, sl]
        yg = yg * lax.rsqrt(jnp.mean(yg * yg, axis=-1, keepdims=True) + EPS)
        y_ref[0, :, sl] = (yg * ng_ref[:, sl]).astype(BF16)
    cN_ref[0] = xpad[Tt:Tt + SUBLANES, :]


def _ssd(x, g_mix, w_ssd, c0, s0, cw, cb, dtb, alog, dvec, ng, Tt, q):
    B, L, _ = x.shape
    seq = lambda w: pl.BlockSpec((1, Tt, w), lambda b, t: (b, t, 0))
    per_b = pl.BlockSpec((1, SUBLANES, D_XBC), lambda b, t: (b, 0, 0))
    st = pl.BlockSpec((1, SSD_HEADS, SSD_HEADDIM, D_STATE), lambda b, t: (b, 0, 0, 0))
    masks = _ssd_masks(q)
    return pl.pallas_call(
        functools.partial(_ssd_body, Tt=Tt, q=q),
        grid=(B, L // Tt),
        in_specs=[seq(D_MODEL), _full((1, D_MODEL)), _full(w_ssd.shape),
                  per_b, st, _full((CONV_W, D_XBC)), _full((1, D_XBC)),
                  _full((1, LANES)), _full((1, LANES)), _full((1, D_SSM)), _full((1, D_SSM))]
                 + [_full(m.shape) for m in masks],
        out_specs=[seq(D_SSM), st, per_b],
        out_shape=[jax.ShapeDtypeStruct((B, L, D_SSM), BF16),
                   jax.ShapeDtypeStruct((B, SSD_HEADS, SSD_HEADDIM, D_STATE), F32),
                   jax.ShapeDtypeStruct((B, SUBLANES, D_XBC), F32)],
        scratch_shapes=[pltpu.VMEM((Tt + SUBLANES, D_XBC), F32), pltpu.VMEM((Tt, D_XBC), F32),
                        pltpu.VMEM((Tt, D_SSM), F32), pltpu.VMEM((D_STATE, D_SSM), F32),
                        pltpu.VMEM((Tt, LANES), F32), pltpu.VMEM((Tt, D_SSM), F32),
                        pltpu.VMEM((Tt // q, D_STATE, D_SSM), F32), pltpu.VMEM((Tt, D_SSM), F32),
                        pltpu.VMEM((Tt // q * SUBLANES, D_SSM), F32)],
        compiler_params=_params(2),
        name="ssd",
    )(x, g_mix, w_ssd, c0, s0, cw, cb, dtb, alog, dvec, ng, *masks)


def _out_router_body(x_ref, yl_ref, ys_ref, wo_ref, gf_ref, wr_ref, br_ref, cin_ref,
                     h1_ref, xn_ref, meta_ref, gate_ref, cnt_ref, carry, *, tm):
    step = pl.program_id(0)

    @pl.when(step == 0)
    def _():
        carry[...] = cin_ref[...]

    mix = jnp.concatenate([yl_ref[...], ys_ref[...]], axis=1)
    h1 = x_ref[...] + jnp.dot(mix, wo_ref[...], preferred_element_type=F32)
    h1_ref[...] = h1
    xn = _rmsnorm(h1, gf_ref[...])
    xn_ref[...] = xn

    xh = xn.astype(BF16)
    xm = (xn - xh.astype(F32)).astype(BF16)
    small = (jnp.dot(xh, wr_ref[1], preferred_element_type=F32)
             + jnp.dot(xm, wr_ref[0], preferred_element_type=F32))
    logits = small + jnp.dot(xh, wr_ref[0], preferred_element_type=F32) + br_ref[...]
    lt = logits.T
    rows8 = lax.broadcasted_iota(I32, (SUBLANES, tm), 0)
    lg = jnp.where(rows8 < MOE_GROUPS, lt[0:SUBLANES, :], -jnp.inf)
    eg = jnp.exp(lg - jnp.max(lg, axis=0, keepdims=True))
    pg = eg / jnp.sum(eg, axis=0, keepdims=True)
    pgs = jnp.max(pg, axis=0, keepdims=True)
    rows8f = rows8.astype(F32)
    gsel = jnp.min(jnp.where(pg == pgs, rows8f, float(SUBLANES)), axis=0, keepdims=True)

    rows32 = lax.broadcasted_iota(I32, (N_EXPERTS, tm), 0)
    rows32f = rows32.astype(F32)
    grp = (rows32 // EXPERTS_PER_GROUP).astype(F32)
    le = lt[ROUTER_E0:ROUTER_E0 + N_EXPERTS, :]
    ing = grp == gsel
    lem = jnp.where(ing, le, -jnp.inf)
    ee = jnp.exp(lem - jnp.max(lem, axis=0, keepdims=True))
    pe = ee / jnp.sum(ee, axis=0, keepdims=True)
    pe1 = jnp.where(ing, pe, -1.0)
    v1 = jnp.max(pe1, axis=0, keepdims=True)
    i1 = jnp.min(jnp.where(pe1 == v1, rows32f, float(N_EXPERTS)), axis=0, keepdims=True)
    pe2 = jnp.where(rows32f == i1, -1.0, pe1)
    v2 = jnp.max(pe2, axis=0, keepdims=True)
    i2 = jnp.min(jnp.where(pe2 == v2, rows32f, float(N_EXPERTS)), axis=0, keepdims=True)
    sv = v1 + v2
    w1 = v1 / sv * pgs
    w2 = v2 / sv * pgs

    oh1 = rows32f == i1
    oh2 = rows32f == i2
    oh = jnp.where(oh1 | oh2, 1.0, 0.0)
    before = (lax.broadcasted_iota(I32, (tm, tm), 0) < lax.broadcasted_iota(I32, (tm, tm), 1))
    pref = jnp.dot(oh.astype(BF16), jnp.where(before, 1.0, 0.0).astype(BF16), preferred_element_type=F32)
    pref = pref + carry[:, 0:1]
    r1 = jnp.sum(jnp.where(oh1, pref, 0.0), axis=0, keepdims=True)
    r2 = jnp.sum(jnp.where(oh2, pref, 0.0), axis=0, keepdims=True)
    carry[...] = carry[...] + jnp.sum(oh, axis=1, keepdims=True)
    cnt_ref[...] = carry[...]

    meta = jnp.where(rows8 == 0, i1, jnp.where(rows8 == 1, i2, jnp.where(rows8 == 2, r1, jnp.where(rows8 == 3, r2, 0.0))))
    meta_ref[...] = meta.astype(I32)
    gate_ref[...] = jnp.where(rows8 == 0, w1, jnp.where(rows8 == 1, w2, 0.0))


def _out_router(x2d, y_lru, y_ssd, w_out, gf, wr, br, counts_in, tm):
    T = x2d.shape[0]
    row = lambda w: pl.BlockSpec((tm, w), lambda i: (i, 0))
    col = pl.BlockSpec((SUBLANES, tm), lambda i: (0, i))
    return pl.pallas_call(
        functools.partial(_out_router_body, tm=tm),
        grid=(T // tm,),
        in_specs=[row(D_MODEL), row(D_LRU), row(D_SSM), _full((D_LRU + D_SSM, D_MODEL)), _full((1, D_MODEL)),
                  _full((2, D_MODEL, LANES)), _full((1, LANES)), _full((N_EXPERTS, LANES))],
        out_specs=[row(D_MODEL), row(D_MODEL), col, col, _full((N_EXPERTS, LANES))],
        out_shape=[jax.ShapeDtypeStruct((T, D_MODEL), F32), jax.ShapeDtypeStruct((T, D_MODEL), F32),
                   jax.ShapeDtypeStruct((SUBLANES, T), I32), jax.ShapeDtypeStruct((SUBLANES, T), F32),
                   jax.ShapeDtypeStruct((N_EXPERTS, LANES), F32)],
        scratch_shapes=[pltpu.VMEM((N_EXPERTS, LANES), F32)],
        compiler_params=_params(1),
        name="out_router",
    )(x2d, y_lru, y_ssd, w_out, gf, wr, br, counts_in)


def _row_copy(src_hbm, src_row, dst, dst_row, sem):
    return pltpu.make_async_copy(src_hbm.at[pl.ds(src_row, 1)], dst.at[pl.ds(dst_row, 1)], sem)


def _positions_body(offs_ref, meta_ref, pos_ref):
    m = meta_ref[...]
    base = jnp.zeros_like(m)
    for e in range(N_EXPERTS):
        base = jnp.where(m == e, offs_ref[e], base)
    pos_ref[...] = base + pltpu.roll(m, SUBLANES - 2, 0)


def _positions(offs, meta):
    T = meta.shape[1]
    tb = min(T, 4096)
    blk = pl.BlockSpec((SUBLANES, tb), lambda i, offs: (0, i))
    return pl.pallas_call(
        _positions_body,
        grid_spec=pltpu.PrefetchScalarGridSpec(num_scalar_prefetch=1, grid=(T // tb,), in_specs=[blk], out_specs=blk),
        out_shape=jax.ShapeDtypeStruct((SUBLANES, T), I32),
        compiler_params=_params(1),
        name="positions",
    )(offs, meta)


def _wait_rows(src, dst, sem):
    pltpu.make_async_copy(src, dst, sem).wait()


def _dispatch_body(zf_ref, pos_ref, *rest, tm, tme, steps):
    n_groups = len(steps)
    xn_hbms = rest[:n_groups]
    xs_hbm, zbuf, xbuf, zsem, lsem, ssem = rest[n_groups:]
    n_steps = sum(steps)
    n_tiles = xs_hbm.shape[0] // tme
    ring = xbuf.shape[0]
    step = pl.program_id(0)

    def zero_copy(i):
        return pltpu.make_async_copy(zbuf, xs_hbm.at[pl.ds(pl.multiple_of(i * tme, tme), tme)], zsem)

    def load(i, start):
        slot = i % ring
        first = 0
        for xn_hbm, n in zip(xn_hbms, steps):
            @pl.when(jnp.logical_and(i >= first, i < first + n))
            def _(xn_hbm=xn_hbm, first=first):
                rows = pl.ds(pl.multiple_of((i - first) * tm, tm), tm)
                cp = pltpu.make_async_copy(xn_hbm.at[rows], xbuf.at[slot], lsem.at[slot])
                if start:
                    cp.start()
                else:
                    cp.wait()
            first += n

    def wait_scatter(i):
        slot = i % ring
        for k in range(2):
            _wait_rows(xbuf.at[slot], xs_hbm.at[pl.ds(0, tm)], ssem.at[slot])

    @pl.when(step == 0)
    def _():
        zbuf[...] = jnp.zeros_like(zbuf)

        def start(i, c):
            @pl.when(zf_ref[i] != 0)
            def _():
                zero_copy(i).start()
            return c

        def wait(i, c):
            @pl.when(zf_ref[i] != 0)
            def _():
                zero_copy(i).wait()
            return c

        lax.fori_loop(0, n_tiles, start, 0)
        lax.fori_loop(0, n_tiles, wait, 0)
        load(step, True)

    @pl.when(step + 1 < n_steps)
    def _():
        @pl.when(step >= ring - 1)
        def _():
            wait_scatter(step - (ring - 1))
        load(step + 1, True)

    load(step, False)
    slot = step % ring
    for t in range(tm):
        for k in range(2):
            _row_copy(xbuf.at[slot], t, xs_hbm, pos_ref[k, t], ssem.at[slot]).start(priority=k)

    @pl.when(step == n_steps - 1)
    def _():
        for j in range(max(n_steps - ring, 0), n_steps):
            wait_scatter(j)


def _dispatch(zflag, pos, xns, n_rows, tm, tme):
    steps = tuple(xn.shape[0] // tm for xn in xns)
    ring = 3
    return pl.pallas_call(
        functools.partial(_dispatch_body, tm=tm, tme=tme, steps=steps),
        grid_spec=pltpu.PrefetchScalarGridSpec(
            num_scalar_prefetch=1,
            grid=(sum(steps),),
            in_specs=[pl.BlockSpec((SUBLANES, tm), lambda i, zf: (0, i), memory_space=pltpu.SMEM)]
                     + [pl.BlockSpec(memory_space=pl.ANY) for _ in xns],
            out_specs=pl.BlockSpec(memory_space=pl.ANY),
            scratch_shapes=[pltpu.VMEM((tme, D_MODEL), F32), pltpu.VMEM((ring, tm, D_MODEL), F32),
                            pltpu.SemaphoreType.DMA(()), pltpu.SemaphoreType.DMA((ring,)),
                            pltpu.SemaphoreType.DMA((ring,))],
        ),
        out_shape=jax.ShapeDtypeStruct((n_rows, D_MODEL), F32),
        compiler_params=_params(1),
        name="dispatch",
    )(zflag, pos, *xns)


def _experts_body(te_ref, na_ref, x_ref, w1_ref, w3_ref, w2_ref, o_ref, w1b, w3b, w2b):
    i = pl.program_id(0)
    changed = jnp.logical_or(i == 0, te_ref[i] != te_ref[jnp.maximum(i - 1, 0)])

    @pl.when(changed)
    def _():
        w1b[...] = w1_ref[0].astype(BF16)
        w3b[...] = w3_ref[0].astype(BF16)
        w2b[...] = w2_ref[0].astype(BF16)

    @pl.when(i < na_ref[0])
    def _():
        xb = x_ref[...].astype(BF16)
        a = jnp.dot(xb, w1b[...], preferred_element_type=F32)
        b = jnp.dot(xb, w3b[...], preferred_element_type=F32)
        hd = (jax.nn.silu(a) * b).astype(BF16)
        o_ref[...] = jnp.dot(hd, w2b[...], preferred_element_type=F32)

    @pl.when(i >= na_ref[0])
    def _():
        o_ref[...] = jnp.zeros_like(o_ref)


def _experts(tile_e, n_active, xs, w1, w3, w2, tme):
    n_rows = xs.shape[0]
    row = pl.BlockSpec((tme, D_MODEL), lambda i, te, na: (i, 0))
    return pl.pallas_call(
        _experts_body,
        grid_spec=pltpu.PrefetchScalarGridSpec(
            num_scalar_prefetch=2,
            grid=(n_rows // tme,),
            in_specs=[row,
                      pl.BlockSpec((1, D_MODEL, D_EXPERT), lambda i, te, na: (te[i], 0, 0)),
                      pl.BlockSpec((1, D_MODEL, D_EXPERT), lambda i, te, na: (te[i], 0, 0)),
                      pl.BlockSpec((1, D_EXPERT, D_MODEL), lambda i, te, na: (te[i], 0, 0))],
            out_specs=row,
            scratch_shapes=[pltpu.VMEM((D_MODEL, D_EXPERT), BF16), pltpu.VMEM((D_MODEL, D_EXPERT), BF16),
                            pltpu.VMEM((D_EXPERT, D_MODEL), BF16)],
        ),
        out_shape=jax.ShapeDtypeStruct((n_rows, D_MODEL), F32),
        compiler_params=_params(1),
        name="experts",
    )(tile_e, n_active, xs, w1, w3, w2)


def _combine_body(pos_ref, nxt_ref, gate_ref, h1_ref, fg_ref, ys_hbm, y_ref, ybuf, sem, *, tm):
    step = pl.program_id(0)
    slot = step % 2

    def gather(p_ref, s):
        for t in range(tm):
            for k in range(2):
                _row_copy(ys_hbm, p_ref[k, t], ybuf.at[s, k], t, sem.at[s]).start(priority=k)

    @pl.when(step == 0)
    def _():
        gather(pos_ref, 0)

    @pl.when(step + 1 < pl.num_programs(0))
    def _():
        gather(nxt_ref, 1 - slot)

    for k in range(2):
        _wait_rows(ys_hbm.at[pl.ds(0, tm)], ybuf.at[slot, k], sem.at[slot])

    eye = lax.broadcasted_iota(I32, (tm, tm), 0) == lax.broadcasted_iota(I32, (tm, tm), 1)
    g1 = jnp.sum(jnp.where(eye, gate_ref[0:1, :], 0.0), axis=1, keepdims=True)
    g2 = jnp.sum(jnp.where(eye, gate_ref[1:2, :], 0.0), axis=1, keepdims=True)
    h2 = h1_ref[...] + (g1 * ybuf[slot, 0] + g2 * ybuf[slot, 1])
    y_ref[...] = _rmsnorm(h2, fg_ref[...])


def _combine(pos, gates, h1, fg, ys, tm):
    T = h1.shape[0]
    last = T // tm - 1
    return pl.pallas_call(
        functools.partial(_combine_body, tm=tm),
        grid=(T // tm,),
        in_specs=[pl.BlockSpec((SUBLANES, tm), lambda i: (0, i), memory_space=pltpu.SMEM),
                  pl.BlockSpec((SUBLANES, tm), lambda i: (0, jnp.minimum(i + 1, last)), memory_space=pltpu.SMEM),
                  pl.BlockSpec((SUBLANES, tm), lambda i: (0, i)),
                  pl.BlockSpec((tm, D_MODEL), lambda i: (i, 0)),
                  pl.BlockSpec((1, D_MODEL), lambda i: (0, 0)),
                  pl.BlockSpec(memory_space=pl.ANY)],
        out_specs=pl.BlockSpec((tm, D_MODEL), lambda i: (i, 0)),
        scratch_shapes=[pltpu.VMEM((2, 2, tm, D_MODEL), F32), pltpu.SemaphoreType.DMA((2,))],
        out_shape=jax.ShapeDtypeStruct((T, D_MODEL), F32),
        compiler_params=_params(1),
        name="combine",
    )(pos, pos, gates, h1, fg, ys)


def _blockdiag(w):
    per = GATE_TILE // LRU_BLOCK
    w4 = w.reshape(LRU_BLOCKS // per, per, LRU_BLOCK, LRU_BLOCK)
    eye = jnp.eye(per, dtype=w.dtype)
    return jnp.einsum('jbio,bc->jbico', w4, eye).reshape(LRU_BLOCKS // per, GATE_TILE, GATE_TILE)


def _pad_rows(c):
    return jnp.pad(c, ((0, 0), (SUBLANES - (CONV_W - 1), 0), (0, 0)))


def _router_cols(group_part, expert_part):
    r = group_part.shape[0]
    out = jnp.zeros((r, LANES), F32)
    out = out.at[:, 0:MOE_GROUPS].set(group_part)
    return out.at[:, ROUTER_E0:ROUTER_E0 + N_EXPERTS].set(expert_part)


def _lane_row(v, width=LANES):
    return jnp.pad(v, (0, width - v.shape[0])).reshape(1, width)


def _prep(norm_mix_g, w_in, lru_conv_w, lru_conv_b, lru_wa, lru_ba, lru_wx, lru_bx, lru_lambda,
          ssd_conv_w, ssd_conv_b, ssd_dt_bias, ssd_a_log, ssd_d, ssd_norm_g, w_out,
          norm_ffn_g, router_group_w, router_group_b, router_expert_w, router_expert_b,
          moe_w1, moe_w3, moe_w2, final_norm_g):
    w = w_in[0]
    wr = _router_cols(router_group_w[0], router_expert_w[0])
    wr_hi = wr.astype(BF16)
    P = dict(
        g_mix=norm_mix_g[0].reshape(1, D_MODEL),
        w_lru=w[:, :2 * D_LRU].astype(BF16),
        w_ssd=jnp.pad(w[:, 2 * D_LRU:], ((0, 0), (0, LANES - SSD_HEADS))).astype(BF16),
        lru_cw=lru_conv_w[0], lru_cb=lru_conv_b[0].reshape(1, D_LRU),
        wbd=jnp.concatenate([_blockdiag(lru_wa[0]), _blockdiag(lru_wx[0])], axis=2).astype(BF16),
        ba=lru_ba[0].reshape(1, D_LRU), bx=lru_bx[0].reshape(1, D_LRU), lam=lru_lambda[0].reshape(1, D_LRU),
        ssd_cw=ssd_conv_w[0], ssd_cb=ssd_conv_b[0].reshape(1, D_XBC),
        dtb=_lane_row(ssd_dt_bias[0]), alog=_lane_row(ssd_a_log[0]),
        dvec=jnp.repeat(ssd_d[0], SSD_HEADDIM).reshape(1, D_SSM),
        ng=ssd_norm_g[0].reshape(1, D_SSM),
        w_out=w_out[0].astype(BF16),
        g_ffn=norm_ffn_g[0].reshape(1, D_MODEL),
        wr=jnp.stack([wr_hi, (wr - wr_hi.astype(F32)).astype(BF16)]),
        br=_router_cols(router_group_b[0][None], router_expert_b[0][None]),
        w1=moe_w1[0], w3=moe_w3[0], w2=moe_w2[0],
        g_final=final_norm_g.reshape(1, D_MODEL),
    )
    return P


def _expert_layout(counts, n_pairs, tme):
    cnt = counts[:, 0].astype(I32)
    padded = ((cnt + tme - 1) // tme) * tme
    ends = jnp.cumsum(padded)
    offs = ends - padded
    n_tiles = n_pairs // tme + N_EXPERTS
    n_active = ends[-1] // tme
    tiles = jnp.arange(n_tiles, dtype=I32)
    tile_e = jnp.sum((tiles * tme)[:, None] >= ends[None, :], axis=1).astype(I32)
    last_e = jnp.sum((n_active - 1) * tme >= ends).astype(I32)
    tile_e = jnp.where(tiles < n_active, tile_e, last_e)
    is_last = jnp.any((tiles[:, None] + 1) * tme == ends[None, :], axis=1)
    zflag = jnp.logical_or(is_last, tiles >= n_active).astype(I32)
    return offs.astype(I32), tile_e, n_active.reshape(1).astype(I32), zflag, n_tiles * tme


def _mixer_router(x, lru_h0, lru_c0, ssd_h0, ssd_c0, P, start_pos, counts_in):
    B, L, _ = x.shape
    T = B * L
    Tt = min(MIX_TILE, L)
    q = min(SSD_CHUNK, L)
    tm = min(ROW_TILE, T)
    x2d = x.reshape(T, D_MODEL)

    y_lru, lru_h, lru_c = _lru(x, P['g_mix'], P['w_lru'], _pad_rows(lru_c0), lru_h0.reshape(B, 1, D_LRU),
                               P['lru_cw'], P['lru_cb'], P['wbd'], P['ba'], P['bx'], P['lam'], Tt, start_pos)
    y_ssd, ssd_h, ssd_c = _ssd(x, P['g_mix'], P['w_ssd'], _pad_rows(ssd_c0), ssd_h0,
                               P['ssd_cw'], P['ssd_cb'], P['dtb'], P['alog'], P['dvec'], P['ng'], Tt, q)
    h1, xn, meta, gates, counts = _out_router(
        x2d, y_lru.reshape(T, D_LRU), y_ssd.reshape(T, D_SSM), P['w_out'], P['g_ffn'],
        P['wr'], P['br'], counts_in, min(MIX_TILE, T))

    hist = SUBLANES - (CONV_W - 1)
    states = (lru_h.reshape(1, B, D_LRU), lru_c[:, hist:][None], ssd_h[None], ssd_c[:, hist:][None])
    return dict(h1=h1, xn=xn, meta=meta, gates=gates, shape=(B, L, D_MODEL), tm=tm), counts, states


def _moe_final(groups, counts, P):
    tme = EXPERT_TILE
    n_pairs = 2 * sum(g['h1'].shape[0] for g in groups)
    offs, tile_e, n_active, zflag, n_rows = _expert_layout(counts, n_pairs, tme)
    tm = groups[0]['tm']
    assert all(g['tm'] == tm for g in groups)
    for g in groups:
        g['pos'] = _positions(offs, g['meta'])
    pos_all = jnp.concatenate([g['pos'] for g in groups], axis=1)
    xs = _dispatch(zflag, pos_all, [g['xn'] for g in groups], n_rows, tm, tme)
    ys = _experts(tile_e, n_active, xs, P['w1'], P['w3'], P['w2'], tme)
    return [_combine(g['pos'], g['gates'], g['h1'], P['g_final'], ys, g['tm']).reshape(g['shape'])
            for g in groups]


def kernel(x_prompt, x_sample, state_lru_h, state_lru_conv, state_ssd, state_ssd_conv, norm_mix_g, w_in, lru_conv_w, lru_conv_b, lru_wa, lru_ba, lru_wx, lru_bx, lru_lambda, ssd_conv_w, ssd_conv_b, ssd_dt_bias, ssd_a_log, ssd_d, ssd_norm_g, w_out, norm_ffn_g, router_group_w, router_group_b, router_expert_w, router_expert_b, moe_w1, moe_w3, moe_w2, final_norm_g):
    P = _prep(norm_mix_g, w_in, lru_conv_w, lru_conv_b, lru_wa, lru_ba, lru_wx, lru_bx, lru_lambda,
              ssd_conv_w, ssd_conv_b, ssd_dt_bias, ssd_a_log, ssd_d, ssd_norm_g, w_out,
              norm_ffn_g, router_group_w, router_group_b, router_expert_w, router_expert_b,
              moe_w1, moe_w3, moe_w2, final_norm_g)
    bp = x_prompt.shape[0]
    gp, counts, (a1, a2, a3, a4) = _mixer_router(
        x_prompt,
        jnp.zeros((bp, D_LRU), F32), jnp.zeros((bp, CONV_W - 1, D_LRU), F32),
        jnp.zeros((bp, SSD_HEADS, SSD_HEADDIM, D_STATE), F32), jnp.zeros((bp, CONV_W - 1, D_XBC), F32),
        P, 0, jnp.zeros((N_EXPERTS, LANES), F32))
    gs, counts, (b1, b2, b3, b4) = _mixer_router(
        x_sample, state_lru_h[0], state_lru_conv[0], state_ssd[0], state_ssd_conv[0], P, PAST_LEN, counts)
    yp, ys = _moe_final([gp, gs], counts, P)
    return (yp, ys, a1, a2, a3, a4, b1, b2, b3, b4)
```

```python
import functools

import jax
import jax.numpy as jnp
from jax import lax
from jax.experimental import pallas as pl
from jax.experimental.pallas import tpu as pltpu

F32 = jnp.float32
BF16 = jnp.bfloat16
I32 = jnp.int32

D_MODEL = 1024
D_LRU = 1024
LRU_BLOCKS = 16
LRU_BLOCK = 64
LRU_C = 8.0
CONV_W = 4
D_SSM = 1024
SSD_HEADDIM = 64
SSD_HEADS = 16
SSD_GROUPS = 2
SSD_HPG = 8
D_STATE = 128
D_XBC = 1536
MOE_GROUPS = 4
EXPERTS_PER_GROUP = 8
N_EXPERTS = 32
D_EXPERT = 512
EPS = 1e-6
SSD_CHUNK = 64
PAST_LEN = 1024

LANES = 128
SUBLANES = 8
GATE_TILE = 256
ROW_TILE = 256
MIX_TILE = 512
EXPERT_TILE = 512
ROUTER_E0 = 32
VMEM_LIMIT = 52 * 1024 * 1024

_NT = (((1,), (1,)), ((), ()))
_TN = (((0,), (0,)), ((), ()))


def _params(n_axes):
    return pltpu.CompilerParams(dimension_semantics=("arbitrary",) * n_axes,
                                vmem_limit_bytes=VMEM_LIMIT)


def _rmsnorm(x, g):
    return x * lax.rsqrt(jnp.mean(x * x, axis=-1, keepdims=True) + EPS) * g


def _full(shape):
    n = len(shape)
    return pl.BlockSpec(shape, lambda *_: (0,) * n)


def _project(xb, w_ref, lo, hi):
    return jnp.dot(xb, w_ref[:, lo:hi], preferred_element_type=F32)


def _conv_block(xpad, cw_ref, cb_ref, Tt, sl):
    cw = cw_ref[:, sl]
    full = xpad[:, sl]
    y = cb_ref[:, sl]
    for k in range(CONV_W):
        shift = CONV_W - 1 - k
        xk = pltpu.roll(full, shift, 0) if shift else full
        y = y + xk[SUBLANES:SUBLANES + Tt, :] * cw[k:k + 1, :]
    return y


def _carry_history(t, xpad, c0_ref, Tt):
    @pl.when(t == 0)
    def _():
        xpad[0:SUBLANES, :] = c0_ref[0]

    @pl.when(t > 0)
    def _():
        xpad[0:SUBLANES, :] = xpad[Tt:Tt + SUBLANES, :]


def _lru_body(x_ref, gm_ref, w_ref, c0_ref, h0_ref, cw_ref, cb_ref, wbd_ref, ba_ref, bx_ref, lam_ref,
              y_ref, hN_ref, cN_ref, xpad, y_s, hcar, *, Tt, start_pos):
    t = pl.program_id(1)
    _carry_history(t, xpad, c0_ref, Tt)

    @pl.when(t == 0)
    def _():
        hcar[...] = jnp.broadcast_to(h0_ref[0], (SUBLANES, D_LRU))

    xb = _rmsnorm(x_ref[0], gm_ref[...]).astype(BF16)
    sp = jax.nn.softplus(-lam_ref[...])
    pos0 = (lax.broadcasted_iota(I32, (Tt, 1), 0) + t * Tt + start_pos) == 0
    rows = lax.broadcasted_iota(I32, (SUBLANES, GATE_TILE), 0)
    for j in range(D_LRU // GATE_TILE):
        sl = slice(GATE_TILE * j, GATE_TILE * (j + 1))
        xpad[SUBLANES:SUBLANES + Tt, sl] = _project(xb, w_ref, sl.start, sl.stop)
        xc = _conv_block(xpad, cw_ref, cb_ref, Tt, sl)
        ga = jnp.dot(xc.astype(BF16), wbd_ref[j], preferred_element_type=F32)
        r = jax.nn.sigmoid(ga[:, :GATE_TILE] + ba_ref[:, sl])
        i = jax.nn.sigmoid(ga[:, GATE_TILE:] + bx_ref[:, sl])
        a = jnp.exp((-LRU_C * r) * sp[:, sl])
        mult = jnp.where(pos0, 1.0, jnp.sqrt(1.0 - a * a))
        u = mult * i * xc
        g = _project(xb, w_ref, D_LRU + sl.start, D_LRU + sl.stop)
        gel = jax.nn.gelu(g, approximate=True)
        h = hcar[:, sl]
        for gi in range(Tt // SUBLANES):
            rs = slice(gi * SUBLANES, (gi + 1) * SUBLANES)
            a8, u8 = a[rs], u[rs]
            for s in (1, 2, 4):
                ok = rows >= s
                u_sh = pltpu.roll(u8, s, 0)
                a_sh = pltpu.roll(a8, s, 0)
                u8 = jnp.where(ok, u8 + a8 * u_sh, u8)
                a8 = jnp.where(ok, a8 * a_sh, a8)
            h8 = u8 + a8 * h
            y_s[rs, sl] = h8 * gel[rs]
            h = jnp.broadcast_to(h8[SUBLANES - 1:SUBLANES, :], (SUBLANES, GATE_TILE))
        hcar[:, sl] = h

    y_ref[0] = y_s[...].astype(BF16)
    hN_ref[0] = hcar[0:1, :]
    cN_ref[0] = xpad[Tt:Tt + SUBLANES, :]


def _lru(x, g_mix, w_lru, c0, h0, cw, cb, wbd, ba, bx, lam, Tt, start_pos):
    B, L, _ = x.shape
    seq = pl.BlockSpec((1, Tt, D_LRU), lambda b, t: (b, t, 0))
    per_b = lambda r: pl.BlockSpec((1, r, D_LRU), lambda b, t: (b, 0, 0))
    return pl.pallas_call(
        functools.partial(_lru_body, Tt=Tt, start_pos=start_pos),
        grid=(B, L // Tt),
        in_specs=[pl.BlockSpec((1, Tt, D_MODEL), lambda b, t: (b, t, 0)), _full((1, D_MODEL)), _full(w_lru.shape),
                  per_b(SUBLANES), per_b(1), _full((CONV_W, D_LRU)), _full((1, D_LRU)),
                  _full(wbd.shape), _full((1, D_LRU)), _full((1, D_LRU)), _full((1, D_LRU))],
        out_specs=[seq, per_b(1), per_b(SUBLANES)],
        out_shape=[jax.ShapeDtypeStruct((B, L, D_LRU), BF16),
                   jax.ShapeDtypeStruct((B, 1, D_LRU), F32),
                   jax.ShapeDtypeStruct((B, SUBLANES, D_LRU), F32)],
        scratch_shapes=[pltpu.VMEM((Tt + SUBLANES, D_LRU), F32), pltpu.VMEM((Tt, D_LRU), F32),
                        pltpu.VMEM((SUBLANES, D_LRU), F32)],
        compiler_params=_params(2),
        name="lru",
    )(x, g_mix, w_lru, c0, h0, cw, cb, wbd, ba, bx, lam)


def _split3(v):
    hi = v.astype(BF16)
    r1 = v - hi.astype(F32)
    mid = r1.astype(BF16)
    lo = (r1 - mid.astype(F32)).astype(BF16)
    return hi, mid, lo


def _pad_time(v, rows):
    if v.shape[0] == rows:
        return v
    return jnp.concatenate([v, jnp.zeros((rows - v.shape[0], v.shape[1]), v.dtype)], axis=0)


def _ssd_masks(q):
    P = SSD_HEADDIM
    tri = jnp.arange(q)[:, None] >= jnp.arange(q)[None, :]
    expand = jnp.arange(LANES)[:, None] == jnp.arange(D_SSM)[None, :] // P
    row_q = jnp.arange(q)[:, None]
    lane_k = jnp.arange(D_SSM)[None, :] % P
    diag = row_q == lane_k
    causal = row_q >= lane_k
    bd = jnp.arange(GATE_TILE)[:, None] // P == jnp.arange(GATE_TILE)[None, :] // P
    return (tri.astype(BF16), expand.astype(BF16), diag.astype(F32), causal.astype(F32), bd.astype(BF16))


def _ssd_body(x_ref, gm_ref, w_ref, c0_ref, s0_ref, cw_ref, cb_ref, dtb_ref, alog_ref, dvec_ref, ng_ref,
              tri_ref, expand_ref, diag_ref, causal_ref, bd_ref,
              y_ref, sN_ref, cN_ref, xpad, xa_s, y_s, st_s, dt_s, zs_s, stn_s, ecs_s, cdec_s, *, Tt, q):
    t = pl.program_id(1)
    _carry_history(t, xpad, c0_ref, Tt)

    @pl.when(t == 0)
    def _():
        st_s[...] = s0_ref[0].reshape(D_SSM, D_STATE).T

    xb = _rmsnorm(x_ref[0], gm_ref[...]).astype(BF16)
    for j in range(D_XBC // GATE_TILE):
        sl = slice(GATE_TILE * j, GATE_TILE * (j + 1))
        xpad[SUBLANES:SUBLANES + Tt, sl] = _project(xb, w_ref, D_SSM + sl.start, D_SSM + sl.stop)
        xa_s[:, sl] = jax.nn.silu(_conv_block(xpad, cw_ref, cb_ref, Tt, sl))
    for j in range(D_SSM // GATE_TILE):
        sl = slice(GATE_TILE * j, GATE_TILE * (j + 1))
        zs_s[:, sl] = jax.nn.silu(_project(xb, w_ref, sl.start, sl.stop))
    dt_s[...] = _project(xb, w_ref, D_SSM + D_XBC, D_SSM + D_XBC + LANES)
    A = -jnp.exp(alog_ref[...])
    P = SSD_HEADDIM
    blk = GATE_TILE // P
    off_b = D_SSM
    off_c = D_SSM + SSD_GROUPS * D_STATE
    gw = D_SSM // SSD_GROUPS

    def exact01(parts, w01, left):
        one = (lambda p: jnp.dot(w01, p, preferred_element_type=F32)) if left else (
            lambda p: jnp.dot(p, w01, preferred_element_type=F32))
        hi, mid, lo = parts
        return (one(lo) + one(mid)) + one(hi)

    n_chunks = Tt // q
    nb = min(8, n_chunks)

    def within_chunks(cb, carry):
        cidx = [cb * nb + i for i in range(nb)]
        r0s = [pl.multiple_of(c * q, q) for c in cidx]
        xs = [xa_s[pl.ds(r0, q), 0:D_SSM] for r0 in r0s]
        dts = [jax.nn.softplus(dt_s[pl.ds(r0, q), :] + dtb_ref[...]) for r0 in r0s]
        css = [exact01(_split3(dt * A), tri_ref[...], left=True) for dt in dts]
        Es = [exact01(_split3(jnp.concatenate([cs, dt], axis=0)), expand_ref[...], left=False)
              for cs, dt in zip(css, dts)]
        diag = diag_ref[...] != 0.0
        causal = causal_ref[...] != 0.0
        Bgs = [[xa_s[pl.ds(r0, q), off_b + g * D_STATE:off_b + (g + 1) * D_STATE].astype(BF16)
                for g in range(SSD_GROUPS)] for r0 in r0s]
        Cgs = [[xa_s[pl.ds(r0, q), off_c + g * D_STATE:off_c + (g + 1) * D_STATE].astype(BF16)
                for g in range(SSD_GROUPS)] for r0 in r0s]
        CBs = [jnp.concatenate(
            [lax.dot_general(Cg, jnp.concatenate([_pad_time(Bg, P)] * SSD_HPG, axis=0), _NT,
                             preferred_element_type=F32) for Bg, Cg in zip(Bgc, Cgc)], axis=1)
            for Bgc, Cgc in zip(Bgs, Cgs)]
        xws = []
        for i in range(nb):
            E_cs, E_dt = Es[i][0:q], Es[i][q:2 * q]
            cs_last = E_cs[q - 1:q, :]
            xws.append((jnp.exp(cs_last - E_cs) * E_dt * xs[i]).astype(BF16))
            ecs_s[pl.ds(r0s[i], q), :] = jnp.exp(E_cs)
            cdec_s[pl.ds(pl.multiple_of(cidx[i] * SUBLANES, SUBLANES), SUBLANES), :] = jnp.broadcast_to(
                jnp.exp(cs_last), (SUBLANES, D_SSM))
        for i in range(nb):
            for g in range(SSD_GROUPS):
                sl = slice(g * gw, (g + 1) * gw)
                stn_s[cidx[i], :, sl] = lax.dot_general(Bgs[i][g], xws[i][:, sl], _TN, preferred_element_type=F32)
        Mws = []
        for i in range(nb):
            E_cs, E_dt = Es[i][0:q], Es[i][q:2 * q]
            r_cs = jnp.sum(jnp.where(diag, E_cs, 0.0), axis=0, keepdims=True)
            r_dt = jnp.sum(jnp.where(diag, E_dt, 0.0), axis=0, keepdims=True)
            Lm = jnp.where(causal, jnp.exp(jnp.where(causal, E_cs - r_cs, 0.0)), 0.0)
            Mws.append((CBs[i] * Lm * r_dt).astype(BF16))
        for i in range(nb):
            xsb = xs[i].astype(BF16)
            for j in range(D_SSM // GATE_TILE):
                sl = slice(j * GATE_TILE, (j + 1) * GATE_TILE)
                slab = _pad_time(xsb[:, sl], P)
                rhs = jnp.concatenate([slab] * blk, axis=0) * bd_ref[...]
                y_s[pl.ds(r0s[i], q), sl] = jnp.dot(Mws[i][:, sl], rhs, preferred_element_type=F32)
        return carry

    def across_chunks(c, carry):
        r0 = pl.multiple_of(c * q, q)
        cdec = cdec_s[pl.ds(pl.multiple_of(c * SUBLANES, SUBLANES), 1), :]
        for g in range(SSD_GROUPS):
            sl = slice(g * gw, (g + 1) * gw)
            Cg = xa_s[pl.ds(r0, q), off_c + g * D_STATE:off_c + (g + 1) * D_STATE].astype(BF16)
            S = st_s[:, sl]
            yo = jnp.dot(Cg, S.astype(BF16), preferred_element_type=F32)
            y_s[pl.ds(r0, q), sl] = y_s[pl.ds(r0, q), sl] + yo * ecs_s[pl.ds(r0, q), sl]
            st_s[:, sl] = cdec[:, sl] * S + stn_s[c, :, sl]
        return carry

    lax.fori_loop(0, n_chunks // nb, within_chunks, 0)
    lax.fori_loop(0, n_chunks, across_chunks, 0, unroll=True)

    @pl.when(t == pl.num_programs(1) - 1)
    def _():
        sN_ref[0] = st_s[...].T.reshape(SSD_HEADS, SSD_HEADDIM, D_STATE)

    y = y_s[...] + dvec_ref[...] * xa_s[:, 0:D_SSM]
    y = y * zs_s[...]
    gw = D_SSM // SSD_GROUPS
    for g in range(SSD_GROUPS):
        sl = slice(g * gw, (g + 1) * gw)
        yg = y[:, sl]
        yg = yg * lax.rsqrt(jnp.mean(yg * yg, axis=-1, keepdims=True) + EPS)
        y_ref[0, :, sl] = (yg * ng_ref[:, sl]).astype(BF16)
    cN_ref[0] = xpad[Tt:Tt + SUBLANES, :]


def _ssd(x, g_mix, w_ssd, c0, s0, cw, cb, dtb, alog, dvec, ng, Tt, q):
    B, L, _ = x.shape
    seq = lambda w: pl.BlockSpec((1, Tt, w), lambda b, t: (b, t, 0))
    per_b = pl.BlockSpec((1, SUBLANES, D_XBC), lambda b, t: (b, 0, 0))
    st = pl.BlockSpec((1, SSD_HEADS, SSD_HEADDIM, D_STATE), lambda b, t: (b, 0, 0, 0))
    masks = _ssd_masks(q)
    return pl.pallas_call(
        functools.partial(_ssd_body, Tt=Tt, q=q),
        grid=(B, L // Tt),
        in_specs=[seq(D_MODEL), _full((1, D_MODEL)), _full(w_ssd.shape),
                  per_b, st, _full((CONV_W, D_XBC)), _full((1, D_XBC)),
                  _full((1, LANES)), _full((1, LANES)), _full((1, D_SSM)), _full((1, D_SSM))]
                 + [_full(m.shape) for m in masks],
        out_specs=[seq(D_SSM), st, per_b],
        out_shape=[jax.ShapeDtypeStruct((B, L, D_SSM), BF16),
                   jax.ShapeDtypeStruct((B, SSD_HEADS, SSD_HEADDIM, D_STATE), F32),
                   jax.ShapeDtypeStruct((B, SUBLANES, D_XBC), F32)],
        scratch_shapes=[pltpu.VMEM((Tt + SUBLANES, D_XBC), F32), pltpu.VMEM((Tt, D_XBC), F32),
                        pltpu.VMEM((Tt, D_SSM), F32), pltpu.VMEM((D_STATE, D_SSM), F32),
                        pltpu.VMEM((Tt, LANES), F32), pltpu.VMEM((Tt, D_SSM), F32),
                        pltpu.VMEM((Tt // q, D_STATE, D_SSM), F32), pltpu.VMEM((Tt, D_SSM), F32),
                        pltpu.VMEM((Tt // q * SUBLANES, D_SSM), F32)],
        compiler_params=_params(2),
        name="ssd",
    )(x, g_mix, w_ssd, c0, s0, cw, cb, dtb, alog, dvec, ng, *masks)


def _out_router_body(x_ref, yl_ref, ys_ref, wo_ref, gf_ref, wr_ref, br_ref, cin_ref,
                     h1_ref, xn_ref, meta_ref, gate_ref, cnt_ref, carry, *, tm):
    step = pl.program_id(0)

    @pl.when(step == 0)
    def _():
        carry[...] = cin_ref[...]

    mix = jnp.concatenate([yl_ref[...], ys_ref[...]], axis=1)
    h1 = x_ref[...] + jnp.dot(mix, wo_ref[...], preferred_element_type=F32)
    h1_ref[...] = h1
    xn = _rmsnorm(h1, gf_ref[...])
    xn_ref[...] = xn

    xh = xn.astype(BF16)
    xm = (xn - xh.astype(F32)).astype(BF16)
    small = (jnp.dot(xh, wr_ref[1], preferred_element_type=F32)
             + jnp.dot(xm, wr_ref[0], preferred_element_type=F32))
    logits = small + jnp.dot(xh, wr_ref[0], preferred_element_type=F32) + br_ref[...]
    lt = logits.T
    rows8 = lax.broadcasted_iota(I32, (SUBLANES, tm), 0)
    lg = jnp.where(rows8 < MOE_GROUPS, lt[0:SUBLANES, :], -jnp.inf)
    eg = jnp.exp(lg - jnp.max(lg, axis=0, keepdims=True))
    pg = eg / jnp.sum(eg, axis=0, keepdims=True)
    pgs = jnp.max(pg, axis=0, keepdims=True)
    rows8f = rows8.astype(F32)
    gsel = jnp.min(jnp.where(pg == pgs, rows8f, float(SUBLANES)), axis=0, keepdims=True)

    rows32 = lax.broadcasted_iota(I32, (N_EXPERTS, tm), 0)
    rows32f = rows32.astype(F32)
    grp = (rows32 // EXPERTS_PER_GROUP).astype(F32)
    le = lt[ROUTER_E0:ROUTER_E0 + N_EXPERTS, :]
    ing = grp == gsel
    lem = jnp.where(ing, le, -jnp.inf)
    ee = jnp.exp(lem - jnp.max(lem, axis=0, keepdims=True))
    pe = ee / jnp.sum(ee, axis=0, keepdims=True)
    pe1 = jnp.where(ing, pe, -1.0)
    v1 = jnp.max(pe1, axis=0, keepdims=True)
    i1 = jnp.min(jnp.where(pe1 == v1, rows32f, float(N_EXPERTS)), axis=0, keepdims=True)
    pe2 = jnp.where(rows32f == i1, -1.0, pe1)
    v2 = jnp.max(pe2, axis=0, keepdims=True)
    i2 = jnp.min(jnp.where(pe2 == v2, rows32f, float(N_EXPERTS)), axis=0, keepdims=True)
    sv = v1 + v2
    w1 = v1 / sv * pgs
    w2 = v2 / sv * pgs

    oh1 = rows32f == i1
    oh2 = rows32f == i2
    oh = jnp.where(oh1 | oh2, 1.0, 0.0)
    before = (lax.broadcasted_iota(I32, (tm, tm), 0) < lax.broadcasted_iota(I32, (tm, tm), 1))
    pref = jnp.dot(oh.astype(BF16), jnp.where(before, 1.0, 0.0).astype(BF16), preferred_element_type=F32)
    pref = pref + carry[:, 0:1]
    r1 = jnp.sum(jnp.where(oh1, pref, 0.0), axis=0, keepdims=True)
    r2 = jnp.sum(jnp.where(oh2, pref, 0.0), axis=0, keepdims=True)
    carry[...] = carry[...] + jnp.sum(oh, axis=1, keepdims=True)
    cnt_ref[...] = carry[...]

    meta = jnp.where(rows8 == 0, i1, jnp.where(rows8 == 1, i2, jnp.where(rows8 == 2, r1, jnp.where(rows8 == 3, r2, 0.0))))
    meta_ref[...] = meta.astype(I32)
    gate_ref[...] = jnp.where(rows8 == 0, w1, jnp.where(rows8 == 1, w2, 0.0))


def _out_router(x2d, y_lru, y_ssd, w_out, gf, wr, br, counts_in, tm):
    T = x2d.shape[0]
    row = lambda w: pl.BlockSpec((tm, w), lambda i: (i, 0))
    col = pl.BlockSpec((SUBLANES, tm), lambda i: (0, i))
    return pl.pallas_call(
        functools.partial(_out_router_body, tm=tm),
        grid=(T // tm,),
        in_specs=[row(D_MODEL), row(D_LRU), row(D_SSM), _full((D_LRU + D_SSM, D_MODEL)), _full((1, D_MODEL)),
                  _full((2, D_MODEL, LANES)), _full((1, LANES)), _full((N_EXPERTS, LANES))],
        out_specs=[row(D_MODEL), row(D_MODEL), col, col, _full((N_EXPERTS, LANES))],
        out_shape=[jax.ShapeDtypeStruct((T, D_MODEL), F32), jax.ShapeDtypeStruct((T, D_MODEL), F32),
                   jax.ShapeDtypeStruct((SUBLANES, T), I32), jax.ShapeDtypeStruct((SUBLANES, T), F32),
                   jax.ShapeDtypeStruct((N_EXPERTS, LANES), F32)],
        scratch_shapes=[pltpu.VMEM((N_EXPERTS, LANES), F32)],
        compiler_params=_params(1),
        name="out_router",
    )(x2d, y_lru, y_ssd, w_out, gf, wr, br, counts_in)


def _row_copy(src_hbm, src_row, dst, dst_row, sem):
    return pltpu.make_async_copy(src_hbm.at[pl.ds(src_row, 1)], dst.at[pl.ds(dst_row, 1)], sem)


def _positions_body(offs_ref, meta_ref, pos_ref):
    m = meta_ref[...]
    base = jnp.zeros_like(m)
    for e in range(N_EXPERTS):
        base = jnp.where(m == e, offs_ref[e], base)
    pos_ref[...] = base + pltpu.roll(m, SUBLANES - 2, 0)


def _positions(offs, meta):
    T = meta.shape[1]
    tb = min(T, 4096)
    blk = pl.BlockSpec((SUBLANES, tb), lambda i, offs: (0, i))
    return pl.pallas_call(
        _positions_body,
        grid_spec=pltpu.PrefetchScalarGridSpec(num_scalar_prefetch=1, grid=(T // tb,), in_specs=[blk], out_specs=blk),
        out_shape=jax.ShapeDtypeStruct((SUBLANES, T), I32),
        compiler_params=_params(1),
        name="positions",
    )(offs, meta)


def _wait_rows(src, dst, sem):
    pltpu.make_async_copy(src, dst, sem).wait()


def _dispatch_body(zf_ref, pos_ref, *rest, tm, tme, steps):
    n_groups = len(steps)
    xn_hbms = rest[:n_groups]
    xs_hbm, zbuf, xbuf, zsem, lsem, ssem = rest[n_groups:]
    n_steps = sum(steps)
    n_tiles = xs_hbm.shape[0] // tme
    ring = xbuf.shape[0]
    step = pl.program_id(0)

    def zero_copy(i):
        return pltpu.make_async_copy(zbuf, xs_hbm.at[pl.ds(pl.multiple_of(i * tme, tme), tme)], zsem)

    def load(i, start):
        slot = i % ring
        first = 0
        for xn_hbm, n in zip(xn_hbms, steps):
            @pl.when(jnp.logical_and(i >= first, i < first + n))
            def _(xn_hbm=xn_hbm, first=first):
                rows = pl.ds(pl.multiple_of((i - first) * tm, tm), tm)
                cp = pltpu.make_async_copy(xn_hbm.at[rows], xbuf.at[slot], lsem.at[slot])
                if start:
                    cp.start()
                else:
                    cp.wait()
            first += n

    def wait_scatter(i):
        slot = i % ring
        for k in range(2):
            _wait_rows(xbuf.at[slot], xs_hbm.at[pl.ds(0, tm)], ssem.at[slot])

    @pl.when(step == 0)
    def _():
        zbuf[...] = jnp.zeros_like(zbuf)

        def start(i, c):
            @pl.when(zf_ref[i] != 0)
            def _():
                zero_copy(i).start()
            return c

        def wait(i, c):
            @pl.when(zf_ref[i] != 0)
            def _():
                zero_copy(i).wait()
            return c

        lax.fori_loop(0, n_tiles, start, 0)
        lax.fori_loop(0, n_tiles, wait, 0)
        load(step, True)

    @pl.when(step + 1 < n_steps)
    def _():
        @pl.when(step >= ring - 1)
        def _():
            wait_scatter(step - (ring - 1))
        load(step + 1, True)

    load(step, False)
    slot = step % ring
    for t in range(tm):
        for k in range(2):
            _row_copy(xbuf.at[slot], t, xs_hbm, pos_ref[k, t], ssem.at[slot]).start(priority=k)

    @pl.when(step == n_steps - 1)
    def _():
        for j in range(max(n_steps - ring, 0), n_steps):
            wait_scatter(j)


def _dispatch(zflag, pos, xns, n_rows, tm, tme):
    steps = tuple(xn.shape[0] // tm for xn in xns)
    ring = 3
    return pl.pallas_call(
        functools.partial(_dispatch_body, tm=tm, tme=tme, steps=steps),
        grid_spec=pltpu.PrefetchScalarGridSpec(
            num_scalar_prefetch=1,
            grid=(sum(steps),),
            in_specs=[pl.BlockSpec((SUBLANES, tm), lambda i, zf: (0, i), memory_space=pltpu.SMEM)]
                     + [pl.BlockSpec(memory_space=pl.ANY) for _ in xns],
            out_specs=pl.BlockSpec(memory_space=pl.ANY),
            scratch_shapes=[pltpu.VMEM((tme, D_MODEL), F32), pltpu.VMEM((ring, tm, D_MODEL), F32),
                            pltpu.SemaphoreType.DMA(()), pltpu.SemaphoreType.DMA((ring,)),
                            pltpu.SemaphoreType.DMA((ring,))],
        ),
        out_shape=jax.ShapeDtypeStruct((n_rows, D_MODEL), F32),
        compiler_params=_params(1),
        name="dispatch",
    )(zflag, pos, *xns)


def _experts_body(te_ref, na_ref, x_ref, w1_ref, w3_ref, w2_ref, o_ref, w1b, w3b, w2b):
    i = pl.program_id(0)
    changed = jnp.logical_or(i == 0, te_ref[i] != te_ref[jnp.maximum(i - 1, 0)])

    @pl.when(changed)
    def _():
        w1b[...] = w1_ref[0].astype(BF16)
        w3b[...] = w3_ref[0].astype(BF16)
        w2b[...] = w2_ref[0].astype(BF16)

    @pl.when(i < na_ref[0])
    def _():
        xb = x_ref[...].astype(BF16)
        a = jnp.dot(xb, w1b[...], preferred_element_type=F32)
        b = jnp.dot(xb, w3b[...], preferred_element_type=F32)
        hd = (jax.nn.silu(a) * b).astype(BF16)
        o_ref[...] = jnp.dot(hd, w2b[...], preferred_element_type=F32)

    @pl.when(i >= na_ref[0])
    def _():
        o_ref[...] = jnp.zeros_like(o_ref)


def _experts(tile_e, n_active, xs, w1, w3, w2, tme):
    n_rows = xs.shape[0]
    row = pl.BlockSpec((tme, D_MODEL), lambda i, te, na: (i, 0))
    return pl.pallas_call(
        _experts_body,
        grid_spec=pltpu.PrefetchScalarGridSpec(
            num_scalar_prefetch=2,
            grid=(n_rows // tme,),
            in_specs=[row,
                      pl.BlockSpec((1, D_MODEL, D_EXPERT), lambda i, te, na: (te[i], 0, 0)),
                      pl.BlockSpec((1, D_MODEL, D_EXPERT), lambda i, te, na: (te[i], 0, 0)),
                      pl.BlockSpec((1, D_EXPERT, D_MODEL), lambda i, te, na: (te[i], 0, 0))],
            out_specs=row,
            scratch_shapes=[pltpu.VMEM((D_MODEL, D_EXPERT), BF16), pltpu.VMEM((D_MODEL, D_EXPERT), BF16),
                            pltpu.VMEM((D_EXPERT, D_MODEL), BF16)],
        ),
        out_shape=jax.ShapeDtypeStruct((n_rows, D_MODEL), F32),
        compiler_params=_params(1),
        name="experts",
    )(tile_e, n_active, xs, w1, w3, w2)


def _combine_body(pos_ref, nxt_ref, gate_ref, h1_ref, fg_ref, ys_hbm, y_ref, ybuf, sem, *, tm):
    step = pl.program_id(0)
    slot = step % 2

    def gather(p_ref, s):
        for t in range(tm):
            for k in range(2):
                _row_copy(ys_hbm, p_ref[k, t], ybuf.at[s, k], t, sem.at[s]).start(priority=k)

    @pl.when(step == 0)
    def _():
        gather(pos_ref, 0)

    @pl.when(step + 1 < pl.num_programs(0))
    def _():
        gather(nxt_ref, 1 - slot)

    for k in range(2):
        _wait_rows(ys_hbm.at[pl.ds(0, tm)], ybuf.at[slot, k], sem.at[slot])

    eye = lax.broadcasted_iota(I32, (tm, tm), 0) == lax.broadcasted_iota(I32, (tm, tm), 1)
    g1 = jnp.sum(jnp.where(eye, gate_ref[0:1, :], 0.0), axis=1, keepdims=True)
    g2 = jnp.sum(jnp.where(eye, gate_ref[1:2, :], 0.0), axis=1, keepdims=True)
    h2 = h1_ref[...] + (g1 * ybuf[slot, 0] + g2 * ybuf[slot, 1])
    y_ref[...] = _rmsnorm(h2, fg_ref[...])


def _combine(pos, gates, h1, fg, ys, tm):
    T = h1.shape[0]
    last = T // tm - 1
    return pl.pallas_call(
        functools.partial(_combine_body, tm=tm),
        grid=(T // tm,),
        in_specs=[pl.BlockSpec((SUBLANES, tm), lambda i: (0, i), memory_space=pltpu.SMEM),
                  pl.BlockSpec((SUBLANES, tm), lambda i: (0, jnp.minimum(i + 1, last)), memory_space=pltpu.SMEM),
                  pl.BlockSpec((SUBLANES, tm), lambda i: (0, i)),
                  pl.BlockSpec((tm, D_MODEL), lambda i: (i, 0)),
                  pl.BlockSpec((1, D_MODEL), lambda i: (0, 0)),
                  pl.BlockSpec(memory_space=pl.ANY)],
        out_specs=pl.BlockSpec((tm, D_MODEL), lambda i: (i, 0)),
        scratch_shapes=[pltpu.VMEM((2, 2, tm, D_MODEL), F32), pltpu.SemaphoreType.DMA((2,))],
        out_shape=jax.ShapeDtypeStruct((T, D_MODEL), F32),
        compiler_params=_params(1),
        name="combine",
    )(pos, pos, gates, h1, fg, ys)


def _blockdiag(w):
    per = GATE_TILE // LRU_BLOCK
    w4 = w.reshape(LRU_BLOCKS // per, per, LRU_BLOCK, LRU_BLOCK)
    eye = jnp.eye(per, dtype=w.dtype)
    return jnp.einsum('jbio,bc->jbico', w4, eye).reshape(LRU_BLOCKS // per, GATE_TILE, GATE_TILE)


def _pad_rows(c):
    return jnp.pad(c, ((0, 0), (SUBLANES - (CONV_W - 1), 0), (0, 0)))


def _router_cols(group_part, expert_part):
    r = group_part.shape[0]
    out = jnp.zeros((r, LANES), F32)
    out = out.at[:, 0:MOE_GROUPS].set(group_part)
    return out.at[:, ROUTER_E0:ROUTER_E0 + N_EXPERTS].set(expert_part)


def _lane_row(v, width=LANES):
    return jnp.pad(v, (0, width - v.shape[0])).reshape(1, width)


def _prep(norm_mix_g, w_in, lru_conv_w, lru_conv_b, lru_wa, lru_ba, lru_wx, lru_bx, lru_lambda,
          ssd_conv_w, ssd_conv_b, ssd_dt_bias, ssd_a_log, ssd_d, ssd_norm_g, w_out,
          norm_ffn_g, router_group_w, router_group_b, router_expert_w, router_expert_b,
          moe_w1, moe_w3, moe_w2, final_norm_g):
    w = w_in[0]
    wr = _router_cols(router_group_w[0], router_expert_w[0])
    wr_hi = wr.astype(BF16)
    P = dict(
        g_mix=norm_mix_g[0].reshape(1, D_MODEL),
        w_lru=w[:, :2 * D_LRU].astype(BF16),
        w_ssd=jnp.pad(w[:, 2 * D_LRU:], ((0, 0), (0, LANES - SSD_HEADS))).astype(BF16),
        lru_cw=lru_conv_w[0], lru_cb=lru_conv_b[0].reshape(1, D_LRU),
        wbd=jnp.concatenate([_blockdiag(lru_wa[0]), _blockdiag(lru_wx[0])], axis=2).astype(BF16),
        ba=lru_ba[0].reshape(1, D_LRU), bx=lru_bx[0].reshape(1, D_LRU), lam=lru_lambda[0].reshape(1, D_LRU),
        ssd_cw=ssd_conv_w[0], ssd_cb=ssd_conv_b[0].reshape(1, D_XBC),
        dtb=_lane_row(ssd_dt_bias[0]), alog=_lane_row(ssd_a_log[0]),
        dvec=jnp.repeat(ssd_d[0], SSD_HEADDIM).reshape(1, D_SSM),
        ng=ssd_norm_g[0].reshape(1, D_SSM),
        w_out=w_out[0].astype(BF16),
        g_ffn=norm_ffn_g[0].reshape(1, D_MODEL),
        wr=jnp.stack([wr_hi, (wr - wr_hi.astype(F32)).astype(BF16)]),
        br=_router_cols(router_group_b[0][None], router_expert_b[0][None]),
        w1=moe_w1[0], w3=moe_w3[0], w2=moe_w2[0],
        g_final=final_norm_g.reshape(1, D_MODEL),
    )
    return P


def _expert_layout(counts, n_pairs, tme):
    cnt = counts[:, 0].astype(I32)
    padded = ((cnt + tme - 1) // tme) * tme
    ends = jnp.cumsum(padded)
    offs = ends - padded
    n_tiles = n_pairs // tme + N_EXPERTS
    n_active = ends[-1] // tme
    tiles = jnp.arange(n_tiles, dtype=I32)
    tile_e = jnp.sum((tiles * tme)[:, None] >= ends[None, :], axis=1).astype(I32)
    last_e = jnp.sum((n_active - 1) * tme >= ends).astype(I32)
    tile_e = jnp.where(tiles < n_active, tile_e, last_e)
    is_last = jnp.any((tiles[:, None] + 1) * tme == ends[None, :], axis=1)
    zflag = jnp.logical_or(is_last, tiles >= n_active).astype(I32)
    return offs.astype(I32), tile_e, n_active.reshape(1).astype(I32), zflag, n_tiles * tme


def _mixer_router(x, lru_h0, lru_c0, ssd_h0, ssd_c0, P, start_pos, counts_in):
    B, L, _ = x.shape
    T = B * L
    Tt = min(MIX_TILE, L)
    q = min(SSD_CHUNK, L)
    tm = min(ROW_TILE, T)
    x2d = x.reshape(T, D_MODEL)

    y_lru, lru_h, lru_c = _lru(x, P['g_mix'], P['w_lru'], _pad_rows(lru_c0), lru_h0.reshape(B, 1, D_LRU),
                               P['lru_cw'], P['lru_cb'], P['wbd'], P['ba'], P['bx'], P['lam'], Tt, start_pos)
    y_ssd, ssd_h, ssd_c = _ssd(x, P['g_mix'], P['w_ssd'], _pad_rows(ssd_c0), ssd_h0,
                               P['ssd_cw'], P['ssd_cb'], P['dtb'], P['alog'], P['dvec'], P['ng'], Tt, q)
    h1, xn, meta, gates, counts = _out_router(
        x2d, y_lru.reshape(T, D_LRU), y_ssd.reshape(T, D_SSM), P['w_out'], P['g_ffn'],
        P['wr'], P['br'], counts_in, min(MIX_TILE, T))

    hist = SUBLANES - (CONV_W - 1)
    states = (lru_h.reshape(1, B, D_LRU), lru_c[:, hist:][None], ssd_h[None], ssd_c[:, hist:][None])
    return dict(h1=h1, xn=xn, meta=meta, gates=gates, shape=(B, L, D_MODEL), tm=tm), counts, states


def _moe_final(groups, counts, P):
    tme = EXPERT_TILE
    n_pairs = 2 * sum(g['h1'].shape[0] for g in groups)
    offs, tile_e, n_active, zflag, n_rows = _expert_layout(counts, n_pairs, tme)
    tm = groups[0]['tm']
    assert all(g['tm'] == tm for g in groups)
    for g in groups:
        g['pos'] = _positions(offs, g['meta'])
    pos_all = jnp.concatenate([g['pos'] for g in groups], axis=1)
    xs = _dispatch(zflag, pos_all, [g['xn'] for g in groups], n_rows, tm, tme)
    ys = _experts(tile_e, n_active, xs, P['w1'], P['w3'], P['w2'], tme)
    return [_combine(g['pos'], g['gates'], g['h1'], P['g_final'], ys, g['tm']).reshape(g['shape'])
            for g in groups]


def kernel(x_prompt, x_sample, state_lru_h, state_lru_conv, state_ssd, state_ssd_conv, norm_mix_g, w_in, lru_conv_w, lru_conv_b, lru_wa, lru_ba, lru_wx, lru_bx, lru_lambda, ssd_conv_w, ssd_conv_b, ssd_dt_bias, ssd_a_log, ssd_d, ssd_norm_g, w_out, norm_ffn_g, router_group_w, router_group_b, router_expert_w, router_expert_b, moe_w1, moe_w3, moe_w2, final_norm_g):
    P = _prep(norm_mix_g, w_in, lru_conv_w, lru_conv_b, lru_wa, lru_ba, lru_wx, lru_bx, lru_lambda,
              ssd_conv_w, ssd_conv_b, ssd_dt_bias, ssd_a_log, ssd_d, ssd_norm_g, w_out,
              norm_ffn_g, router_group_w, router_group_b, router_expert_w, router_expert_b,
              moe_w1, moe_w3, moe_w2, final_norm_g)
    bp = x_prompt.shape[0]
    gp, counts, (a1, a2, a3, a4) = _mixer_router(
        x_prompt,
        jnp.zeros((bp, D_LRU), F32), jnp.zeros((bp, CONV_W - 1, D_LRU), F32),
        jnp.zeros((bp, SSD_HEADS, SSD_HEADDIM, D_STATE), F32), jnp.zeros((bp, CONV_W - 1, D_XBC), F32),
        P, 0, jnp.zeros((N_EXPERTS, LANES), F32))
    gs, counts, (b1, b2, b3, b4) = _mixer_router(
        x_sample, state_lru_h[0], state_lru_conv[0], state_ssd[0], state_ssd_conv[0], P, PAST_LEN, counts)
    yp, ys = _moe_final([gp, gs], counts, P)
    return (yp, ys, a1, a2, a3, a4, b1, b2, b3, b4)
```

```python
import functools

import jax
import jax.numpy as jnp
from jax import lax
from jax.experimental import pallas as pl
from jax.experimental.pallas import tpu as pltpu

F32 = jnp.float32
BF16 = jnp.bfloat16
I32 = jnp.int32

D_MODEL = 1024
D_LRU = 1024
LRU_BLOCKS = 16
LRU_BLOCK = 64
LRU_C = 8.0
CONV_W = 4
D_SSM = 1024
SSD_HEADDIM = 64
SSD_HEADS = 16
SSD_GROUPS = 2
SSD_HPG = 8
D_STATE = 128
D_XBC = 1536
MOE_GROUPS = 4
EXPERTS_PER_GROUP = 8
N_EXPERTS = 32
D_EXPERT = 512
EPS = 1e-6
SSD_CHUNK = 64
PAST_LEN = 1024

LANES = 128
SUBLANES = 8
GATE_TILE = 256
ROW_TILE = 256
MIX_TILE = 512
EXPERT_TILE = 512
ROUTER_E0 = 32
VMEM_LIMIT = 52 * 1024 * 1024

_NT = (((1,), (1,)), ((), ()))
_TN = (((0,), (0,)), ((), ()))


def _params(n_axes):
    return pltpu.CompilerParams(dimension_semantics=("arbitrary",) * n_axes,
                                vmem_limit_bytes=VMEM_LIMIT)


def _rmsnorm(x, g):
    return x * lax.rsqrt(jnp.mean(x * x, axis=-1, keepdims=True) + EPS) * g


def _full(shape):
    n = len(shape)
    return pl.BlockSpec(shape, lambda *_: (0,) * n)


def _project(xb, w_ref, lo, hi):
    return jnp.dot(xb, w_ref[:, lo:hi], preferred_element_type=F32)


def _conv_block(xpad, cw_ref, cb_ref, Tt, sl):
    cw = cw_ref[:, sl]
    full = xpad[:, sl]
    y = cb_ref[:, sl]
    for k in range(CONV_W):
        shift = CONV_W - 1 - k
        xk = pltpu.roll(full, shift, 0) if shift else full
        y = y + xk[SUBLANES:SUBLANES + Tt, :] * cw[k:k + 1, :]
    return y


def _carry_history(t, xpad, c0_ref, Tt):
    @pl.when(t == 0)
    def _():
        xpad[0:SUBLANES, :] = c0_ref[0]

    @pl.when(t > 0)
    def _():
        xpad[0:SUBLANES, :] = xpad[Tt:Tt + SUBLANES, :]


def _lru_body(x_ref, gm_ref, w_ref, c0_ref, h0_ref, cw_ref, cb_ref, wbd_ref, ba_ref, bx_ref, lam_ref,
              y_ref, hN_ref, cN_ref, xpad, y_s, hcar, *, Tt, start_pos):
    t = pl.program_id(1)
    _carry_history(t, xpad, c0_ref, Tt)

    @pl.when(t == 0)
    def _():
        hcar[...] = jnp.broadcast_to(h0_ref[0], (SUBLANES, D_LRU))

    xb = _rmsnorm(x_ref[0], gm_ref[...]).astype(BF16)
    sp = jax.nn.softplus(-lam_ref[...])
    pos0 = (lax.broadcasted_iota(I32, (Tt, 1), 0) + t * Tt + start_pos) == 0
    rows = lax.broadcasted_iota(I32, (SUBLANES, GATE_TILE), 0)
    for j in range(D_LRU // GATE_TILE):
        sl = slice(GATE_TILE * j, GATE_TILE * (j + 1))
        xpad[SUBLANES:SUBLANES + Tt, sl] = _project(xb, w_ref, sl.start, sl.stop)
        xc = _conv_block(xpad, cw_ref, cb_ref, Tt, sl)
        ga = jnp.dot(xc.astype(BF16), wbd_ref[j], preferred_element_type=F32)
        r = jax.nn.sigmoid(ga[:, :GATE_TILE] + ba_ref[:, sl])
        i = jax.nn.sigmoid(ga[:, GATE_TILE:] + bx_ref[:, sl])
        a = jnp.exp((-LRU_C * r) * sp[:, sl])
        y1 = 1.0 - a * a
        mult = jnp.where(pos0, 1.0, jnp.where(y1 > 0.0, y1 * lax.rsqrt(y1), 0.0))
        u = mult * i * xc
        g = _project(xb, w_ref, D_LRU + sl.start, D_LRU + sl.stop)
        gel = jax.nn.gelu(g, approximate=True)
        h = hcar[:, sl]
        for gi in range(Tt // SUBLANES):
            rs = slice(gi * SUBLANES, (gi + 1) * SUBLANES)
            a8, u8 = a[rs], u[rs]
            for s in (1, 2, 4):
                ok = rows >= s
                u_sh = pltpu.roll(u8, s, 0)
                a_sh = pltpu.roll(a8, s, 0)
                u8 = jnp.where(ok, u8 + a8 * u_sh, u8)
                a8 = jnp.where(ok, a8 * a_sh, a8)
            h8 = u8 + a8 * h
            y_s[rs, sl] = h8 * gel[rs]
            h = jnp.broadcast_to(h8[SUBLANES - 1:SUBLANES, :], (SUBLANES, GATE_TILE))
        hcar[:, sl] = h

    y_ref[0] = y_s[...].astype(BF16)
    hN_ref[0] = hcar[0:1, :]
    cN_ref[0] = xpad[Tt:Tt + SUBLANES, :]


def _lru(x, g_mix, w_lru, c0, h0, cw, cb, wbd, ba, bx, lam, Tt, start_pos):
    B, L, _ = x.shape
    seq = pl.BlockSpec((1, Tt, D_LRU), lambda b, t: (b, t, 0))
    per_b = lambda r: pl.BlockSpec((1, r, D_LRU), lambda b, t: (b, 0, 0))
    return pl.pallas_call(
        functools.partial(_lru_body, Tt=Tt, start_pos=start_pos),
        grid=(B, L // Tt),
        in_specs=[pl.BlockSpec((1, Tt, D_MODEL), lambda b, t: (b, t, 0)), _full((1, D_MODEL)), _full(w_lru.shape),
                  per_b(SUBLANES), per_b(1), _full((CONV_W, D_LRU)), _full((1, D_LRU)),
                  _full(wbd.shape), _full((1, D_LRU)), _full((1, D_LRU)), _full((1, D_LRU))],
        out_specs=[seq, per_b(1), per_b(SUBLANES)],
        out_shape=[jax.ShapeDtypeStruct((B, L, D_LRU), BF16),
                   jax.ShapeDtypeStruct((B, 1, D_LRU), F32),
                   jax.ShapeDtypeStruct((B, SUBLANES, D_LRU), F32)],
        scratch_shapes=[pltpu.VMEM((Tt + SUBLANES, D_LRU), F32), pltpu.VMEM((Tt, D_LRU), F32),
                        pltpu.VMEM((SUBLANES, D_LRU), F32)],
        compiler_params=_params(2),
        name="lru",
    )(x, g_mix, w_lru, c0, h0, cw, cb, wbd, ba, bx, lam)


def _split3(v):
    hi = v.astype(BF16)
    r1 = v - hi.astype(F32)
    mid = r1.astype(BF16)
    lo = (r1 - mid.astype(F32)).astype(BF16)
    return hi, mid, lo


def _pad_time(v, rows):
    if v.shape[0] == rows:
        return v
    return jnp.concatenate([v, jnp.zeros((rows - v.shape[0], v.shape[1]), v.dtype)], axis=0)


def _ssd_masks(q):
    P = SSD_HEADDIM
    tri = jnp.arange(q)[:, None] >= jnp.arange(q)[None, :]
    expand = jnp.arange(LANES)[:, None] == jnp.arange(D_SSM)[None, :] // P
    row_q = jnp.arange(q)[:, None]
    lane_k = jnp.arange(D_SSM)[None, :] % P
    diag = row_q == lane_k
    causal = row_q >= lane_k
    bd = jnp.arange(GATE_TILE)[:, None] // P == jnp.arange(GATE_TILE)[None, :] // P
    return (tri.astype(BF16), expand.astype(BF16), diag.astype(F32), causal.astype(F32), bd.astype(BF16))


def _ssd_body(x_ref, gm_ref, w_ref, c0_ref, s0_ref, cw_ref, cb_ref, dtb_ref, alog_ref, dvec_ref, ng_ref,
              tri_ref, expand_ref, diag_ref, causal_ref, bd_ref,
              y_ref, sN_ref, cN_ref, xpad, xa_s, y_s, st_s, dt_s, zs_s, stn_s, ecs_s, cdec_s, *, Tt, q):
    t = pl.program_id(1)
    _carry_history(t, xpad, c0_ref, Tt)

    @pl.when(t == 0)
    def _():
        st_s[...] = s0_ref[0].reshape(D_SSM, D_STATE).T

    xb = _rmsnorm(x_ref[0], gm_ref[...]).astype(BF16)
    for j in range(D_XBC // GATE_TILE):
        sl = slice(GATE_TILE * j, GATE_TILE * (j + 1))
        xpad[SUBLANES:SUBLANES + Tt, sl] = _project(xb, w_ref, D_SSM + sl.start, D_SSM + sl.stop)
        xa_s[:, sl] = jax.nn.silu(_conv_block(xpad, cw_ref, cb_ref, Tt, sl))
    for j in range(D_SSM // GATE_TILE):
        sl = slice(GATE_TILE * j, GATE_TILE * (j + 1))
        zs_s[:, sl] = jax.nn.silu(_project(xb, w_ref, sl.start, sl.stop))
    dt_s[...] = _project(xb, w_ref, D_SSM + D_XBC, D_SSM + D_XBC + LANES)
    A = -jnp.exp(alog_ref[...])
    P = SSD_HEADDIM
    blk = GATE_TILE // P
    off_b = D_SSM
    off_c = D_SSM + SSD_GROUPS * D_STATE
    gw = D_SSM // SSD_GROUPS

    def exact01(parts, w01, left):
        one = (lambda p: jnp.dot(w01, p, preferred_element_type=F32)) if left else (
            lambda p: jnp.dot(p, w01, preferred_element_type=F32))
        hi, mid, lo = parts
        return (one(lo) + one(mid)) + one(hi)

    n_chunks = Tt // q
    nb = min(8, n_chunks)

    def within_chunks(cb, carry):
        cidx = [cb * nb + i for i in range(nb)]
        r0s = [pl.multiple_of(c * q, q) for c in cidx]
        xs = [xa_s[pl.ds(r0, q), 0:D_SSM] for r0 in r0s]
        dts = [jax.nn.softplus(dt_s[pl.ds(r0, q), :] + dtb_ref[...]) for r0 in r0s]
        css = [exact01(_split3(dt * A), tri_ref[...], left=True) for dt in dts]
        Es = [exact01(_split3(jnp.concatenate([cs, dt], axis=0)), expand_ref[...], left=False)
              for cs, dt in zip(css, dts)]
        diag = diag_ref[...] != 0.0
        causal = causal_ref[...] != 0.0
        Bgs = [[xa_s[pl.ds(r0, q), off_b + g * D_STATE:off_b + (g + 1) * D_STATE].astype(BF16)
                for g in range(SSD_GROUPS)] for r0 in r0s]
        Cgs = [[xa_s[pl.ds(r0, q), off_c + g * D_STATE:off_c + (g + 1) * D_STATE].astype(BF16)
                for g in range(SSD_GROUPS)] for r0 in r0s]
        CBs = [jnp.concatenate(
            [lax.dot_general(Cg, jnp.concatenate([_pad_time(Bg, P)] * SSD_HPG, axis=0), _NT,
                             preferred_element_type=F32) for Bg, Cg in zip(Bgc, Cgc)], axis=1)
            for Bgc, Cgc in zip(Bgs, Cgs)]
        xws = []
        for i in range(nb):
            E_cs, E_dt = Es[i][0:q], Es[i][q:2 * q]
            cs_last = E_cs[q - 1:q, :]
            xws.append((jnp.exp(cs_last - E_cs) * E_dt * xs[i]).astype(BF16))
            ecs_s[pl.ds(r0s[i], q), :] = jnp.exp(E_cs)
            cdec_s[pl.ds(pl.multiple_of(cidx[i] * SUBLANES, SUBLANES), SUBLANES), :] = jnp.broadcast_to(
                jnp.exp(cs_last), (SUBLANES, D_SSM))
        for i in range(nb):
            for g in range(SSD_GROUPS):
                sl = slice(g * gw, (g + 1) * gw)
                stn_s[cidx[i], :, sl] = lax.dot_general(Bgs[i][g], xws[i][:, sl], _TN, preferred_element_type=F32)
        Mws = []
        for i in range(nb):
            E_cs, E_dt = Es[i][0:q], Es[i][q:2 * q]
            r_cs = jnp.sum(jnp.where(diag, E_cs, 0.0), axis=0, keepdims=True)
            r_dt = jnp.sum(jnp.where(diag, E_dt, 0.0), axis=0, keepdims=True)
            Lm = jnp.where(causal, jnp.exp(jnp.where(causal, E_cs - r_cs, 0.0)), 0.0)
            Mws.append((CBs[i] * Lm * r_dt).astype(BF16))
        for i in range(nb):
            xsb = xs[i].astype(BF16)
            for j in range(D_SSM // GATE_TILE):
                sl = slice(j * GATE_TILE, (j + 1) * GATE_TILE)
                slab = _pad_time(xsb[:, sl], P)
                rhs = jnp.concatenate([slab] * blk, axis=0) * bd_ref[...]
                y_s[pl.ds(r0s[i], q), sl] = jnp.dot(Mws[i][:, sl], rhs, preferred_element_type=F32)
        return carry

    def across_chunks(c, carry):
        r0 = pl.multiple_of(c * q, q)
        cdec = cdec_s[pl.ds(pl.multiple_of(c * SUBLANES, SUBLANES), 1), :]
        for g in range(SSD_GROUPS):
            sl = slice(g * gw, (g + 1) * gw)
            Cg = xa_s[pl.ds(r0, q), off_c + g * D_STATE:off_c + (g + 1) * D_STATE].astype(BF16)
            S = st_s[:, sl]
            yo = jnp.dot(Cg, S.astype(BF16), preferred_element_type=F32)
            y_s[pl.ds(r0, q), sl] = y_s[pl.ds(r0, q), sl] + yo * ecs_s[pl.ds(r0, q), sl]
            st_s[:, sl] = cdec[:, sl] * S + stn_s[c, :, sl]
        return carry

    lax.fori_loop(0, n_chunks // nb, within_chunks, 0)
    lax.fori_loop(0, n_chunks, across_chunks, 0, unroll=True)

    @pl.when(t == pl.num_programs(1) - 1)
    def _():
        sN_ref[0] = st_s[...].T.reshape(SSD_HEADS, SSD_HEADDIM, D_STATE)

    y = y_s[...] + dvec_ref[...] * xa_s[:, 0:D_SSM]
    y = y * zs_s[...]
    gw = D_SSM // SSD_GROUPS
    for g in range(SSD_GROUPS):
        sl = slice(g * gw, (g + 1) * gw)
        yg = y[:, sl]
        yg = yg * lax.rsqrt(jnp.mean(yg * yg, axis=-1, keepdims=True) + EPS)
        y_ref[0, :, sl] = (yg * ng_ref[:, sl]).astype(BF16)
    cN_ref[0] = xpad[Tt:Tt + SUBLANES, :]


def _ssd(x, g_mix, w_ssd, c0, s0, cw, cb, dtb, alog, dvec, ng, Tt, q):
    B, L, _ = x.shape
    seq = lambda w: pl.BlockSpec((1, Tt, w), lambda b, t: (b, t, 0))
    per_b = pl.BlockSpec((1, SUBLANES, D_XBC), lambda b, t: (b, 0, 0))
    st = pl.BlockSpec((1, SSD_HEADS, SSD_HEADDIM, D_STATE), lambda b, t: (b, 0, 0, 0))
    masks = _ssd_masks(q)
    return pl.pallas_call(
        functools.partial(_ssd_body, Tt=Tt, q=q),
        grid=(B, L // Tt),
        in_specs=[seq(D_MODEL), _full((1, D_MODEL)), _full(w_ssd.shape),
                  per_b, st, _full((CONV_W, D_XBC)), _full((1, D_XBC)),
                  _full((1, LANES)), _full((1, LANES)), _full((1, D_SSM)), _full((1, D_SSM))]
                 + [_full(m.shape) for m in masks],
        out_specs=[seq(D_SSM), st, per_b],
        out_shape=[jax.ShapeDtypeStruct((B, L, D_SSM), BF16),
                   jax.ShapeDtypeStruct((B, SSD_HEADS, SSD_HEADDIM, D_STATE), F32),
                   jax.ShapeDtypeStruct((B, SUBLANES, D_XBC), F32)],
        scratch_shapes=[pltpu.VMEM((Tt + SUBLANES, D_XBC), F32), pltpu.VMEM((Tt, D_XBC), F32),
                        pltpu.VMEM((Tt, D_SSM), F32), pltpu.VMEM((D_STATE, D_SSM), F32),
                        pltpu.VMEM((Tt, LANES), F32), pltpu.VMEM((Tt, D_SSM), F32),
                        pltpu.VMEM((Tt // q, D_STATE, D_SSM), F32), pltpu.VMEM((Tt, D_SSM), F32),
                        pltpu.VMEM((Tt // q * SUBLANES, D_SSM), F32)],
        compiler_params=_params(2),
        name="ssd",
    )(x, g_mix, w_ssd, c0, s0, cw, cb, dtb, alog, dvec, ng, *masks)


def _out_router_body(x_ref, yl_ref, ys_ref, wo_ref, gf_ref, wr_ref, br_ref, cin_ref,
                     h1_ref, xn_ref, meta_ref, gate_ref, cnt_ref, carry, *, tm):
    step = pl.program_id(0)

    @pl.when(step == 0)
    def _():
        carry[...] = cin_ref[...]

    mix = jnp.concatenate([yl_ref[...], ys_ref[...]], axis=1)
    h1 = x_ref[...] + jnp.dot(mix, wo_ref[...], preferred_element_type=F32)
    h1_ref[...] = h1
    xn = _rmsnorm(h1, gf_ref[...])
    xn_ref[...] = xn

    xh = xn.astype(BF16)
    xm = (xn - xh.astype(F32)).astype(BF16)
    hh_hm = jnp.dot(xh, wr_ref[...], preferred_element_type=F32)
    small = hh_hm[:, LANES:] + jnp.dot(xm, wr_ref[:, 0:LANES], preferred_element_type=F32)
    logits = small + hh_hm[:, 0:LANES] + br_ref[...]
    lt = logits.T
    rows8 = lax.broadcasted_iota(I32, (SUBLANES, tm), 0)
    lg = jnp.where(rows8 < MOE_GROUPS, lt[0:SUBLANES, :], -jnp.inf)
    eg = jnp.exp(lg - jnp.max(lg, axis=0, keepdims=True))
    pg = eg / jnp.sum(eg, axis=0, keepdims=True)
    pgs = jnp.max(pg, axis=0, keepdims=True)
    rows8f = rows8.astype(F32)
    gsel = jnp.min(jnp.where(pg == pgs, rows8f, float(SUBLANES)), axis=0, keepdims=True)

    rows32 = lax.broadcasted_iota(I32, (N_EXPERTS, tm), 0)
    rows32f = rows32.astype(F32)
    grp = (rows32 // EXPERTS_PER_GROUP).astype(F32)
    le = lt[ROUTER_E0:ROUTER_E0 + N_EXPERTS, :]
    ing = grp == gsel
    lem = jnp.where(ing, le, -jnp.inf)
    ee = jnp.exp(lem - jnp.max(lem, axis=0, keepdims=True))
    pe = ee / jnp.sum(ee, axis=0, keepdims=True)
    pe1 = jnp.where(ing, pe, -1.0)
    v1 = jnp.max(pe1, axis=0, keepdims=True)
    i1 = jnp.min(jnp.where(pe1 == v1, rows32f, float(N_EXPERTS)), axis=0, keepdims=True)
    pe2 = jnp.where(rows32f == i1, -1.0, pe1)
    v2 = jnp.max(pe2, axis=0, keepdims=True)
    i2 = jnp.min(jnp.where(pe2 == v2, rows32f, float(N_EXPERTS)), axis=0, keepdims=True)
    sv = v1 + v2
    w1 = v1 / sv * pgs
    w2 = v2 / sv * pgs

    oh1 = rows32f == i1
    oh2 = rows32f == i2
    oh = jnp.where(oh1 | oh2, 1.0, 0.0)
    before = (lax.broadcasted_iota(I32, (tm, tm), 0) < lax.broadcasted_iota(I32, (tm, tm), 1))
    pref = jnp.dot(oh.astype(BF16), jnp.where(before, 1.0, 0.0).astype(BF16), preferred_element_type=F32)
    pref = pref + carry[:, 0:1]
    r1 = jnp.sum(jnp.where(oh1, pref, 0.0), axis=0, keepdims=True)
    r2 = jnp.sum(jnp.where(oh2, pref, 0.0), axis=0, keepdims=True)
    carry[...] = carry[...] + jnp.sum(oh, axis=1, keepdims=True)
    cnt_ref[...] = carry[...]

    meta = jnp.where(rows8 == 0, i1, jnp.where(rows8 == 1, i2, jnp.where(rows8 == 2, r1, jnp.where(rows8 == 3, r2, 0.0))))
    meta_ref[...] = meta.astype(I32)
    gate_ref[...] = jnp.where(rows8 == 0, w1, jnp.where(rows8 == 1, w2, 0.0))


def _out_router(x2d, y_lru, y_ssd, w_out, gf, wr, br, counts_in, tm):
    T = x2d.shape[0]
    row = lambda w: pl.BlockSpec((tm, w), lambda i: (i, 0))
    col = pl.BlockSpec((SUBLANES, tm), lambda i: (0, i))
    return pl.pallas_call(
        functools.partial(_out_router_body, tm=tm),
        grid=(T // tm,),
        in_specs=[row(D_MODEL), row(D_LRU), row(D_SSM), _full((D_LRU + D_SSM, D_MODEL)), _full((1, D_MODEL)),
                  _full((D_MODEL, 2 * LANES)), _full((1, LANES)), _full((N_EXPERTS, LANES))],
        out_specs=[row(D_MODEL), row(D_MODEL), col, col, _full((N_EXPERTS, LANES))],
        out_shape=[jax.ShapeDtypeStruct((T, D_MODEL), F32), jax.ShapeDtypeStruct((T, D_MODEL), F32),
                   jax.ShapeDtypeStruct((SUBLANES, T), I32), jax.ShapeDtypeStruct((SUBLANES, T), F32),
                   jax.ShapeDtypeStruct((N_EXPERTS, LANES), F32)],
        scratch_shapes=[pltpu.VMEM((N_EXPERTS, LANES), F32)],
        compiler_params=_params(1),
        name="out_router",
    )(x2d, y_lru, y_ssd, w_out, gf, wr, br, counts_in)


def _row_copy(src_hbm, src_row, dst, dst_row, sem):
    return pltpu.make_async_copy(src_hbm.at[pl.ds(src_row, 1)], dst.at[pl.ds(dst_row, 1)], sem)


def _positions_body(offs_ref, meta_ref, pos_ref):
    m = meta_ref[...]
    base = jnp.zeros_like(m)
    for e in range(N_EXPERTS):
        base = jnp.where(m == e, offs_ref[e], base)
    pos_ref[...] = base + pltpu.roll(m, SUBLANES - 2, 0)


def _positions(offs, meta):
    T = meta.shape[1]
    tb = min(T, 4096)
    blk = pl.BlockSpec((SUBLANES, tb), lambda i, offs: (0, i))
    return pl.pallas_call(
        _positions_body,
        grid_spec=pltpu.PrefetchScalarGridSpec(num_scalar_prefetch=1, grid=(T // tb,), in_specs=[blk], out_specs=blk),
        out_shape=jax.ShapeDtypeStruct((SUBLANES, T), I32),
        compiler_params=_params(1),
        name="positions",
    )(offs, meta)


def _wait_rows(src, dst, sem):
    pltpu.make_async_copy(src, dst, sem).wait()


def _dispatch_body(zf_ref, pos_ref, *rest, tm, tme, steps):
    n_groups = len(steps)
    xn_hbms = rest[:n_groups]
    xs_hbm, zbuf, xbuf, zsem, lsem, ssem = rest[n_groups:]
    n_steps = sum(steps)
    n_tiles = xs_hbm.shape[0] // tme
    ring = xbuf.shape[0]
    step = pl.program_id(0)

    def zero_copy(i):
        return pltpu.make_async_copy(zbuf, xs_hbm.at[pl.ds(pl.multiple_of(i * tme, tme), tme)], zsem)

    def load(i, start):
        slot = i % ring
        first = 0
        for xn_hbm, n in zip(xn_hbms, steps):
            @pl.when(jnp.logical_and(i >= first, i < first + n))
            def _(xn_hbm=xn_hbm, first=first):
                rows = pl.ds(pl.multiple_of((i - first) * tm, tm), tm)
                cp = pltpu.make_async_copy(xn_hbm.at[rows], xbuf.at[slot], lsem.at[slot])
                if start:
                    cp.start()
                else:
                    cp.wait()
            first += n

    def wait_scatter(i):
        slot = i % ring
        for k in range(2):
            _wait_rows(xbuf.at[slot], xs_hbm.at[pl.ds(0, tm)], ssem.at[slot])

    @pl.when(step == 0)
    def _():
        zbuf[...] = jnp.zeros_like(zbuf)

        def start(i, c):
            @pl.when(zf_ref[i] != 0)
            def _():
                zero_copy(i).start()
            return c

        def wait(i, c):
            @pl.when(zf_ref[i] != 0)
            def _():
                zero_copy(i).wait()
            return c

        lax.fori_loop(0, n_tiles, start, 0)
        lax.fori_loop(0, n_tiles, wait, 0)
        load(step, True)

    @pl.when(step + 1 < n_steps)
    def _():
        @pl.when(step >= ring - 1)
        def _():
            wait_scatter(step - (ring - 1))
        load(step + 1, True)

    load(step, False)
    slot = step % ring
    for t in range(tm):
        for k in range(2):
            _row_copy(xbuf.at[slot], t, xs_hbm, pos_ref[k, t], ssem.at[slot]).start(priority=k)

    @pl.when(step == n_steps - 1)
    def _():
        for j in range(max(n_steps - ring, 0), n_steps):
            wait_scatter(j)


def _dispatch(zflag, pos, xns, n_rows, tm, tme):
    steps = tuple(xn.shape[0] // tm for xn in xns)
    ring = 3
    return pl.pallas_call(
        functools.partial(_dispatch_body, tm=tm, tme=tme, steps=steps),
        grid_spec=pltpu.PrefetchScalarGridSpec(
            num_scalar_prefetch=1,
            grid=(sum(steps),),
            in_specs=[pl.BlockSpec((SUBLANES, tm), lambda i, zf: (0, i), memory_space=pltpu.SMEM)]
                     + [pl.BlockSpec(memory_space=pl.ANY) for _ in xns],
            out_specs=pl.BlockSpec(memory_space=pl.ANY),
            scratch_shapes=[pltpu.VMEM((tme, D_MODEL), F32), pltpu.VMEM((ring, tm, D_MODEL), F32),
                            pltpu.SemaphoreType.DMA(()), pltpu.SemaphoreType.DMA((ring,)),
                            pltpu.SemaphoreType.DMA((ring,))],
        ),
        out_shape=jax.ShapeDtypeStruct((n_rows, D_MODEL), F32),
        compiler_params=_params(1),
        name="dispatch",
    )(zflag, pos, *xns)


def _experts_body(te_ref, na_ref, x_ref, w1_ref, w3_ref, w2_ref, o_ref, w1b, w3b, w2b):
    i = pl.program_id(0)
    changed = jnp.logical_or(i == 0, te_ref[i] != te_ref[jnp.maximum(i - 1, 0)])

    @pl.when(changed)
    def _():
        w1b[...] = w1_ref[0].astype(BF16)
        w3b[...] = w3_ref[0].astype(BF16)
        w2b[...] = w2_ref[0].astype(BF16)

    @pl.when(i < na_ref[0])
    def _():
        xb = x_ref[...].astype(BF16)
        a = jnp.dot(xb, w1b[...], preferred_element_type=F32)
        b = jnp.dot(xb, w3b[...], preferred_element_type=F32)
        hd = (jax.nn.silu(a) * b).astype(BF16)
        o_ref[...] = jnp.dot(hd, w2b[...], preferred_element_type=F32)

    @pl.when(i >= na_ref[0])
    def _():
        o_ref[...] = jnp.zeros_like(o_ref)


def _experts(tile_e, n_active, xs, w1, w3, w2, tme):
    n_rows = xs.shape[0]
    row = pl.BlockSpec((tme, D_MODEL), lambda i, te, na: (i, 0))
    return pl.pallas_call(
        _experts_body,
        grid_spec=pltpu.PrefetchScalarGridSpec(
            num_scalar_prefetch=2,
            grid=(n_rows // tme,),
            in_specs=[row,
                      pl.BlockSpec((1, D_MODEL, D_EXPERT), lambda i, te, na: (te[i], 0, 0)),
                      pl.BlockSpec((1, D_MODEL, D_EXPERT), lambda i, te, na: (te[i], 0, 0)),
                      pl.BlockSpec((1, D_EXPERT, D_MODEL), lambda i, te, na: (te[i], 0, 0))],
            out_specs=row,
            scratch_shapes=[pltpu.VMEM((D_MODEL, D_EXPERT), BF16), pltpu.VMEM((D_MODEL, D_EXPERT), BF16),
                            pltpu.VMEM((D_EXPERT, D_MODEL), BF16)],
        ),
        out_shape=jax.ShapeDtypeStruct((n_rows, D_MODEL), F32),
        compiler_params=_params(1),
        name="experts",
    )(tile_e, n_active, xs, w1, w3, w2)


def _combine_body(pos_ref, nxt_ref, gate_ref, h1_ref, fg_ref, ys_hbm, y_ref, ybuf, sem, *, tm):
    step = pl.program_id(0)
    slot = step % 2

    def gather(p_ref, s):
        for t in range(tm):
            for k in range(2):
                _row_copy(ys_hbm, p_ref[k, t], ybuf.at[s, k], t, sem.at[s]).start(priority=k)

    @pl.when(step == 0)
    def _():
        gather(pos_ref, 0)

    @pl.when(step + 1 < pl.num_programs(0))
    def _():
        gather(nxt_ref, 1 - slot)

    for k in range(2):
        _wait_rows(ys_hbm.at[pl.ds(0, tm)], ybuf.at[slot, k], sem.at[slot])

    eye = lax.broadcasted_iota(I32, (tm, tm), 0) == lax.broadcasted_iota(I32, (tm, tm), 1)
    g1 = jnp.sum(jnp.where(eye, gate_ref[0:1, :], 0.0), axis=1, keepdims=True)
    g2 = jnp.sum(jnp.where(eye, gate_ref[1:2, :], 0.0), axis=1, keepdims=True)
    h2 = h1_ref[...] + (g1 * ybuf[slot, 0] + g2 * ybuf[slot, 1])
    y_ref[...] = _rmsnorm(h2, fg_ref[...])


def _combine(pos, gates, h1, fg, ys, tm):
    T = h1.shape[0]
    last = T // tm - 1
    return pl.pallas_call(
        functools.partial(_combine_body, tm=tm),
        grid=(T // tm,),
        in_specs=[pl.BlockSpec((SUBLANES, tm), lambda i: (0, i), memory_space=pltpu.SMEM),
                  pl.BlockSpec((SUBLANES, tm), lambda i: (0, jnp.minimum(i + 1, last)), memory_space=pltpu.SMEM),
                  pl.BlockSpec((SUBLANES, tm), lambda i: (0, i)),
                  pl.BlockSpec((tm, D_MODEL), lambda i: (i, 0)),
                  pl.BlockSpec((1, D_MODEL), lambda i: (0, 0)),
                  pl.BlockSpec(memory_space=pl.ANY)],
        out_specs=pl.BlockSpec((tm, D_MODEL), lambda i: (i, 0)),
        scratch_shapes=[pltpu.VMEM((2, 2, tm, D_MODEL), F32), pltpu.SemaphoreType.DMA((2,))],
        out_shape=jax.ShapeDtypeStruct((T, D_MODEL), F32),
        compiler_params=_params(1),
        name="combine",
    )(pos, pos, gates, h1, fg, ys)


def _blockdiag(w):
    per = GATE_TILE // LRU_BLOCK
    w4 = w.reshape(LRU_BLOCKS // per, per, LRU_BLOCK, LRU_BLOCK)
    eye = jnp.eye(per, dtype=w.dtype)
    return jnp.einsum('jbio,bc->jbico', w4, eye).reshape(LRU_BLOCKS // per, GATE_TILE, GATE_TILE)


def _pad_rows(c):
    return jnp.pad(c, ((0, 0), (SUBLANES - (CONV_W - 1), 0), (0, 0)))


def _router_cols(group_part, expert_part):
    r = group_part.shape[0]
    out = jnp.zeros((r, LANES), F32)
    out = out.at[:, 0:MOE_GROUPS].set(group_part)
    return out.at[:, ROUTER_E0:ROUTER_E0 + N_EXPERTS].set(expert_part)


def _lane_row(v, width=LANES):
    return jnp.pad(v, (0, width - v.shape[0])).reshape(1, width)


def _prep(norm_mix_g, w_in, lru_conv_w, lru_conv_b, lru_wa, lru_ba, lru_wx, lru_bx, lru_lambda,
          ssd_conv_w, ssd_conv_b, ssd_dt_bias, ssd_a_log, ssd_d, ssd_norm_g, w_out,
          norm_ffn_g, router_group_w, router_group_b, router_expert_w, router_expert_b,
          moe_w1, moe_w3, moe_w2, final_norm_g):
    w = w_in[0]
    wr = _router_cols(router_group_w[0], router_expert_w[0])
    wr_hi = wr.astype(BF16)
    P = dict(
        g_mix=norm_mix_g[0].reshape(1, D_MODEL),
        w_lru=w[:, :2 * D_LRU].astype(BF16),
        w_ssd=jnp.pad(w[:, 2 * D_LRU:], ((0, 0), (0, LANES - SSD_HEADS))).astype(BF16),
        lru_cw=lru_conv_w[0], lru_cb=lru_conv_b[0].reshape(1, D_LRU),
        wbd=jnp.concatenate([_blockdiag(lru_wa[0]), _blockdiag(lru_wx[0])], axis=2).astype(BF16),
        ba=lru_ba[0].reshape(1, D_LRU), bx=lru_bx[0].reshape(1, D_LRU), lam=lru_lambda[0].reshape(1, D_LRU),
        ssd_cw=ssd_conv_w[0], ssd_cb=ssd_conv_b[0].reshape(1, D_XBC),
        dtb=_lane_row(ssd_dt_bias[0]), alog=_lane_row(ssd_a_log[0]),
        dvec=jnp.repeat(ssd_d[0], SSD_HEADDIM).reshape(1, D_SSM),
        ng=ssd_norm_g[0].reshape(1, D_SSM),
        w_out=w_out[0].astype(BF16),
        g_ffn=norm_ffn_g[0].reshape(1, D_MODEL),
        wr=jnp.concatenate([wr_hi, (wr - wr_hi.astype(F32)).astype(BF16)], axis=1),
        br=_router_cols(router_group_b[0][None], router_expert_b[0][None]),
        w1=moe_w1[0], w3=moe_w3[0], w2=moe_w2[0],
        g_final=final_norm_g.reshape(1, D_MODEL),
    )
    return P


def _expert_layout(counts, n_pairs, tme):
    cnt = counts[:, 0].astype(I32)
    padded = ((cnt + tme - 1) // tme) * tme
    ends = jnp.cumsum(padded)
    offs = ends - padded
    n_tiles = n_pairs // tme + N_EXPERTS
    n_active = ends[-1] // tme
    tiles = jnp.arange(n_tiles, dtype=I32)
    tile_e = jnp.sum((tiles * tme)[:, None] >= ends[None, :], axis=1).astype(I32)
    last_e = jnp.sum((n_active - 1) * tme >= ends).astype(I32)
    tile_e = jnp.where(tiles < n_active, tile_e, last_e)
    is_last = jnp.any((tiles[:, None] + 1) * tme == ends[None, :], axis=1)
    zflag = jnp.logical_or(is_last, tiles >= n_active).astype(I32)
    return offs.astype(I32), tile_e, n_active.reshape(1).astype(I32), zflag, n_tiles * tme


def _mixer_router(x, lru_h0, lru_c0, ssd_h0, ssd_c0, P, start_pos, counts_in):
    B, L, _ = x.shape
    T = B * L
    Tt = min(MIX_TILE, L)
    q = min(SSD_CHUNK, L)
    tm = min(ROW_TILE, T)
    x2d = x.reshape(T, D_MODEL)

    y_lru, lru_h, lru_c = _lru(x, P['g_mix'], P['w_lru'], _pad_rows(lru_c0), lru_h0.reshape(B, 1, D_LRU),
                               P['lru_cw'], P['lru_cb'], P['wbd'], P['ba'], P['bx'], P['lam'], Tt, start_pos)
    y_ssd, ssd_h, ssd_c = _ssd(x, P['g_mix'], P['w_ssd'], _pad_rows(ssd_c0), ssd_h0,
                               P['ssd_cw'], P['ssd_cb'], P['dtb'], P['alog'], P['dvec'], P['ng'], Tt, q)
    h1, xn, meta, gates, counts = _out_router(
        x2d, y_lru.reshape(T, D_LRU), y_ssd.reshape(T, D_SSM), P['w_out'], P['g_ffn'],
        P['wr'], P['br'], counts_in, min(MIX_TILE, T))

    hist = SUBLANES - (CONV_W - 1)
    states = (lru_h.reshape(1, B, D_LRU), lru_c[:, hist:][None], ssd_h[None], ssd_c[:, hist:][None])
    return dict(h1=h1, xn=xn, meta=meta, gates=gates, shape=(B, L, D_MODEL), tm=tm), counts, states


def _moe_final(groups, counts, P):
    tme = EXPERT_TILE
    n_pairs = 2 * sum(g['h1'].shape[0] for g in groups)
    offs, tile_e, n_active, zflag, n_rows = _expert_layout(counts, n_pairs, tme)
    tm = groups[0]['tm']
    assert all(g['tm'] == tm for g in groups)
    for g in groups:
        g['pos'] = _positions(offs, g['meta'])
    pos_all = jnp.concatenate([g['pos'] for g in groups], axis=1)
    xs = _dispatch(zflag, pos_all, [g['xn'] for g in groups], n_rows, tm, tme)
    ys = _experts(tile_e, n_active, xs, P['w1'], P['w3'], P['w2'], tme)
    return [_combine(g['pos'], g['gates'], g['h1'], P['g_final'], ys, g['tm']).reshape(g['shape'])
            for g in groups]


def kernel(x_prompt, x_sample, state_lru_h, state_lru_conv, state_ssd, state_ssd_conv, norm_mix_g, w_in, lru_conv_w, lru_conv_b, lru_wa, lru_ba, lru_wx, lru_bx, lru_lambda, ssd_conv_w, ssd_conv_b, ssd_dt_bias, ssd_a_log, ssd_d, ssd_norm_g, w_out, norm_ffn_g, router_group_w, router_group_b, router_expert_w, router_expert_b, moe_w1, moe_w3, moe_w2, final_norm_g):
    P = _prep(norm_mix_g, w_in, lru_conv_w, lru_conv_b, lru_wa, lru_ba, lru_wx, lru_bx, lru_lambda,
              ssd_conv_w, ssd_conv_b, ssd_dt_bias, ssd_a_log, ssd_d, ssd_norm_g, w_out,
              norm_ffn_g, router_group_w, router_group_b, router_expert_w, router_expert_b,
              moe_w1, moe_w3, moe_w2, final_norm_g)
    bp = x_prompt.shape[0]
    gp, counts, (a1, a2, a3, a4) = _mixer_router(
        x_prompt,
        jnp.zeros((bp, D_LRU), F32), jnp.zeros((bp, CONV_W - 1, D_LRU), F32),
        jnp.zeros((bp, SSD_HEADS, SSD_HEADDIM, D_STATE), F32), jnp.zeros((bp, CONV_W - 1, D_XBC), F32),
        P, 0, jnp.zeros((N_EXPERTS, LANES), F32))
    gs, counts, (b1, b2, b3, b4) = _mixer_router(
        x_sample, state_lru_h[0], state_lru_conv[0], state_ssd[0], state_ssd_conv[0], P, PAST_LEN, counts)
    yp, ys = _moe_final([gp, gs], counts, P)
    return (yp, ys, a1, a2, a3, a4, b1, b2, b3, b4)
```

```python
import functools

import jax
import jax.numpy as jnp
from jax import lax
from jax.experimental import pallas as pl
from jax.experimental.pallas import tpu as pltpu

F32 = jnp.float32
BF16 = jnp.bfloat16
I32 = jnp.int32

D_MODEL = 1024
D_LRU = 1024
LRU_BLOCKS = 16
LRU_BLOCK = 64
LRU_C = 8.0
CONV_W = 4
D_SSM = 1024
SSD_HEADDIM = 64
SSD_HEADS = 16
SSD_GROUPS = 2
SSD_HPG = 8
D_STATE = 128
D_XBC = 1536
MOE_GROUPS = 4
EXPERTS_PER_GROUP = 8
N_EXPERTS = 32
D_EXPERT = 512
EPS = 1e-6
SSD_CHUNK = 64
PAST_LEN = 1024

LANES = 128
SUBLANES = 8
GATE_TILE = 256
ROW_TILE = 256
MIX_TILE = 512
EXPERT_TILE = 512
ROUTER_E0 = 32
VMEM_LIMIT = 52 * 1024 * 1024

_NT = (((1,), (1,)), ((), ()))
_TN = (((0,), (0,)), ((), ()))


def _params(n_axes):
    return pltpu.CompilerParams(dimension_semantics=("arbitrary",) * n_axes,
                                vmem_limit_bytes=VMEM_LIMIT)


def _rmsnorm(x, g):
    return x * lax.rsqrt(jnp.mean(x * x, axis=-1, keepdims=True) + EPS) * g


def _full(shape):
    n = len(shape)
    return pl.BlockSpec(shape, lambda *_: (0,) * n)


def _project(xb, w_ref, lo, hi):
    return jnp.dot(xb, w_ref[:, lo:hi], preferred_element_type=F32)


def _conv_block(xpad, cw_ref, cb_ref, Tt, sl):
    cw = cw_ref[:, sl]
    full = xpad[:, sl]
    y = cb_ref[:, sl]
    for k in range(CONV_W):
        shift = CONV_W - 1 - k
        xk = pltpu.roll(full, shift, 0) if shift else full
        y = y + xk[SUBLANES:SUBLANES + Tt, :] * cw[k:k + 1, :]
    return y


def _carry_history(t, xpad, c0_ref, Tt):
    @pl.when(t == 0)
    def _():
        xpad[0:SUBLANES, :] = c0_ref[0]

    @pl.when(t > 0)
    def _():
        xpad[0:SUBLANES, :] = xpad[Tt:Tt + SUBLANES, :]


def _gelu_tanh(x):
    c = 0.7978845608028654
    t = jnp.tanh(x * (c + (c * 0.044715) * (x * x)))
    hx = 0.5 * x
    return hx + hx * t


def _lru_body(x_ref, gm_ref, w_ref, c0_ref, h0_ref, cw_ref, cb_ref, wbd_ref, ba_ref, bx_ref, lam_ref,
              y_ref, hN_ref, cN_ref, xpad, y_s, hcar, *, Tt, start_pos):
    t = pl.program_id(1)
    _carry_history(t, xpad, c0_ref, Tt)

    @pl.when(t == 0)
    def _():
        hcar[...] = jnp.broadcast_to(h0_ref[0], (SUBLANES, D_LRU))

    xb = _rmsnorm(x_ref[0], gm_ref[...]).astype(BF16)
    sp = jax.nn.softplus(-lam_ref[...])
    rows = lax.broadcasted_iota(I32, (SUBLANES, GATE_TILE), 0)
    for j in range(D_LRU // GATE_TILE):
        sl = slice(GATE_TILE * j, GATE_TILE * (j + 1))
        xpad[SUBLANES:SUBLANES + Tt, sl] = _project(xb, w_ref, sl.start, sl.stop)
        xc = _conv_block(xpad, cw_ref, cb_ref, Tt, sl)
        ga = jnp.dot(xc.astype(BF16), wbd_ref[j], preferred_element_type=F32)
        r = jax.nn.sigmoid(ga[:, :GATE_TILE] + ba_ref[:, sl])
        i = jax.nn.sigmoid(ga[:, GATE_TILE:] + bx_ref[:, sl])
        a = jnp.exp((-LRU_C * r) * sp[:, sl])
        y1 = 1.0 - a * a
        mult = jnp.where(y1 > 0.0, y1 * lax.rsqrt(y1), 0.0)
        u = mult * i * xc
        gel = _gelu_tanh(_project(xb, w_ref, D_LRU + sl.start, D_LRU + sl.stop))
        h = hcar[:, sl]
        for gi in range(Tt // SUBLANES):
            rs = slice(gi * SUBLANES, (gi + 1) * SUBLANES)
            a8, u8 = a[rs], u[rs]
            if gi == 0 and start_pos == 0:
                u8 = jnp.where(jnp.logical_and(rows == 0, t == 0), i[rs] * xc[rs], u8)
            for s in (1, 2, 4):
                ok = rows >= s
                u_sh = pltpu.roll(u8, s, 0)
                a_sh = pltpu.roll(a8, s, 0)
                u8 = jnp.where(ok, u8 + a8 * u_sh, u8)
                a8 = jnp.where(ok, a8 * a_sh, a8)
            h8 = u8 + a8 * h
            y_s[rs, sl] = h8 * gel[rs]
            h = jnp.broadcast_to(h8[SUBLANES - 1:SUBLANES, :], (SUBLANES, GATE_TILE))
        hcar[:, sl] = h

    y_ref[0] = y_s[...].astype(BF16)
    hN_ref[0] = hcar[0:1, :]
    cN_ref[0] = xpad[Tt:Tt + SUBLANES, :]


def _lru(x, g_mix, w_lru, c0, h0, cw, cb, wbd, ba, bx, lam, Tt, start_pos):
    B, L, _ = x.shape
    seq = pl.BlockSpec((1, Tt, D_LRU), lambda b, t: (b, t, 0))
    per_b = lambda r: pl.BlockSpec((1, r, D_LRU), lambda b, t: (b, 0, 0))
    return pl.pallas_call(
        functools.partial(_lru_body, Tt=Tt, start_pos=start_pos),
        grid=(B, L // Tt),
        in_specs=[pl.BlockSpec((1, Tt, D_MODEL), lambda b, t: (b, t, 0)), _full((1, D_MODEL)), _full(w_lru.shape),
                  per_b(SUBLANES), per_b(1), _full((CONV_W, D_LRU)), _full((1, D_LRU)),
                  _full(wbd.shape), _full((1, D_LRU)), _full((1, D_LRU)), _full((1, D_LRU))],
        out_specs=[seq, per_b(1), per_b(SUBLANES)],
        out_shape=[jax.ShapeDtypeStruct((B, L, D_LRU), BF16),
                   jax.ShapeDtypeStruct((B, 1, D_LRU), F32),
                   jax.ShapeDtypeStruct((B, SUBLANES, D_LRU), F32)],
        scratch_shapes=[pltpu.VMEM((Tt + SUBLANES, D_LRU), F32), pltpu.VMEM((Tt, D_LRU), F32),
                        pltpu.VMEM((SUBLANES, D_LRU), F32)],
        compiler_params=_params(2),
        name="lru",
    )(x, g_mix, w_lru, c0, h0, cw, cb, wbd, ba, bx, lam)


def _split3(v):
    hi = v.astype(BF16)
    r1 = v - hi.astype(F32)
    mid = r1.astype(BF16)
    lo = (r1 - mid.astype(F32)).astype(BF16)
    return hi, mid, lo


def _pad_time(v, rows):
    if v.shape[0] == rows:
        return v
    return jnp.concatenate([v, jnp.zeros((rows - v.shape[0], v.shape[1]), v.dtype)], axis=0)


def _ssd_masks(q):
    P = SSD_HEADDIM
    tri = jnp.arange(q)[:, None] >= jnp.arange(q)[None, :]
    expand = jnp.arange(LANES)[:, None] == jnp.arange(D_SSM)[None, :] // P
    row_q = jnp.arange(q)[:, None]
    lane_k = jnp.arange(D_SSM)[None, :] % P
    diag = row_q == lane_k
    causal = row_q >= lane_k
    bd = jnp.arange(GATE_TILE)[:, None] // P == jnp.arange(GATE_TILE)[None, :] // P
    return (tri.astype(BF16), expand.astype(BF16), diag.astype(F32), causal.astype(F32), bd.astype(BF16))


def _ssd_body(x_ref, gm_ref, w_ref, c0_ref, s0_ref, cw_ref, cb_ref, dtb_ref, alog_ref, dvec_ref, ng_ref,
              tri_ref, expand_ref, diag_ref, causal_ref, bd_ref,
              y_ref, sN_ref, cN_ref, xpad, xa_s, y_s, st_s, dt_s, zs_s, stn_s, ecs_s, cdec_s, *, Tt, q):
    t = pl.program_id(1)
    _carry_history(t, xpad, c0_ref, Tt)

    @pl.when(t == 0)
    def _():
        st_s[...] = s0_ref[0].reshape(D_SSM, D_STATE).T

    xb = _rmsnorm(x_ref[0], gm_ref[...]).astype(BF16)
    for j in range(D_XBC // GATE_TILE):
        sl = slice(GATE_TILE * j, GATE_TILE * (j + 1))
        xpad[SUBLANES:SUBLANES + Tt, sl] = _project(xb, w_ref, D_SSM + sl.start, D_SSM + sl.stop)
        xa_s[:, sl] = jax.nn.silu(_conv_block(xpad, cw_ref, cb_ref, Tt, sl))
    for j in range(D_SSM // GATE_TILE):
        sl = slice(GATE_TILE * j, GATE_TILE * (j + 1))
        zs_s[:, sl] = jax.nn.silu(_project(xb, w_ref, sl.start, sl.stop))
    dt_s[...] = _project(xb, w_ref, D_SSM + D_XBC, D_SSM + D_XBC + LANES)
    A = -jnp.exp(alog_ref[...])
    P = SSD_HEADDIM
    blk = GATE_TILE // P
    off_b = D_SSM
    off_c = D_SSM + SSD_GROUPS * D_STATE
    gw = D_SSM // SSD_GROUPS

    def exact01(parts, w01, left):
        one = (lambda p: jnp.dot(w01, p, preferred_element_type=F32)) if left else (
            lambda p: jnp.dot(p, w01, preferred_element_type=F32))
        hi, mid, lo = parts
        return (one(lo) + one(mid)) + one(hi)

    n_chunks = Tt // q
    nb = min(8, n_chunks)

    def within_chunks(cb, carry):
        cidx = [cb * nb + i for i in range(nb)]
        r0s = [pl.multiple_of(c * q, q) for c in cidx]
        xs = [xa_s[pl.ds(r0, q), 0:D_SSM] for r0 in r0s]
        dts = [jax.nn.softplus(dt_s[pl.ds(r0, q), :] + dtb_ref[...]) for r0 in r0s]
        css = [exact01(_split3(dt * A), tri_ref[...], left=True) for dt in dts]
        Es = [exact01(_split3(jnp.concatenate([cs, dt], axis=0)), expand_ref[...], left=False)
              for cs, dt in zip(css, dts)]
        diag = diag_ref[...] != 0.0
        causal = causal_ref[...] != 0.0
        Bgs = [[xa_s[pl.ds(r0, q), off_b + g * D_STATE:off_b + (g + 1) * D_STATE].astype(BF16)
                for g in range(SSD_GROUPS)] for r0 in r0s]
        Cgs = [[xa_s[pl.ds(r0, q), off_c + g * D_STATE:off_c + (g + 1) * D_STATE].astype(BF16)
                for g in range(SSD_GROUPS)] for r0 in r0s]
        CBs = [jnp.concatenate(
            [lax.dot_general(Cg, jnp.concatenate([_pad_time(Bg, P)] * SSD_HPG, axis=0), _NT,
                             preferred_element_type=F32) for Bg, Cg in zip(Bgc, Cgc)], axis=1)
            for Bgc, Cgc in zip(Bgs, Cgs)]
        xws = []
        for i in range(nb):
            E_cs, E_dt = Es[i][0:q], Es[i][q:2 * q]
            cs_last = E_cs[q - 1:q, :]
            xws.append((jnp.exp(cs_last - E_cs) * E_dt * xs[i]).astype(BF16))
            ecs_s[pl.ds(r0s[i], q), :] = jnp.exp(E_cs)
            cdec_s[pl.ds(pl.multiple_of(cidx[i] * SUBLANES, SUBLANES), SUBLANES), :] = jnp.broadcast_to(
                jnp.exp(cs_last), (SUBLANES, D_SSM))
        for i in range(nb):
            for g in range(SSD_GROUPS):
                sl = slice(g * gw, (g + 1) * gw)
                stn_s[cidx[i], :, sl] = lax.dot_general(Bgs[i][g], xws[i][:, sl], _TN, preferred_element_type=F32)
        Mws = []
        for i in range(nb):
            E_cs, E_dt = Es[i][0:q], Es[i][q:2 * q]
            r_cs = jnp.sum(jnp.where(diag, E_cs, 0.0), axis=0, keepdims=True)
            r_dt = jnp.sum(jnp.where(diag, E_dt, 0.0), axis=0, keepdims=True)
            Lm = jnp.where(causal, jnp.exp(jnp.where(causal, E_cs - r_cs, 0.0)), 0.0)
            Mws.append((CBs[i] * Lm * r_dt).astype(BF16))
        for i in range(nb):
            xsb = xs[i].astype(BF16)
            for j in range(D_SSM // GATE_TILE):
                sl = slice(j * GATE_TILE, (j + 1) * GATE_TILE)
                slab = _pad_time(xsb[:, sl], P)
                rhs = jnp.concatenate([slab] * blk, axis=0) * bd_ref[...]
                y_s[pl.ds(r0s[i], q), sl] = jnp.dot(Mws[i][:, sl], rhs, preferred_element_type=F32)
        return carry

    def across_chunks(c, carry):
        r0 = pl.multiple_of(c * q, q)
        cdec = cdec_s[pl.ds(pl.multiple_of(c * SUBLANES, SUBLANES), 1), :]
        for g in range(SSD_GROUPS):
            sl = slice(g * gw, (g + 1) * gw)
            Cg = xa_s[pl.ds(r0, q), off_c + g * D_STATE:off_c + (g + 1) * D_STATE].astype(BF16)
            S = st_s[:, sl]
            yo = jnp.dot(Cg, S.astype(BF16), preferred_element_type=F32)
            y_s[pl.ds(r0, q), sl] = y_s[pl.ds(r0, q), sl] + yo * ecs_s[pl.ds(r0, q), sl]
            st_s[:, sl] = cdec[:, sl] * S + stn_s[c, :, sl]
        return carry

    lax.fori_loop(0, n_chunks // nb, within_chunks, 0)
    lax.fori_loop(0, n_chunks, across_chunks, 0, unroll=True)

    @pl.when(t == pl.num_programs(1) - 1)
    def _():
        sN_ref[0] = st_s[...].T.reshape(SSD_HEADS, SSD_HEADDIM, D_STATE)

    y = y_s[...] + dvec_ref[...] * xa_s[:, 0:D_SSM]
    y = y * zs_s[...]
    gw = D_SSM // SSD_GROUPS
    for g in range(SSD_GROUPS):
        sl = slice(g * gw, (g + 1) * gw)
        yg = y[:, sl]
        yg = yg * lax.rsqrt(jnp.mean(yg * yg, axis=-1, keepdims=True) + EPS)
        y_ref[0, :, sl] = (yg * ng_ref[:, sl]).astype(BF16)
    cN_ref[0] = xpad[Tt:Tt + SUBLANES, :]


def _ssd(x, g_mix, w_ssd, c0, s0, cw, cb, dtb, alog, dvec, ng, Tt, q):
    B, L, _ = x.shape
    seq = lambda w: pl.BlockSpec((1, Tt, w), lambda b, t: (b, t, 0))
    per_b = pl.BlockSpec((1, SUBLANES, D_XBC), lambda b, t: (b, 0, 0))
    st = pl.BlockSpec((1, SSD_HEADS, SSD_HEADDIM, D_STATE), lambda b, t: (b, 0, 0, 0))
    masks = _ssd_masks(q)
    return pl.pallas_call(
        functools.partial(_ssd_body, Tt=Tt, q=q),
        grid=(B, L // Tt),
        in_specs=[seq(D_MODEL), _full((1, D_MODEL)), _full(w_ssd.shape),
                  per_b, st, _full((CONV_W, D_XBC)), _full((1, D_XBC)),
                  _full((1, LANES)), _full((1, LANES)), _full((1, D_SSM)), _full((1, D_SSM))]
                 + [_full(m.shape) for m in masks],
        out_specs=[seq(D_SSM), st, per_b],
        out_shape=[jax.ShapeDtypeStruct((B, L, D_SSM), BF16),
                   jax.ShapeDtypeStruct((B, SSD_HEADS, SSD_HEADDIM, D_STATE), F32),
                   jax.ShapeDtypeStruct((B, SUBLANES, D_XBC), F32)],
        scratch_shapes=[pltpu.VMEM((Tt + SUBLANES, D_XBC), F32), pltpu.VMEM((Tt, D_XBC), F32),
                        pltpu.VMEM((Tt, D_SSM), F32), pltpu.VMEM((D_STATE, D_SSM), F32),
                        pltpu.VMEM((Tt, LANES), F32), pltpu.VMEM((Tt, D_SSM), F32),
                        pltpu.VMEM((Tt // q, D_STATE, D_SSM), F32), pltpu.VMEM((Tt, D_SSM), F32),
                        pltpu.VMEM((Tt // q * SUBLANES, D_SSM), F32)],
        compiler_params=_params(2),
        name="ssd",
    )(x, g_mix, w_ssd, c0, s0, cw, cb, dtb, alog, dvec, ng, *masks)


def _out_router_body(x_ref, yl_ref, ys_ref, wo_ref, gf_ref, wr_ref, br_ref, cin_ref,
                     h1_ref, xn_ref, meta_ref, gate_ref, cnt_ref, carry, *, tm):
    step = pl.program_id(0)

    @pl.when(step == 0)
    def _():
        carry[...] = cin_ref[...]

    mix = jnp.concatenate([yl_ref[...], ys_ref[...]], axis=1)
    h1 = x_ref[...] + jnp.dot(mix, wo_ref[...], preferred_element_type=F32)
    h1_ref[...] = h1
    xn = _rmsnorm(h1, gf_ref[...])
    xn_ref[...] = xn

    xh = xn.astype(BF16)
    xm = (xn - xh.astype(F32)).astype(BF16)
    hh_hm = jnp.dot(xh, wr_ref[...], preferred_element_type=F32)
    small = hh_hm[:, LANES:] + jnp.dot(xm, wr_ref[:, 0:LANES], preferred_element_type=F32)
    logits = small + hh_hm[:, 0:LANES] + br_ref[...]
    lt = logits.T
    rows8 = lax.broadcasted_iota(I32, (SUBLANES, tm), 0)
    lg = jnp.where(rows8 < MOE_GROUPS, lt[0:SUBLANES, :], -jnp.inf)
    eg = jnp.exp(lg - jnp.max(lg, axis=0, keepdims=True))
    pg = eg / jnp.sum(eg, axis=0, keepdims=True)
    pgs = jnp.max(pg, axis=0, keepdims=True)
    rows8f = rows8.astype(F32)
    gsel = jnp.min(jnp.where(pg == pgs, rows8f, float(SUBLANES)), axis=0, keepdims=True)

    rows32 = lax.broadcasted_iota(I32, (N_EXPERTS, tm), 0)
    rows32f = rows32.astype(F32)
    grp = (rows32 // EXPERTS_PER_GROUP).astype(F32)
    le = lt[ROUTER_E0:ROUTER_E0 + N_EXPERTS, :]
    ing = grp == gsel
    lem = jnp.where(ing, le, -jnp.inf)
    ee = jnp.exp(lem - jnp.max(lem, axis=0, keepdims=True))
    pe = ee / jnp.sum(ee, axis=0, keepdims=True)
    pe1 = jnp.where(ing, pe, -1.0)
    v1 = jnp.max(pe1, axis=0, keepdims=True)
    i1 = jnp.min(jnp.where(pe1 == v1, rows32f, float(N_EXPERTS)), axis=0, keepdims=True)
    pe2 = jnp.where(rows32f == i1, -1.0, pe1)
    v2 = jnp.max(pe2, axis=0, keepdims=True)
    i2 = jnp.min(jnp.where(pe2 == v2, rows32f, float(N_EXPERTS)), axis=0, keepdims=True)
    sv = v1 + v2
    w1 = v1 / sv * pgs
    w2 = v2 / sv * pgs

    oh1 = rows32f == i1
    oh2 = rows32f == i2
    oh = jnp.where(oh1 | oh2, 1.0, 0.0)
    before = (lax.broadcasted_iota(I32, (tm, tm), 0) < lax.broadcasted_iota(I32, (tm, tm), 1))
    pref = jnp.dot(oh.astype(BF16), jnp.where(before, 1.0, 0.0).astype(BF16), preferred_element_type=F32)
    pref = pref + carry[:, 0:1]
    r1 = jnp.sum(jnp.where(oh1, pref, 0.0), axis=0, keepdims=True)
    r2 = jnp.sum(jnp.where(oh2, pref, 0.0), axis=0, keepdims=True)
    carry[...] = carry[...] + jnp.sum(oh, axis=1, keepdims=True)
    cnt_ref[...] = carry[...]

    meta = jnp.where(rows8 == 0, i1, jnp.where(rows8 == 1, i2, jnp.where(rows8 == 2, r1, jnp.where(rows8 == 3, r2, 0.0))))
    meta_ref[...] = meta.astype(I32)
    gate_ref[...] = jnp.where(rows8 == 0, w1, jnp.where(rows8 == 1, w2, 0.0))


def _out_router(x2d, y_lru, y_ssd, w_out, gf, wr, br, counts_in, tm):
    T = x2d.shape[0]
    row = lambda w: pl.BlockSpec((tm, w), lambda i: (i, 0))
    col = pl.BlockSpec((SUBLANES, tm), lambda i: (0, i))
    return pl.pallas_call(
        functools.partial(_out_router_body, tm=tm),
        grid=(T // tm,),
        in_specs=[row(D_MODEL), row(D_LRU), row(D_SSM), _full((D_LRU + D_SSM, D_MODEL)), _full((1, D_MODEL)),
                  _full((D_MODEL, 2 * LANES)), _full((1, LANES)), _full((N_EXPERTS, LANES))],
        out_specs=[row(D_MODEL), row(D_MODEL), col, col, _full((N_EXPERTS, LANES))],
        out_shape=[jax.ShapeDtypeStruct((T, D_MODEL), F32), jax.ShapeDtypeStruct((T, D_MODEL), F32),
                   jax.ShapeDtypeStruct((SUBLANES, T), I32), jax.ShapeDtypeStruct((SUBLANES, T), F32),
                   jax.ShapeDtypeStruct((N_EXPERTS, LANES), F32)],
        scratch_shapes=[pltpu.VMEM((N_EXPERTS, LANES), F32)],
        compiler_params=_params(1),
        name="out_router",
    )(x2d, y_lru, y_ssd, w_out, gf, wr, br, counts_in)


def _row_copy(src_hbm, src_row, dst, dst_row, sem):
    return pltpu.make_async_copy(src_hbm.at[pl.ds(src_row, 1)], dst.at[pl.ds(dst_row, 1)], sem)


def _positions_body(offs_ref, meta_ref, pos_ref):
    m = meta_ref[...]
    base = jnp.zeros_like(m)
    for e in range(N_EXPERTS):
        base = jnp.where(m == e, offs_ref[e], base)
    pos_ref[...] = base + pltpu.roll(m, SUBLANES - 2, 0)


def _positions(offs, meta):
    T = meta.shape[1]
    tb = min(T, 4096)
    blk = pl.BlockSpec((SUBLANES, tb), lambda i, offs: (0, i))
    return pl.pallas_call(
        _positions_body,
        grid_spec=pltpu.PrefetchScalarGridSpec(num_scalar_prefetch=1, grid=(T // tb,), in_specs=[blk], out_specs=blk),
        out_shape=jax.ShapeDtypeStruct((SUBLANES, T), I32),
        compiler_params=_params(1),
        name="positions",
    )(offs, meta)


def _wait_rows(src, dst, sem):
    pltpu.make_async_copy(src, dst, sem).wait()


def _dispatch_body(zf_ref, pos_ref, *rest, tm, tme, steps):
    n_groups = len(steps)
    xn_hbms = rest[:n_groups]
    xs_hbm, zbuf, xbuf, zsem, lsem, ssem = rest[n_groups:]
    n_steps = sum(steps)
    n_tiles = xs_hbm.shape[0] // tme
    ring = xbuf.shape[0]
    step = pl.program_id(0)

    def zero_copy(i):
        return pltpu.make_async_copy(zbuf, xs_hbm.at[pl.ds(pl.multiple_of(i * tme, tme), tme)], zsem)

    def load(i, start):
        slot = i % ring
        first = 0
        for xn_hbm, n in zip(xn_hbms, steps):
            @pl.when(jnp.logical_and(i >= first, i < first + n))
            def _(xn_hbm=xn_hbm, first=first):
                rows = pl.ds(pl.multiple_of((i - first) * tm, tm), tm)
                cp = pltpu.make_async_copy(xn_hbm.at[rows], xbuf.at[slot], lsem.at[slot])
                if start:
                    cp.start()
                else:
                    cp.wait()
            first += n

    def wait_scatter(i):
        slot = i % ring
        for k in range(2):
            _wait_rows(xbuf.at[slot], xs_hbm.at[pl.ds(0, tm)], ssem.at[slot])

    @pl.when(step == 0)
    def _():
        zbuf[...] = jnp.zeros_like(zbuf)

        def start(i, c):
            @pl.when(zf_ref[i] != 0)
            def _():
                zero_copy(i).start()
            return c

        def wait(i, c):
            @pl.when(zf_ref[i] != 0)
            def _():
                zero_copy(i).wait()
            return c

        lax.fori_loop(0, n_tiles, start, 0)
        lax.fori_loop(0, n_tiles, wait, 0)
        load(step, True)

    @pl.when(step + 1 < n_steps)
    def _():
        @pl.when(step >= ring - 1)
        def _():
            wait_scatter(step - (ring - 1))
        load(step + 1, True)

    load(step, False)
    slot = step % ring
    for t in range(tm):
        for k in range(2):
            _row_copy(xbuf.at[slot], t, xs_hbm, pos_ref[k, t], ssem.at[slot]).start(priority=k)

    @pl.when(step == n_steps - 1)
    def _():
        for j in range(max(n_steps - ring, 0), n_steps):
            wait_scatter(j)


def _dispatch(zflag, pos, xns, n_rows, tm, tme):
    steps = tuple(xn.shape[0] // tm for xn in xns)
    ring = 3
    return pl.pallas_call(
        functools.partial(_dispatch_body, tm=tm, tme=tme, steps=steps),
        grid_spec=pltpu.PrefetchScalarGridSpec(
            num_scalar_prefetch=1,
            grid=(sum(steps),),
            in_specs=[pl.BlockSpec((SUBLANES, tm), lambda i, zf: (0, i), memory_space=pltpu.SMEM)]
                     + [pl.BlockSpec(memory_space=pl.ANY) for _ in xns],
            out_specs=pl.BlockSpec(memory_space=pl.ANY),
            scratch_shapes=[pltpu.VMEM((tme, D_MODEL), F32), pltpu.VMEM((ring, tm, D_MODEL), F32),
                            pltpu.SemaphoreType.DMA(()), pltpu.SemaphoreType.DMA((ring,)),
                            pltpu.SemaphoreType.DMA((ring,))],
        ),
        out_shape=jax.ShapeDtypeStruct((n_rows, D_MODEL), F32),
        compiler_params=_params(1),
        name="dispatch",
    )(zflag, pos, *xns)


def _experts_body(te_ref, na_ref, x_ref, w1_ref, w3_ref, w2_ref, o_ref, w1b, w3b, w2b):
    i = pl.program_id(0)
    changed = jnp.logical_or(i == 0, te_ref[i] != te_ref[jnp.maximum(i - 1, 0)])

    @pl.when(changed)
    def _():
        w1b[...] = w1_ref[0].astype(BF16)
        w3b[...] = w3_ref[0].astype(BF16)
        w2b[...] = w2_ref[0].astype(BF16)

    @pl.when(i < na_ref[0])
    def _():
        xb = x_ref[...].astype(BF16)
        a = jnp.dot(xb, w1b[...], preferred_element_type=F32)
        b = jnp.dot(xb, w3b[...], preferred_element_type=F32)
        hd = (jax.nn.silu(a) * b).astype(BF16)
        o_ref[...] = jnp.dot(hd, w2b[...], preferred_element_type=F32)

    @pl.when(i >= na_ref[0])
    def _():
        o_ref[...] = jnp.zeros_like(o_ref)


def _experts(tile_e, n_active, xs, w1, w3, w2, tme):
    n_rows = xs.shape[0]
    row = pl.BlockSpec((tme, D_MODEL), lambda i, te, na: (i, 0))
    return pl.pallas_call(
        _experts_body,
        grid_spec=pltpu.PrefetchScalarGridSpec(
            num_scalar_prefetch=2,
            grid=(n_rows // tme,),
            in_specs=[row,
                      pl.BlockSpec((1, D_MODEL, D_EXPERT), lambda i, te, na: (te[i], 0, 0)),
                      pl.BlockSpec((1, D_MODEL, D_EXPERT), lambda i, te, na: (te[i], 0, 0)),
                      pl.BlockSpec((1, D_EXPERT, D_MODEL), lambda i, te, na: (te[i], 0, 0))],
            out_specs=row,
            scratch_shapes=[pltpu.VMEM((D_MODEL, D_EXPERT), BF16), pltpu.VMEM((D_MODEL, D_EXPERT), BF16),
                            pltpu.VMEM((D_EXPERT, D_MODEL), BF16)],
        ),
        out_shape=jax.ShapeDtypeStruct((n_rows, D_MODEL), F32),
        compiler_params=_params(1),
        name="experts",
    )(tile_e, n_active, xs, w1, w3, w2)


def _combine_body(pos_ref, nxt_ref, gate_ref, h1_ref, fg_ref, ys_hbm, y_ref, ybuf, sem, *, tm):
    step = pl.program_id(0)
    slot = step % 2

    def gather(p_ref, s):
        for t in range(tm):
            for k in range(2):
                _row_copy(ys_hbm, p_ref[k, t], ybuf.at[s, k], t, sem.at[s]).start(priority=k)

    @pl.when(step == 0)
    def _():
        gather(pos_ref, 0)

    @pl.when(step + 1 < pl.num_programs(0))
    def _():
        gather(nxt_ref, 1 - slot)

    for k in range(2):
        _wait_rows(ys_hbm.at[pl.ds(0, tm)], ybuf.at[slot, k], sem.at[slot])

    eye = lax.broadcasted_iota(I32, (tm, tm), 0) == lax.broadcasted_iota(I32, (tm, tm), 1)
    g1 = jnp.sum(jnp.where(eye, gate_ref[0:1, :], 0.0), axis=1, keepdims=True)
    g2 = jnp.sum(jnp.where(eye, gate_ref[1:2, :], 0.0), axis=1, keepdims=True)
    h2 = h1_ref[...] + (g1 * ybuf[slot, 0] + g2 * ybuf[slot, 1])
    y_ref[...] = _rmsnorm(h2, fg_ref[...])


def _combine(pos, gates, h1, fg, ys, tm):
    T = h1.shape[0]
    last = T // tm - 1
    return pl.pallas_call(
        functools.partial(_combine_body, tm=tm),
        grid=(T // tm,),
        in_specs=[pl.BlockSpec((SUBLANES, tm), lambda i: (0, i), memory_space=pltpu.SMEM),
                  pl.BlockSpec((SUBLANES, tm), lambda i: (0, jnp.minimum(i + 1, last)), memory_space=pltpu.SMEM),
                  pl.BlockSpec((SUBLANES, tm), lambda i: (0, i)),
                  pl.BlockSpec((tm, D_MODEL), lambda i: (i, 0)),
                  pl.BlockSpec((1, D_MODEL), lambda i: (0, 0)),
                  pl.BlockSpec(memory_space=pl.ANY)],
        out_specs=pl.BlockSpec((tm, D_MODEL), lambda i: (i, 0)),
        scratch_shapes=[pltpu.VMEM((2, 2, tm, D_MODEL), F32), pltpu.SemaphoreType.DMA((2,))],
        out_shape=jax.ShapeDtypeStruct((T, D_MODEL), F32),
        compiler_params=_params(1),
        name="combine",
    )(pos, pos, gates, h1, fg, ys)


def _blockdiag(w):
    per = GATE_TILE // LRU_BLOCK
    w4 = w.reshape(LRU_BLOCKS // per, per, LRU_BLOCK, LRU_BLOCK)
    eye = jnp.eye(per, dtype=w.dtype)
    return jnp.einsum('jbio,bc->jbico', w4, eye).reshape(LRU_BLOCKS // per, GATE_TILE, GATE_TILE)


def _pad_rows(c):
    return jnp.pad(c, ((0, 0), (SUBLANES - (CONV_W - 1), 0), (0, 0)))


def _router_cols(group_part, expert_part):
    r = group_part.shape[0]
    out = jnp.zeros((r, LANES), F32)
    out = out.at[:, 0:MOE_GROUPS].set(group_part)
    return out.at[:, ROUTER_E0:ROUTER_E0 + N_EXPERTS].set(expert_part)


def _lane_row(v, width=LANES):
    return jnp.pad(v, (0, width - v.shape[0])).reshape(1, width)


def _prep(norm_mix_g, w_in, lru_conv_w, lru_conv_b, lru_wa, lru_ba, lru_wx, lru_bx, lru_lambda,
          ssd_conv_w, ssd_conv_b, ssd_dt_bias, ssd_a_log, ssd_d, ssd_norm_g, w_out,
          norm_ffn_g, router_group_w, router_group_b, router_expert_w, router_expert_b,
          moe_w1, moe_w3, moe_w2, final_norm_g):
    w = w_in[0]
    wr = _router_cols(router_group_w[0], router_expert_w[0])
    wr_hi = wr.astype(BF16)
    P = dict(
        g_mix=norm_mix_g[0].reshape(1, D_MODEL),
        w_lru=w[:, :2 * D_LRU].astype(BF16),
        w_ssd=jnp.pad(w[:, 2 * D_LRU:], ((0, 0), (0, LANES - SSD_HEADS))).astype(BF16),
        lru_cw=lru_conv_w[0], lru_cb=lru_conv_b[0].reshape(1, D_LRU),
        wbd=jnp.concatenate([_blockdiag(lru_wa[0]), _blockdiag(lru_wx[0])], axis=2).astype(BF16),
        ba=lru_ba[0].reshape(1, D_LRU), bx=lru_bx[0].reshape(1, D_LRU), lam=lru_lambda[0].reshape(1, D_LRU),
        ssd_cw=ssd_conv_w[0], ssd_cb=ssd_conv_b[0].reshape(1, D_XBC),
        dtb=_lane_row(ssd_dt_bias[0]), alog=_lane_row(ssd_a_log[0]),
        dvec=jnp.repeat(ssd_d[0], SSD_HEADDIM).reshape(1, D_SSM),
        ng=ssd_norm_g[0].reshape(1, D_SSM),
        w_out=w_out[0].astype(BF16),
        g_ffn=norm_ffn_g[0].reshape(1, D_MODEL),
        wr=jnp.concatenate([wr_hi, (wr - wr_hi.astype(F32)).astype(BF16)], axis=1),
        br=_router_cols(router_group_b[0][None], router_expert_b[0][None]),
        w1=moe_w1[0], w3=moe_w3[0], w2=moe_w2[0],
        g_final=final_norm_g.reshape(1, D_MODEL),
    )
    return P


def _expert_layout(counts, n_pairs, tme):
    cnt = counts[:, 0].astype(I32)
    padded = ((cnt + tme - 1) // tme) * tme
    ends = jnp.cumsum(padded)
    offs = ends - padded
    n_tiles = n_pairs // tme + N_EXPERTS
    n_active = ends[-1] // tme
    tiles = jnp.arange(n_tiles, dtype=I32)
    tile_e = jnp.sum((tiles * tme)[:, None] >= ends[None, :], axis=1).astype(I32)
    last_e = jnp.sum((n_active - 1) * tme >= ends).astype(I32)
    tile_e = jnp.where(tiles < n_active, tile_e, last_e)
    is_last = jnp.any((tiles[:, None] + 1) * tme == ends[None, :], axis=1)
    zflag = jnp.logical_or(is_last, tiles >= n_active).astype(I32)
    return offs.astype(I32), tile_e, n_active.reshape(1).astype(I32), zflag, n_tiles * tme


def _mixer_router(x, lru_h0, lru_c0, ssd_h0, ssd_c0, P, start_pos, counts_in):
    B, L, _ = x.shape
    T = B * L
    Tt = min(MIX_TILE, L)
    q = min(SSD_CHUNK, L)
    tm = min(ROW_TILE, T)
    x2d = x.reshape(T, D_MODEL)

    y_lru, lru_h, lru_c = _lru(x, P['g_mix'], P['w_lru'], _pad_rows(lru_c0), lru_h0.reshape(B, 1, D_LRU),
                               P['lru_cw'], P['lru_cb'], P['wbd'], P['ba'], P['bx'], P['lam'], Tt, start_pos)
    y_ssd, ssd_h, ssd_c = _ssd(x, P['g_mix'], P['w_ssd'], _pad_rows(ssd_c0), ssd_h0,
                               P['ssd_cw'], P['ssd_cb'], P['dtb'], P['alog'], P['dvec'], P['ng'], Tt, q)
    h1, xn, meta, gates, counts = _out_router(
        x2d, y_lru.reshape(T, D_LRU), y_ssd.reshape(T, D_SSM), P['w_out'], P['g_ffn'],
        P['wr'], P['br'], counts_in, min(MIX_TILE, T))

    hist = SUBLANES - (CONV_W - 1)
    states = (lru_h.reshape(1, B, D_LRU), lru_c[:, hist:][None], ssd_h[None], ssd_c[:, hist:][None])
    return dict(h1=h1, xn=xn, meta=meta, gates=gates, shape=(B, L, D_MODEL), tm=tm), counts, states


def _moe_final(groups, counts, P):
    tme = EXPERT_TILE
    n_pairs = 2 * sum(g['h1'].shape[0] for g in groups)
    offs, tile_e, n_active, zflag, n_rows = _expert_layout(counts, n_pairs, tme)
    tm = groups[0]['tm']
    assert all(g['tm'] == tm for g in groups)
    for g in groups:
        g['pos'] = _positions(offs, g['meta'])
    pos_all = jnp.concatenate([g['pos'] for g in groups], axis=1)
    xs = _dispatch(zflag, pos_all, [g['xn'] for g in groups], n_rows, tm, tme)
    ys = _experts(tile_e, n_active, xs, P['w1'], P['w3'], P['w2'], tme)
    return [_combine(g['pos'], g['gates'], g['h1'], P['g_final'], ys, g['tm']).reshape(g['shape'])
            for g in groups]


def kernel(x_prompt, x_sample, state_lru_h, state_lru_conv, state_ssd, state_ssd_conv, norm_mix_g, w_in, lru_conv_w, lru_conv_b, lru_wa, lru_ba, lru_wx, lru_bx, lru_lambda, ssd_conv_w, ssd_conv_b, ssd_dt_bias, ssd_a_log, ssd_d, ssd_norm_g, w_out, norm_ffn_g, router_group_w, router_group_b, router_expert_w, router_expert_b, moe_w1, moe_w3, moe_w2, final_norm_g):
    P = _prep(norm_mix_g, w_in, lru_conv_w, lru_conv_b, lru_wa, lru_ba, lru_wx, lru_bx, lru_lambda,
              ssd_conv_w, ssd_conv_b, ssd_dt_bias, ssd_a_log, ssd_d, ssd_norm_g, w_out,
              norm_ffn_g, router_group_w, router_group_b, router_expert_w, router_expert_b,
              moe_w1, moe_w3, moe_w2, final_norm_g)
    bp = x_prompt.shape[0]
    gp, counts, (a1, a2, a3, a4) = _mixer_router(
        x_prompt,
        jnp.zeros((bp, D_LRU), F32), jnp.zeros((bp, CONV_W - 1, D_LRU), F32),
        jnp.zeros((bp, SSD_HEADS, SSD_HEADDIM, D_STATE), F32), jnp.zeros((bp, CONV_W - 1, D_XBC), F32),
        P, 0, jnp.zeros((N_EXPERTS, LANES), F32))
    gs, counts, (b1, b2, b3, b4) = _mixer_router(
        x_sample, state_lru_h[0], state_lru_conv[0], state_ssd[0], state_ssd_conv[0], P, PAST_LEN, counts)
    yp, ys = _moe_final([gp, gs], counts, P)
    return (yp, ys, a1, a2, a3, a4, b1, b2, b3, b4)
```

```python
import functools

import jax
import jax.numpy as jnp
from jax import lax
from jax.experimental import pallas as pl
from jax.experimental.pallas import tpu as pltpu

F32 = jnp.float32
BF16 = jnp.bfloat16
I32 = jnp.int32

D_MODEL = 1024
D_LRU = 1024
LRU_BLOCKS = 16
LRU_BLOCK = 64
LRU_C = 8.0
CONV_W = 4
D_SSM = 1024
SSD_HEADDIM = 64
SSD_HEADS = 16
SSD_GROUPS = 2
SSD_HPG = 8
D_STATE = 128
D_XBC = 1536
MOE_GROUPS = 4
EXPERTS_PER_GROUP = 8
N_EXPERTS = 32
D_EXPERT = 512
EPS = 1e-6
SSD_CHUNK = 64
PAST_LEN = 1024

LANES = 128
SUBLANES = 8
GATE_TILE = 256
ROW_TILE = 256
MIX_TILE = 512
EXPERT_TILE = 512
ROUTER_E0 = 32
VMEM_LIMIT = 52 * 1024 * 1024

_NT = (((1,), (1,)), ((), ()))
_TN = (((0,), (0,)), ((), ()))


def _params(n_axes):
    return pltpu.CompilerParams(dimension_semantics=("arbitrary",) * n_axes,
                                vmem_limit_bytes=VMEM_LIMIT)


def _rmsnorm(x, g):
    return x * lax.rsqrt(jnp.mean(x * x, axis=-1, keepdims=True) + EPS) * g


def _full(shape):
    n = len(shape)
    return pl.BlockSpec(shape, lambda *_: (0,) * n)


def _project(xb, w_ref, lo, hi):
    return jnp.dot(xb, w_ref[:, lo:hi], preferred_element_type=F32)


def _conv_block(xpad, cw_ref, cb_ref, Tt, sl):
    cw = cw_ref[:, sl]
    full = xpad[:, sl]
    y = cb_ref[:, sl]
    for k in range(CONV_W):
        shift = CONV_W - 1 - k
        xk = pltpu.roll(full, shift, 0) if shift else full
        y = y + xk[SUBLANES:SUBLANES + Tt, :] * cw[k:k + 1, :]
    return y


def _carry_history(t, xpad, c0_ref, Tt):
    @pl.when(t == 0)
    def _():
        xpad[0:SUBLANES, :] = c0_ref[0]

    @pl.when(t > 0)
    def _():
        xpad[0:SUBLANES, :] = xpad[Tt:Tt + SUBLANES, :]


def _gelu_tanh(x):
    c = 0.7978845608028654
    t = jnp.tanh(x * (c + (c * 0.044715) * (x * x)))
    hx = 0.5 * x
    return hx + hx * t


def _lru_body(x_ref, gm_ref, w_ref, c0_ref, h0_ref, cw_ref, cb_ref, wbd_ref, ba_ref, bx_ref, lam_ref,
              y_ref, hN_ref, cN_ref, xpad, y_s, hcar, *, Tt, start_pos):
    t = pl.program_id(1)
    _carry_history(t, xpad, c0_ref, Tt)

    @pl.when(t == 0)
    def _():
        hcar[...] = jnp.broadcast_to(h0_ref[0], (SUBLANES, D_LRU))

    xb = _rmsnorm(x_ref[0], gm_ref[...]).astype(BF16)
    sp = jax.nn.softplus(-lam_ref[...])
    rows = lax.broadcasted_iota(I32, (SUBLANES, GATE_TILE), 0)
    for j in range(D_LRU // GATE_TILE):
        sl = slice(GATE_TILE * j, GATE_TILE * (j + 1))
        xpad[SUBLANES:SUBLANES + Tt, sl] = _project(xb, w_ref, sl.start, sl.stop)
        xc = _conv_block(xpad, cw_ref, cb_ref, Tt, sl)
        ga = jnp.dot(xc.astype(BF16), wbd_ref[j], preferred_element_type=F32)
        r = jax.nn.sigmoid(ga[:, :GATE_TILE] + ba_ref[:, sl])
        i = jax.nn.sigmoid(ga[:, GATE_TILE:] + bx_ref[:, sl])
        a = jnp.exp((-LRU_C * r) * sp[:, sl])
        y1 = 1.0 - a * a
        mult = jnp.where(y1 > 0.0, y1 * lax.rsqrt(y1), 0.0)
        u = mult * i * xc
        gel = _gelu_tanh(_project(xb, w_ref, D_LRU + sl.start, D_LRU + sl.stop))
        h = hcar[:, sl]
        for gi in range(Tt // SUBLANES):
            rs = slice(gi * SUBLANES, (gi + 1) * SUBLANES)
            a8, u8 = a[rs], u[rs]
            if gi == 0 and start_pos == 0:
                u8 = jnp.where(jnp.logical_and(rows == 0, t == 0), i[rs] * xc[rs], u8)
            for s in (1, 2, 4):
                ok = rows >= s
                u_sh = pltpu.roll(u8, s, 0)
                a_sh = pltpu.roll(a8, s, 0)
                u8 = jnp.where(ok, u8 + a8 * u_sh, u8)
                a8 = jnp.where(ok, a8 * a_sh, a8)
            h8 = u8 + a8 * h
            y_s[rs, sl] = h8 * gel[rs]
            h = jnp.broadcast_to(h8[SUBLANES - 1:SUBLANES, :], (SUBLANES, GATE_TILE))
        hcar[:, sl] = h

    y_ref[0] = y_s[...].astype(BF16)
    hN_ref[0] = hcar[0:1, :]
    cN_ref[0] = xpad[Tt:Tt + SUBLANES, :]


def _lru(x, g_mix, w_lru, c0, h0, cw, cb, wbd, ba, bx, lam, Tt, start_pos):
    B, L, _ = x.shape
    seq = pl.BlockSpec((1, Tt, D_LRU), lambda b, t: (b, t, 0))
    per_b = lambda r: pl.BlockSpec((1, r, D_LRU), lambda b, t: (b, 0, 0))
    return pl.pallas_call(
        functools.partial(_lru_body, Tt=Tt, start_pos=start_pos),
        grid=(B, L // Tt),
        in_specs=[pl.BlockSpec((1, Tt, D_MODEL), lambda b, t: (b, t, 0)), _full((1, D_MODEL)), _full(w_lru.shape),
                  per_b(SUBLANES), per_b(1), _full((CONV_W, D_LRU)), _full((1, D_LRU)),
                  _full(wbd.shape), _full((1, D_LRU)), _full((1, D_LRU)), _full((1, D_LRU))],
        out_specs=[seq, per_b(1), per_b(SUBLANES)],
        out_shape=[jax.ShapeDtypeStruct((B, L, D_LRU), BF16),
                   jax.ShapeDtypeStruct((B, 1, D_LRU), F32),
                   jax.ShapeDtypeStruct((B, SUBLANES, D_LRU), F32)],
        scratch_shapes=[pltpu.VMEM((Tt + SUBLANES, D_LRU), F32), pltpu.VMEM((Tt, D_LRU), F32),
                        pltpu.VMEM((SUBLANES, D_LRU), F32)],
        compiler_params=_params(2),
        name="lru",
    )(x, g_mix, w_lru, c0, h0, cw, cb, wbd, ba, bx, lam)


def _split3(v):
    hi = v.astype(BF16)
    r1 = v - hi.astype(F32)
    mid = r1.astype(BF16)
    lo = (r1 - mid.astype(F32)).astype(BF16)
    return hi, mid, lo


def _pad_time(v, rows):
    if v.shape[0] == rows:
        return v
    return jnp.concatenate([v, jnp.zeros((rows - v.shape[0], v.shape[1]), v.dtype)], axis=0)


def _ssd_masks(q):
    P = SSD_HEADDIM
    tri = jnp.arange(q)[:, None] >= jnp.arange(q)[None, :]
    expand = jnp.arange(LANES)[:, None] == jnp.arange(D_SSM)[None, :] // P
    row_q = jnp.arange(q)[:, None]
    lane_k = jnp.arange(D_SSM)[None, :] % P
    diag = row_q == lane_k
    causal = row_q >= lane_k
    bd = jnp.arange(GATE_TILE)[:, None] // P == jnp.arange(GATE_TILE)[None, :] // P
    return (tri.astype(BF16), expand.astype(BF16), diag.astype(F32), causal.astype(F32), bd.astype(BF16))


def _ssd_body(x_ref, gm_ref, w_ref, c0_ref, s0_ref, cw_ref, cb_ref, dtb_ref, alog_ref, dvec_ref, ng_ref,
              tri_ref, expand_ref, diag_ref, causal_ref, bd_ref,
              y_ref, sN_ref, cN_ref, xpad, xa_s, y_s, st_s, dt_s, zs_s, stn_s, ecs_s, cdec_s, *, Tt, q):
    t = pl.program_id(1)
    _carry_history(t, xpad, c0_ref, Tt)

    @pl.when(t == 0)
    def _():
        st_s[...] = s0_ref[0].reshape(D_SSM, D_STATE).T

    xb = _rmsnorm(x_ref[0], gm_ref[...]).astype(BF16)
    for j in range(D_XBC // GATE_TILE):
        sl = slice(GATE_TILE * j, GATE_TILE * (j + 1))
        xpad[SUBLANES:SUBLANES + Tt, sl] = _project(xb, w_ref, D_SSM + sl.start, D_SSM + sl.stop)
        xa_s[:, sl] = jax.nn.silu(_conv_block(xpad, cw_ref, cb_ref, Tt, sl))
    for j in range(D_SSM // GATE_TILE):
        sl = slice(GATE_TILE * j, GATE_TILE * (j + 1))
        zs_s[:, sl] = jax.nn.silu(_project(xb, w_ref, sl.start, sl.stop))
    dt_s[...] = _project(xb, w_ref, D_SSM + D_XBC, D_SSM + D_XBC + LANES)
    A = -jnp.exp(alog_ref[...])
    P = SSD_HEADDIM
    blk = GATE_TILE // P
    off_b = D_SSM
    off_c = D_SSM + SSD_GROUPS * D_STATE
    gw = D_SSM // SSD_GROUPS

    def exact01(parts, w01, left):
        one = (lambda p: jnp.dot(w01, p, preferred_element_type=F32)) if left else (
            lambda p: jnp.dot(p, w01, preferred_element_type=F32))
        hi, mid, lo = parts
        return (one(lo) + one(mid)) + one(hi)

    n_chunks = Tt // q
    nb = min(8, n_chunks)

    def within_chunks(cb, carry):
        cidx = [cb * nb + i for i in range(nb)]
        r0s = [pl.multiple_of(c * q, q) for c in cidx]
        xs = [xa_s[pl.ds(r0, q), 0:D_SSM] for r0 in r0s]
        dts = [jax.nn.softplus(dt_s[pl.ds(r0, q), :] + dtb_ref[...]) for r0 in r0s]
        css = [exact01(_split3(dt * A), tri_ref[...], left=True) for dt in dts]
        Es = [exact01(_split3(jnp.concatenate([cs, dt], axis=0)), expand_ref[...], left=False)
              for cs, dt in zip(css, dts)]
        diag = diag_ref[...] != 0.0
        causal = causal_ref[...] != 0.0
        Bgs = [[xa_s[pl.ds(r0, q), off_b + g * D_STATE:off_b + (g + 1) * D_STATE].astype(BF16)
                for g in range(SSD_GROUPS)] for r0 in r0s]
        Cgs = [[xa_s[pl.ds(r0, q), off_c + g * D_STATE:off_c + (g + 1) * D_STATE].astype(BF16)
                for g in range(SSD_GROUPS)] for r0 in r0s]
        CBs = [jnp.concatenate(
            [lax.dot_general(Cg, jnp.concatenate([_pad_time(Bg, P)] * SSD_HPG, axis=0), _NT,
                             preferred_element_type=F32) for Bg, Cg in zip(Bgc, Cgc)], axis=1)
            for Bgc, Cgc in zip(Bgs, Cgs)]
        xws = []
        for i in range(nb):
            E_cs, E_dt = Es[i][0:q], Es[i][q:2 * q]
            cs_last = E_cs[q - 1:q, :]
            xws.append((jnp.exp(cs_last - E_cs) * E_dt * xs[i]).astype(BF16))
            ecs_s[pl.ds(r0s[i], q), :] = jnp.exp(E_cs)
            cdec_s[pl.ds(pl.multiple_of(cidx[i] * SUBLANES, SUBLANES), SUBLANES), :] = jnp.broadcast_to(
                jnp.exp(cs_last), (SUBLANES, D_SSM))
        for i in range(nb):
            for g in range(SSD_GROUPS):
                sl = slice(g * gw, (g + 1) * gw)
                stn_s[cidx[i], :, sl] = lax.dot_general(Bgs[i][g], xws[i][:, sl], _TN, preferred_element_type=F32)
        Mws = []
        for i in range(nb):
            E_cs, E_dt = Es[i][0:q], Es[i][q:2 * q]
            r_cs = jnp.sum(jnp.where(diag, E_cs, 0.0), axis=0, keepdims=True)
            r_dt = jnp.sum(jnp.where(diag, E_dt, 0.0), axis=0, keepdims=True)
            Lm = jnp.exp(jnp.where(causal, E_cs - r_cs, -jnp.inf))
            Mws.append((CBs[i] * Lm * r_dt).astype(BF16))
        for i in range(nb):
            xsb = xs[i].astype(BF16)
            for j in range(D_SSM // GATE_TILE):
                sl = slice(j * GATE_TILE, (j + 1) * GATE_TILE)
                slab = _pad_time(xsb[:, sl], P)
                rhs = jnp.concatenate([slab] * blk, axis=0) * bd_ref[...]
                y_s[pl.ds(r0s[i], q), sl] = jnp.dot(Mws[i][:, sl], rhs, preferred_element_type=F32)
        return carry

    def across_chunks(c, carry):
        r0 = pl.multiple_of(c * q, q)
        cdec = cdec_s[pl.ds(pl.multiple_of(c * SUBLANES, SUBLANES), 1), :]
        for g in range(SSD_GROUPS):
            sl = slice(g * gw, (g + 1) * gw)
            Cg = xa_s[pl.ds(r0, q), off_c + g * D_STATE:off_c + (g + 1) * D_STATE].astype(BF16)
            S = st_s[:, sl]
            yo = jnp.dot(Cg, S.astype(BF16), preferred_element_type=F32)
            y_s[pl.ds(r0, q), sl] = y_s[pl.ds(r0, q), sl] + yo * ecs_s[pl.ds(r0, q), sl]
            st_s[:, sl] = cdec[:, sl] * S + stn_s[c, :, sl]
        return carry

    lax.fori_loop(0, n_chunks // nb, within_chunks, 0)
    lax.fori_loop(0, n_chunks, across_chunks, 0, unroll=True)

    @pl.when(t == pl.num_programs(1) - 1)
    def _():
        sN_ref[0] = st_s[...].T.reshape(SSD_HEADS, SSD_HEADDIM, D_STATE)

    y = y_s[...] + dvec_ref[...] * xa_s[:, 0:D_SSM]
    y = y * zs_s[...]
    gw = D_SSM // SSD_GROUPS
    for g in range(SSD_GROUPS):
        sl = slice(g * gw, (g + 1) * gw)
        yg = y[:, sl]
        yg = yg * lax.rsqrt(jnp.mean(yg * yg, axis=-1, keepdims=True) + EPS)
        y_ref[0, :, sl] = (yg * ng_ref[:, sl]).astype(BF16)
    cN_ref[0] = xpad[Tt:Tt + SUBLANES, :]


def _ssd(x, g_mix, w_ssd, c0, s0, cw, cb, dtb, alog, dvec, ng, Tt, q):
    B, L, _ = x.shape
    seq = lambda w: pl.BlockSpec((1, Tt, w), lambda b, t: (b, t, 0))
    per_b = pl.BlockSpec((1, SUBLANES, D_XBC), lambda b, t: (b, 0, 0))
    st = pl.BlockSpec((1, SSD_HEADS, SSD_HEADDIM, D_STATE), lambda b, t: (b, 0, 0, 0))
    masks = _ssd_masks(q)
    return pl.pallas_call(
        functools.partial(_ssd_body, Tt=Tt, q=q),
        grid=(B, L // Tt),
        in_specs=[seq(D_MODEL), _full((1, D_MODEL)), _full(w_ssd.shape),
                  per_b, st, _full((CONV_W, D_XBC)), _full((1, D_XBC)),
                  _full((1, LANES)), _full((1, LANES)), _full((1, D_SSM)), _full((1, D_SSM))]
                 + [_full(m.shape) for m in masks],
        out_specs=[seq(D_SSM), st, per_b],
        out_shape=[jax.ShapeDtypeStruct((B, L, D_SSM), BF16),
                   jax.ShapeDtypeStruct((B, SSD_HEADS, SSD_HEADDIM, D_STATE), F32),
                   jax.ShapeDtypeStruct((B, SUBLANES, D_XBC), F32)],
        scratch_shapes=[pltpu.VMEM((Tt + SUBLANES, D_XBC), F32), pltpu.VMEM((Tt, D_XBC), F32),
                        pltpu.VMEM((Tt, D_SSM), F32), pltpu.VMEM((D_STATE, D_SSM), F32),
                        pltpu.VMEM((Tt, LANES), F32), pltpu.VMEM((Tt, D_SSM), F32),
                        pltpu.VMEM((Tt // q, D_STATE, D_SSM), F32), pltpu.VMEM((Tt, D_SSM), F32),
                        pltpu.VMEM((Tt // q * SUBLANES, D_SSM), F32)],
        compiler_params=_params(2),
        name="ssd",
    )(x, g_mix, w_ssd, c0, s0, cw, cb, dtb, alog, dvec, ng, *masks)


def _out_router_body(x_ref, yl_ref, ys_ref, wo_ref, gf_ref, wr_ref, br_ref, cin_ref,
                     h1_ref, xn_ref, meta_ref, gate_ref, cnt_ref, carry, *, tm):
    step = pl.program_id(0)

    @pl.when(step == 0)
    def _():
        carry[...] = cin_ref[...]

    mix = jnp.concatenate([yl_ref[...], ys_ref[...]], axis=1)
    h1 = x_ref[...] + jnp.dot(mix, wo_ref[...], preferred_element_type=F32)
    h1_ref[...] = h1
    xn = _rmsnorm(h1, gf_ref[...])
    xn_ref[...] = xn

    xh = xn.astype(BF16)
    xm = (xn - xh.astype(F32)).astype(BF16)
    hh_hm = jnp.dot(xh, wr_ref[...], preferred_element_type=F32)
    small = hh_hm[:, LANES:] + jnp.dot(xm, wr_ref[:, 0:LANES], preferred_element_type=F32)
    logits = small + hh_hm[:, 0:LANES] + br_ref[...]
    lt = logits.T
    rows8 = lax.broadcasted_iota(I32, (SUBLANES, tm), 0)
    lg = jnp.where(rows8 < MOE_GROUPS, lt[0:SUBLANES, :], -jnp.inf)
    eg = jnp.exp(lg - jnp.max(lg, axis=0, keepdims=True))
    pg = eg / jnp.sum(eg, axis=0, keepdims=True)
    pgs = jnp.max(pg, axis=0, keepdims=True)
    rows8f = rows8.astype(F32)
    gsel = jnp.min(jnp.where(pg == pgs, rows8f, float(SUBLANES)), axis=0, keepdims=True)

    rows32 = lax.broadcasted_iota(I32, (N_EXPERTS, tm), 0)
    rows32f = rows32.astype(F32)
    grp = (rows32 // EXPERTS_PER_GROUP).astype(F32)
    le = lt[ROUTER_E0:ROUTER_E0 + N_EXPERTS, :]
    ing = grp == gsel
    lem = jnp.where(ing, le, -jnp.inf)
    ee = jnp.exp(lem - jnp.max(lem, axis=0, keepdims=True))
    pe = ee / jnp.sum(ee, axis=0, keepdims=True)
    pe1 = jnp.where(ing, pe, -1.0)
    v1 = jnp.max(pe1, axis=0, keepdims=True)
    i1 = jnp.min(jnp.where(pe1 == v1, rows32f, float(N_EXPERTS)), axis=0, keepdims=True)
    pe2 = jnp.where(rows32f == i1, -1.0, pe1)
    v2 = jnp.max(pe2, axis=0, keepdims=True)
    i2 = jnp.min(jnp.where(pe2 == v2, rows32f, float(N_EXPERTS)), axis=0, keepdims=True)
    sv = v1 + v2
    w1 = v1 / sv * pgs
    w2 = v2 / sv * pgs

    oh1 = rows32f == i1
    oh2 = rows32f == i2
    oh = jnp.where(oh1 | oh2, 1.0, 0.0)
    before = (lax.broadcasted_iota(I32, (tm, tm), 0) < lax.broadcasted_iota(I32, (tm, tm), 1))
    pref = jnp.dot(oh.astype(BF16), jnp.where(before, 1.0, 0.0).astype(BF16), preferred_element_type=F32)
    pref = pref + carry[:, 0:1]
    r1 = jnp.sum(jnp.where(oh1, pref, 0.0), axis=0, keepdims=True)
    r2 = jnp.sum(jnp.where(oh2, pref, 0.0), axis=0, keepdims=True)
    carry[...] = carry[...] + jnp.sum(oh, axis=1, keepdims=True)
    cnt_ref[...] = carry[...]

    meta = jnp.where(rows8 == 0, i1, jnp.where(rows8 == 1, i2, jnp.where(rows8 == 2, r1, jnp.where(rows8 == 3, r2, 0.0))))
    meta_ref[...] = meta.astype(I32)
    gate_ref[...] = jnp.where(rows8 == 0, w1, jnp.where(rows8 == 1, w2, 0.0))


def _out_router(x2d, y_lru, y_ssd, w_out, gf, wr, br, counts_in, tm):
    T = x2d.shape[0]
    row = lambda w: pl.BlockSpec((tm, w), lambda i: (i, 0))
    col = pl.BlockSpec((SUBLANES, tm), lambda i: (0, i))
    return pl.pallas_call(
        functools.partial(_out_router_body, tm=tm),
        grid=(T // tm,),
        in_specs=[row(D_MODEL), row(D_LRU), row(D_SSM), _full((D_LRU + D_SSM, D_MODEL)), _full((1, D_MODEL)),
                  _full((D_MODEL, 2 * LANES)), _full((1, LANES)), _full((N_EXPERTS, LANES))],
        out_specs=[row(D_MODEL), row(D_MODEL), col, col, _full((N_EXPERTS, LANES))],
        out_shape=[jax.ShapeDtypeStruct((T, D_MODEL), F32), jax.ShapeDtypeStruct((T, D_MODEL), F32),
                   jax.ShapeDtypeStruct((SUBLANES, T), I32), jax.ShapeDtypeStruct((SUBLANES, T), F32),
                   jax.ShapeDtypeStruct((N_EXPERTS, LANES), F32)],
        scratch_shapes=[pltpu.VMEM((N_EXPERTS, LANES), F32)],
        compiler_params=_params(1),
        name="out_router",
    )(x2d, y_lru, y_ssd, w_out, gf, wr, br, counts_in)


def _row_copy(src_hbm, src_row, dst, dst_row, sem):
    return pltpu.make_async_copy(src_hbm.at[pl.ds(src_row, 1)], dst.at[pl.ds(dst_row, 1)], sem)


def _positions_body(offs_ref, meta_ref, pos_ref):
    m = meta_ref[...]
    base = jnp.zeros_like(m)
    for e in range(N_EXPERTS):
        base = jnp.where(m == e, offs_ref[e], base)
    pos_ref[...] = base + pltpu.roll(m, SUBLANES - 2, 0)


def _positions(offs, meta):
    T = meta.shape[1]
    tb = min(T, 4096)
    blk = pl.BlockSpec((SUBLANES, tb), lambda i, offs: (0, i))
    return pl.pallas_call(
        _positions_body,
        grid_spec=pltpu.PrefetchScalarGridSpec(num_scalar_prefetch=1, grid=(T // tb,), in_specs=[blk], out_specs=blk),
        out_shape=jax.ShapeDtypeStruct((SUBLANES, T), I32),
        compiler_params=_params(1),
        name="positions",
    )(offs, meta)


def _wait_rows(src, dst, sem):
    pltpu.make_async_copy(src, dst, sem).wait()


def _dispatch_body(zf_ref, pos_ref, *rest, tm, tme, steps):
    n_groups = len(steps)
    xn_hbms = rest[:n_groups]
    xs_hbm, zbuf, xbuf, zsem, lsem, ssem = rest[n_groups:]
    n_steps = sum(steps)
    n_tiles = xs_hbm.shape[0] // tme
    ring = xbuf.shape[0]
    step = pl.program_id(0)

    def zero_copy(i):
        return pltpu.make_async_copy(zbuf, xs_hbm.at[pl.ds(pl.multiple_of(i * tme, tme), tme)], zsem)

    def load(i, start):
        slot = i % ring
        first = 0
        for xn_hbm, n in zip(xn_hbms, steps):
            @pl.when(jnp.logical_and(i >= first, i < first + n))
            def _(xn_hbm=xn_hbm, first=first):
                rows = pl.ds(pl.multiple_of((i - first) * tm, tm), tm)
                cp = pltpu.make_async_copy(xn_hbm.at[rows], xbuf.at[slot], lsem.at[slot])
                if start:
                    cp.start()
                else:
                    cp.wait()
            first += n

    def wait_scatter(i):
        slot = i % ring
        for k in range(2):
            _wait_rows(xbuf.at[slot], xs_hbm.at[pl.ds(0, tm)], ssem.at[slot])

    @pl.when(step == 0)
    def _():
        zbuf[...] = jnp.zeros_like(zbuf)

        def start(i, c):
            @pl.when(zf_ref[i] != 0)
            def _():
                zero_copy(i).start()
            return c

        def wait(i, c):
            @pl.when(zf_ref[i] != 0)
            def _():
                zero_copy(i).wait()
            return c

        lax.fori_loop(0, n_tiles, start, 0)
        lax.fori_loop(0, n_tiles, wait, 0)
        load(step, True)

    @pl.when(step + 1 < n_steps)
    def _():
        @pl.when(step >= ring - 1)
        def _():
            wait_scatter(step - (ring - 1))
        load(step + 1, True)

    load(step, False)
    slot = step % ring
    for t in range(tm):
        for k in range(2):
            _row_copy(xbuf.at[slot], t, xs_hbm, pos_ref[k, t], ssem.at[slot]).start(priority=k)

    @pl.when(step == n_steps - 1)
    def _():
        for j in range(max(n_steps - ring, 0), n_steps):
            wait_scatter(j)


def _dispatch(zflag, pos, xns, n_rows, tm, tme):
    steps = tuple(xn.shape[0] // tm for xn in xns)
    ring = 3
    return pl.pallas_call(
        functools.partial(_dispatch_body, tm=tm, tme=tme, steps=steps),
        grid_spec=pltpu.PrefetchScalarGridSpec(
            num_scalar_prefetch=1,
            grid=(sum(steps),),
            in_specs=[pl.BlockSpec((SUBLANES, tm), lambda i, zf: (0, i), memory_space=pltpu.SMEM)]
                     + [pl.BlockSpec(memory_space=pl.ANY) for _ in xns],
            out_specs=pl.BlockSpec(memory_space=pl.ANY),
            scratch_shapes=[pltpu.VMEM((tme, D_MODEL), F32), pltpu.VMEM((ring, tm, D_MODEL), F32),
                            pltpu.SemaphoreType.DMA(()), pltpu.SemaphoreType.DMA((ring,)),
                            pltpu.SemaphoreType.DMA((ring,))],
        ),
        out_shape=jax.ShapeDtypeStruct((n_rows, D_MODEL), F32),
        compiler_params=_params(1),
        name="dispatch",
    )(zflag, pos, *xns)


def _experts_body(te_ref, na_ref, x_ref, w1_ref, w3_ref, w2_ref, o_ref, w1b, w3b, w2b):
    i = pl.program_id(0)
    changed = jnp.logical_or(i == 0, te_ref[i] != te_ref[jnp.maximum(i - 1, 0)])

    @pl.when(changed)
    def _():
        w1b[...] = w1_ref[0].astype(BF16)
        w3b[...] = w3_ref[0].astype(BF16)
        w2b[...] = w2_ref[0].astype(BF16)

    @pl.when(i < na_ref[0])
    def _():
        xb = x_ref[...].astype(BF16)
        a = jnp.dot(xb, w1b[...], preferred_element_type=F32)
        b = jnp.dot(xb, w3b[...], preferred_element_type=F32)
        hd = (jax.nn.silu(a) * b).astype(BF16)
        o_ref[...] = jnp.dot(hd, w2b[...], preferred_element_type=F32)

    @pl.when(i >= na_ref[0])
    def _():
        o_ref[...] = jnp.zeros_like(o_ref)


def _experts(tile_e, n_active, xs, w1, w3, w2, tme):
    n_rows = xs.shape[0]
    row = pl.BlockSpec((tme, D_MODEL), lambda i, te, na: (i, 0))
    return pl.pallas_call(
        _experts_body,
        grid_spec=pltpu.PrefetchScalarGridSpec(
            num_scalar_prefetch=2,
            grid=(n_rows // tme,),
            in_specs=[pl.BlockSpec((tme, D_MODEL), lambda i, te, na: (jnp.minimum(i, na[0] - 1), 0)),
                      pl.BlockSpec((1, D_MODEL, D_EXPERT), lambda i, te, na: (te[i], 0, 0)),
                      pl.BlockSpec((1, D_MODEL, D_EXPERT), lambda i, te, na: (te[i], 0, 0)),
                      pl.BlockSpec((1, D_EXPERT, D_MODEL), lambda i, te, na: (te[i], 0, 0))],
            out_specs=row,
            scratch_shapes=[pltpu.VMEM((D_MODEL, D_EXPERT), BF16), pltpu.VMEM((D_MODEL, D_EXPERT), BF16),
                            pltpu.VMEM((D_EXPERT, D_MODEL), BF16)],
        ),
        out_shape=jax.ShapeDtypeStruct((n_rows, D_MODEL), F32),
        compiler_params=_params(1),
        name="experts",
    )(tile_e, n_active, xs, w1, w3, w2)


def _combine_body(pos_ref, nxt_ref, gate_ref, h1_ref, fg_ref, ys_hbm, y_ref, ybuf, sem, *, tm):
    step = pl.program_id(0)
    slot = step % 2

    def gather(p_ref, s):
        for t in range(tm):
            for k in range(2):
                _row_copy(ys_hbm, p_ref[k, t], ybuf.at[s, k], t, sem.at[s]).start(priority=k)

    @pl.when(step == 0)
    def _():
        gather(pos_ref, 0)

    @pl.when(step + 1 < pl.num_programs(0))
    def _():
        gather(nxt_ref, 1 - slot)

    for k in range(2):
        _wait_rows(ys_hbm.at[pl.ds(0, tm)], ybuf.at[slot, k], sem.at[slot])

    eye = lax.broadcasted_iota(I32, (tm, tm), 0) == lax.broadcasted_iota(I32, (tm, tm), 1)
    g1 = jnp.sum(jnp.where(eye, gate_ref[0:1, :], 0.0), axis=1, keepdims=True)
    g2 = jnp.sum(jnp.where(eye, gate_ref[1:2, :], 0.0), axis=1, keepdims=True)
    h2 = h1_ref[...] + (g1 * ybuf[slot, 0] + g2 * ybuf[slot, 1])
    y_ref[...] = _rmsnorm(h2, fg_ref[...])


def _combine(pos, gates, h1, fg, ys, tm):
    T = h1.shape[0]
    last = T // tm - 1
    return pl.pallas_call(
        functools.partial(_combine_body, tm=tm),
        grid=(T // tm,),
        in_specs=[pl.BlockSpec((SUBLANES, tm), lambda i: (0, i), memory_space=pltpu.SMEM),
                  pl.BlockSpec((SUBLANES, tm), lambda i: (0, jnp.minimum(i + 1, last)), memory_space=pltpu.SMEM),
                  pl.BlockSpec((SUBLANES, tm), lambda i: (0, i)),
                  pl.BlockSpec((tm, D_MODEL), lambda i: (i, 0)),
                  pl.BlockSpec((1, D_MODEL), lambda i: (0, 0)),
                  pl.BlockSpec(memory_space=pl.ANY)],
        out_specs=pl.BlockSpec((tm, D_MODEL), lambda i: (i, 0)),
        scratch_shapes=[pltpu.VMEM((2, 2, tm, D_MODEL), F32), pltpu.SemaphoreType.DMA((2,))],
        out_shape=jax.ShapeDtypeStruct((T, D_MODEL), F32),
        compiler_params=_params(1),
        name="combine",
    )(pos, pos, gates, h1, fg, ys)


def _blockdiag(w):
    per = GATE_TILE // LRU_BLOCK
    w4 = w.reshape(LRU_BLOCKS // per, per, LRU_BLOCK, LRU_BLOCK)
    eye = jnp.eye(per, dtype=w.dtype)
    return jnp.einsum('jbio,bc->jbico', w4, eye).reshape(LRU_BLOCKS // per, GATE_TILE, GATE_TILE)


def _pad_rows(c):
    return jnp.pad(c, ((0, 0), (SUBLANES - (CONV_W - 1), 0), (0, 0)))


def _router_cols(group_part, expert_part):
    r = group_part.shape[0]
    out = jnp.zeros((r, LANES), F32)
    out = out.at[:, 0:MOE_GROUPS].set(group_part)
    return out.at[:, ROUTER_E0:ROUTER_E0 + N_EXPERTS].set(expert_part)


def _lane_row(v, width=LANES):
    return jnp.pad(v, (0, width - v.shape[0])).reshape(1, width)


def _prep(norm_mix_g, w_in, lru_conv_w, lru_conv_b, lru_wa, lru_ba, lru_wx, lru_bx, lru_lambda,
          ssd_conv_w, ssd_conv_b, ssd_dt_bias, ssd_a_log, ssd_d, ssd_norm_g, w_out,
          norm_ffn_g, router_group_w, router_group_b, router_expert_w, router_expert_b,
          moe_w1, moe_w3, moe_w2, final_norm_g):
    w = w_in[0]
    wr = _router_cols(router_group_w[0], router_expert_w[0])
    wr_hi = wr.astype(BF16)
    P = dict(
        g_mix=norm_mix_g[0].reshape(1, D_MODEL),
        w_lru=w[:, :2 * D_LRU].astype(BF16),
        w_ssd=jnp.pad(w[:, 2 * D_LRU:], ((0, 0), (0, LANES - SSD_HEADS))).astype(BF16),
        lru_cw=lru_conv_w[0], lru_cb=lru_conv_b[0].reshape(1, D_LRU),
        wbd=jnp.concatenate([_blockdiag(lru_wa[0]), _blockdiag(lru_wx[0])], axis=2).astype(BF16),
        ba=lru_ba[0].reshape(1, D_LRU), bx=lru_bx[0].reshape(1, D_LRU), lam=lru_lambda[0].reshape(1, D_LRU),
        ssd_cw=ssd_conv_w[0], ssd_cb=ssd_conv_b[0].reshape(1, D_XBC),
        dtb=_lane_row(ssd_dt_bias[0]), alog=_lane_row(ssd_a_log[0]),
        dvec=jnp.repeat(ssd_d[0], SSD_HEADDIM).reshape(1, D_SSM),
        ng=ssd_norm_g[0].reshape(1, D_SSM),
        w_out=w_out[0].astype(BF16),
        g_ffn=norm_ffn_g[0].reshape(1, D_MODEL),
        wr=jnp.concatenate([wr_hi, (wr - wr_hi.astype(F32)).astype(BF16)], axis=1),
        br=_router_cols(router_group_b[0][None], router_expert_b[0][None]),
        w1=moe_w1[0], w3=moe_w3[0], w2=moe_w2[0],
        g_final=final_norm_g.reshape(1, D_MODEL),
    )
    return P


def _expert_layout(counts, n_pairs, tme):
    cnt = counts[:, 0].astype(I32)
    padded = ((cnt + tme - 1) // tme) * tme
    ends = jnp.cumsum(padded)
    offs = ends - padded
    n_tiles = n_pairs // tme + N_EXPERTS
    n_active = ends[-1] // tme
    tiles = jnp.arange(n_tiles, dtype=I32)
    tile_e = jnp.sum((tiles * tme)[:, None] >= ends[None, :], axis=1).astype(I32)
    last_e = jnp.sum((n_active - 1) * tme >= ends).astype(I32)
    tile_e = jnp.where(tiles < n_active, tile_e, last_e)
    is_last = jnp.any((tiles[:, None] + 1) * tme == ends[None, :], axis=1)
    zflag = jnp.logical_or(is_last, tiles >= n_active).astype(I32)
    return offs.astype(I32), tile_e, n_active.reshape(1).astype(I32), zflag, n_tiles * tme


def _mixer_router(x, lru_h0, lru_c0, ssd_h0, ssd_c0, P, start_pos, counts_in):
    B, L, _ = x.shape
    T = B * L
    Tt = min(MIX_TILE, L)
    q = min(SSD_CHUNK, L)
    tm = min(ROW_TILE, T)
    x2d = x.reshape(T, D_MODEL)

    y_lru, lru_h, lru_c = _lru(x, P['g_mix'], P['w_lru'], _pad_rows(lru_c0), lru_h0.reshape(B, 1, D_LRU),
                               P['lru_cw'], P['lru_cb'], P['wbd'], P['ba'], P['bx'], P['lam'], Tt, start_pos)
    y_ssd, ssd_h, ssd_c = _ssd(x, P['g_mix'], P['w_ssd'], _pad_rows(ssd_c0), ssd_h0,
                               P['ssd_cw'], P['ssd_cb'], P['dtb'], P['alog'], P['dvec'], P['ng'], Tt, q)
    h1, xn, meta, gates, counts = _out_router(
        x2d, y_lru.reshape(T, D_LRU), y_ssd.reshape(T, D_SSM), P['w_out'], P['g_ffn'],
        P['wr'], P['br'], counts_in, min(MIX_TILE, T))

    hist = SUBLANES - (CONV_W - 1)
    states = (lru_h.reshape(1, B, D_LRU), lru_c[:, hist:][None], ssd_h[None], ssd_c[:, hist:][None])
    return dict(h1=h1, xn=xn, meta=meta, gates=gates, shape=(B, L, D_MODEL), tm=tm), counts, states


def _moe_final(groups, counts, P):
    tme = EXPERT_TILE
    n_pairs = 2 * sum(g['h1'].shape[0] for g in groups)
    offs, tile_e, n_active, zflag, n_rows = _expert_layout(counts, n_pairs, tme)
    tm = groups[0]['tm']
    assert all(g['tm'] == tm for g in groups)
    for g in groups:
        g['pos'] = _positions(offs, g['meta'])
    pos_all = jnp.concatenate([g['pos'] for g in groups], axis=1)
    xs = _dispatch(zflag, pos_all, [g['xn'] for g in groups], n_rows, tm, tme)
    ys = _experts(tile_e, n_active, xs, P['w1'], P['w3'], P['w2'], tme)
    return [_combine(g['pos'], g['gates'], g['h1'], P['g_final'], ys, g['tm']).reshape(g['shape'])
            for g in groups]


def kernel(x_prompt, x_sample, state_lru_h, state_lru_conv, state_ssd, state_ssd_conv, norm_mix_g, w_in, lru_conv_w, lru_conv_b, lru_wa, lru_ba, lru_wx, lru_bx, lru_lambda, ssd_conv_w, ssd_conv_b, ssd_dt_bias, ssd_a_log, ssd_d, ssd_norm_g, w_out, norm_ffn_g, router_group_w, router_group_b, router_expert_w, router_expert_b, moe_w1, moe_w3, moe_w2, final_norm_g):
    P = _prep(norm_mix_g, w_in, lru_conv_w, lru_conv_b, lru_wa, lru_ba, lru_wx, lru_bx, lru_lambda,
              ssd_conv_w, ssd_conv_b, ssd_dt_bias, ssd_a_log, ssd_d, ssd_norm_g, w_out,
              norm_ffn_g, router_group_w, router_group_b, router_expert_w, router_expert_b,
              moe_w1, moe_w3, moe_w2, final_norm_g)
    bp = x_prompt.shape[0]
    gp, counts, (a1, a2, a3, a4) = _mixer_router(
        x_prompt,
        jnp.zeros((bp, D_LRU), F32), jnp.zeros((bp, CONV_W - 1, D_LRU), F32),
        jnp.zeros((bp, SSD_HEADS, SSD_HEADDIM, D_STATE), F32), jnp.zeros((bp, CONV_W - 1, D_XBC), F32),
        P, 0, jnp.zeros((N_EXPERTS, LANES), F32))
    gs, counts, (b1, b2, b3, b4) = _mixer_router(
        x_sample, state_lru_h[0], state_lru_conv[0], state_ssd[0], state_ssd_conv[0], P, PAST_LEN, counts)
    yp, ys = _moe_final([gp, gs], counts, P)
    return (yp, ys, a1, a2, a3, a4, b1, b2, b3, b4)
```
